```python
import jax, jax.numpy as jnp
from jax import lax
import numpy as np

D_MODEL = 1024
BATCH = 8
SEQ = 4096
DEPTH = 2

D_MIX = D_MODEL
CONV_CH = D_MIX // 2
CONV_WIDTH = 31
FOX_HEADS = 8
FOX_HEAD_DIM = 64
FOX_WIDTH = FOX_HEADS * FOX_HEAD_DIM
Q_BLOCK = 128
N_MEM = 256
MEM_HEADS = 4
MEM_HEAD_DIM = 128
MEM_INNER = MEM_HEADS * MEM_HEAD_DIM
D_FF = 4 * D_MODEL
EPS = 1e-6
NEG_INF = -1e30
IN_COLS = 2 * CONV_CH + 3 * FOX_WIDTH + FOX_HEADS

kernel_name = "hymba_conformer_fox_sandwich_memory"


def rms_norm(x, g):
    xf = x.astype(jnp.float32)
    y = xf * lax.rsqrt(jnp.mean(xf * xf, axis=-1, keepdims=True) + EPS)
    return (y * g.astype(jnp.float32)).astype(x.dtype)


def layer_norm(x, g, b):
    xf = x.astype(jnp.float32)
    mu = jnp.mean(xf, axis=-1, keepdims=True)
    xc = xf - mu
    y = xc * lax.rsqrt(jnp.mean(xc * xc, axis=-1, keepdims=True) + EPS)
    return (y * g.astype(jnp.float32) + b.astype(jnp.float32)).astype(x.dtype)


def causal_depthwise_conv(u, w, b):
    out = lax.conv_general_dilated(
        u, w[:, None, :], window_strides=(1,), padding=[(CONV_WIDTH - 1, 0)],
        dimension_numbers=("NWC", "WIO", "NWC"), feature_group_count=u.shape[-1])
    return out + b


def forgetting_attention(q, k, v, log_f):
    S = q.shape[1]
    dh = q.shape[-1]
    scale = dh ** -0.5
    cum = jnp.cumsum(log_f, axis=1).transpose(0, 2, 1)
    outs = []
    for i in range(S // Q_BLOCK):
        q0 = i * Q_BLOCK
        kl = q0 + Q_BLOCK
        qb = q[:, q0:kl]
        kb = k[:, :kl]
        vb = v[:, :kl]
        logits = jnp.einsum("bqhd,bkhd->bhqk", qb, kb,
                            preferred_element_type=jnp.float32) * scale
        logits = logits + cum[:, :, q0:kl, None] - cum[:, :, None, :kl]
        q_pos = q0 + jnp.arange(Q_BLOCK)
        k_pos = jnp.arange(kl)
        mask = k_pos[None, :] <= q_pos[:, None]
        p = jax.nn.softmax(jnp.where(mask, logits, NEG_INF), axis=-1)
        outs.append(jnp.einsum("bhqk,bkhd->bqhd", p.astype(vb.dtype), vb))
    return jnp.concatenate(outs, axis=1)


def hybrid_mixer(h, w_in, b_forget, conv_w, conv_b, conv_ln_g, conv_ln_b, w_out):
    B, S, _ = h.shape
    z = h @ w_in
    o = 0
    a = z[..., o:o + CONV_CH]; o += CONV_CH
    g = z[..., o:o + CONV_CH]; o += CONV_CH
    q = z[..., o:o + FOX_WIDTH]; o += FOX_WIDTH
    k = z[..., o:o + FOX_WIDTH]; o += FOX_WIDTH
    v = z[..., o:o + FOX_WIDTH]; o += FOX_WIDTH
    f_logit = z[..., o:o + FOX_HEADS]

    u = a * jax.nn.sigmoid(g)
    u = causal_depthwise_conv(u, conv_w, conv_b)
    u = jax.nn.silu(layer_norm(u, conv_ln_g, conv_ln_b))

    log_f = jax.nn.log_sigmoid((f_logit + b_forget).astype(jnp.float32))
    shp = (B, S, FOX_HEADS, FOX_HEAD_DIM)
    att = forgetting_attention(q.reshape(shp), k.reshape(shp), v.reshape(shp), log_f)
    att = att.reshape(B, S, FOX_WIDTH)

    return jnp.concatenate([u, att], axis=-1) @ w_out


def memory_cross_attention(h, mem_n, w_mq, w_mk, w_mv, w_mo):
    B, S, _ = h.shape
    q = (h @ w_mq).reshape(B, S, MEM_HEADS, MEM_HEAD_DIM)
    k = (mem_n @ w_mk).reshape(B, N_MEM, MEM_HEADS, MEM_HEAD_DIM)
    v = (mem_n @ w_mv).reshape(B, N_MEM, MEM_HEADS, MEM_HEAD_DIM)
    logits = jnp.einsum("bqhd,bmhd->bhqm", q, k,
                        preferred_element_type=jnp.float32) * (MEM_HEAD_DIM ** -0.5)
    p = jax.nn.softmax(logits, axis=-1)
    out = jnp.einsum("bhqm,bmhd->bqhd", p.astype(v.dtype), v).reshape(B, S, MEM_INNER)
    return out @ w_mo


def squared_relu_mlp(h, w_up, w_down):
    return jnp.square(jax.nn.relu(h @ w_up)) @ w_down


def _fwd_setup_inputs(seed: int = 0) -> dict:
    key = jax.random.key(seed)
    ks = jax.random.split(key, 24)
    nrm = lambda k, shape, fan_in: jax.random.normal(k, shape, jnp.float32) * (fan_in ** -0.5)
    gain = lambda k, shape: 1.0 + 0.05 * jax.random.normal(k, shape, jnp.float32)
    small = lambda k, shape: 0.02 * jax.random.normal(k, shape, jnp.float32)
    L = DEPTH
    return {
        "x": jax.random.normal(ks[0], (BATCH, SEQ, D_MODEL), jnp.float32),
        "mem": jax.random.normal(ks[1], (BATCH, N_MEM, D_MODEL), jnp.float32),
        "norm_mix_pre": gain(ks[2], (L, D_MODEL)),
        "norm_mix_post": gain(ks[3], (L, D_MODEL)),
        "w_in": nrm(ks[4], (L, D_MODEL, IN_COLS), D_MODEL),
        "b_forget": jax.random.uniform(ks[5], (L, FOX_HEADS), jnp.float32, 1.0, 5.0),
        "conv_w": nrm(ks[6], (L, CONV_WIDTH, CONV_CH), CONV_WIDTH),
        "conv_b": small(ks[7], (L, CONV_CH)),
        "conv_ln_g": gain(ks[8], (L, CONV_CH)),
        "conv_ln_b": small(ks[9], (L, CONV_CH)),
        "w_out": nrm(ks[10], (L, D_MIX, D_MODEL), D_MIX),
        "norm_mem_pre": gain(ks[11], (L, D_MODEL)),
        "norm_mem_post": gain(ks[12], (L, D_MODEL)),
        "norm_memkv": gain(ks[13], (L, D_MODEL)),
        "w_mq": nrm(ks[14], (L, D_MODEL, MEM_INNER), D_MODEL),
        "w_mk": nrm(ks[15], (L, D_MODEL, MEM_INNER), D_MODEL),
        "w_mv": nrm(ks[16], (L, D_MODEL, MEM_INNER), D_MODEL),
        "w_mo": nrm(ks[17], (L, MEM_INNER, D_MODEL), MEM_INNER),
        "norm_mlp_pre": gain(ks[18], (L, D_MODEL)),
        "norm_mlp_post": gain(ks[19], (L, D_MODEL)),
        "w_up": nrm(ks[20], (L, D_MODEL, D_FF), D_MODEL),
        "w_down": nrm(ks[21], (L, D_FF, D_MODEL), D_FF),
    }


def _fwd_reference(x, mem, norm_mix_pre, norm_mix_post, w_in, b_forget, conv_w, conv_b,
              conv_ln_g, conv_ln_b, w_out, norm_mem_pre, norm_mem_post, norm_memkv,
              w_mq, w_mk, w_mv, w_mo, norm_mlp_pre, norm_mlp_post, w_up, w_down):
    for l in range(DEPTH):
        h = rms_norm(x, norm_mix_pre[l])
        y = hybrid_mixer(h, w_in[l], b_forget[l], conv_w[l], conv_b[l],
                         conv_ln_g[l], conv_ln_b[l], w_out[l])
        x = x + rms_norm(y, norm_mix_post[l])
        h = rms_norm(x, norm_mem_pre[l])
        mem_n = rms_norm(mem, norm_memkv[l])
        y = memory_cross_attention(h, mem_n, w_mq[l], w_mk[l], w_mv[l], w_mo[l])
        x = x + rms_norm(y, norm_mem_post[l])
        h = rms_norm(x, norm_mlp_pre[l])
        y = squared_relu_mlp(h, w_up[l], w_down[l])
        x = x + rms_norm(y, norm_mlp_post[l])
    return x


import jax as _jax
import jax.numpy as _jnp

TWIN_FORMAT = 'train_step'
FWD_PARAMS = ['x', 'mem', 'norm_mix_pre', 'norm_mix_post', 'w_in', 'b_forget', 'conv_w', 'conv_b', 'conv_ln_g', 'conv_ln_b', 'w_out', 'norm_mem_pre', 'norm_mem_post', 'norm_memkv', 'w_mq', 'w_mk', 'w_mv', 'w_mo', 'norm_mlp_pre', 'norm_mlp_post', 'w_up', 'w_down']
TWIN_WEIGHTS = ['norm_mix_pre', 'norm_mix_post', 'w_in', 'b_forget', 'conv_w', 'conv_b', 'conv_ln_g', 'conv_ln_b', 'w_out', 'norm_mem_pre', 'norm_mem_post', 'norm_memkv', 'w_mq', 'w_mk', 'w_mv', 'w_mo', 'norm_mlp_pre', 'norm_mlp_post', 'w_up', 'w_down']
TWIN_DIFF_INPUT = 'x'
TWIN_INPUTS = ['x', 'mem', 'norm_mix_pre', 'norm_mix_post', 'w_in', 'b_forget', 'conv_w', 'conv_b', 'conv_ln_g', 'conv_ln_b', 'w_out', 'norm_mem_pre', 'norm_mem_post', 'norm_memkv', 'w_mq', 'w_mk', 'w_mv', 'w_mo', 'norm_mlp_pre', 'norm_mlp_post', 'w_up', 'w_down', 'loss_target', 'm_norm_mix_pre', 'm_norm_mix_post', 'm_w_in', 'm_b_forget', 'm_conv_w', 'm_conv_b', 'm_conv_ln_g', 'm_conv_ln_b', 'm_w_out', 'm_norm_mem_pre', 'm_norm_mem_post', 'm_norm_memkv', 'm_w_mq', 'm_w_mk', 'm_w_mv', 'm_w_mo', 'm_norm_mlp_pre', 'm_norm_mlp_post', 'm_w_up', 'm_w_down', 'v_norm_mix_pre', 'v_norm_mix_post', 'v_w_in', 'v_b_forget', 'v_conv_w', 'v_conv_b', 'v_conv_ln_g', 'v_conv_ln_b', 'v_w_out', 'v_norm_mem_pre', 'v_norm_mem_post', 'v_norm_memkv', 'v_w_mq', 'v_w_mk', 'v_w_mv', 'v_w_mo', 'v_norm_mlp_pre', 'v_norm_mlp_post', 'v_w_up', 'v_w_down']
TWIN_OUTPUTS = ['loss', 'grad_x', 'grad_norm_mix_pre', 'grad_norm_mix_post', 'grad_w_in', 'grad_b_forget', 'grad_conv_w', 'grad_conv_b', 'grad_conv_ln_g', 'grad_conv_ln_b', 'grad_w_out', 'grad_norm_mem_pre', 'grad_norm_mem_post', 'grad_norm_memkv', 'grad_w_mq', 'grad_w_mk', 'grad_w_mv', 'grad_w_mo', 'grad_norm_mlp_pre', 'grad_norm_mlp_post', 'grad_w_up', 'grad_w_down', 'delta_norm_mix_pre', 'delta_norm_mix_post', 'delta_w_in', 'delta_b_forget', 'delta_conv_w', 'delta_conv_b', 'delta_conv_ln_g', 'delta_conv_ln_b', 'delta_w_out', 'delta_norm_mem_pre', 'delta_norm_mem_post', 'delta_norm_memkv', 'delta_w_mq', 'delta_w_mk', 'delta_w_mv', 'delta_w_mo', 'delta_norm_mlp_pre', 'delta_norm_mlp_post', 'delta_w_up', 'delta_w_down', 'new_m_norm_mix_pre', 'new_m_norm_mix_post', 'new_m_w_in', 'new_m_b_forget', 'new_m_conv_w', 'new_m_conv_b', 'new_m_conv_ln_g', 'new_m_conv_ln_b', 'new_m_w_out', 'new_m_norm_mem_pre', 'new_m_norm_mem_post', 'new_m_norm_memkv', 'new_m_w_mq', 'new_m_w_mk', 'new_m_w_mv', 'new_m_w_mo', 'new_m_norm_mlp_pre', 'new_m_norm_mlp_post', 'new_m_w_up', 'new_m_w_down', 'new_v_norm_mix_pre', 'new_v_norm_mix_post', 'new_v_w_in', 'new_v_b_forget', 'new_v_conv_w', 'new_v_conv_b', 'new_v_conv_ln_g', 'new_v_conv_ln_b', 'new_v_w_out', 'new_v_norm_mem_pre', 'new_v_norm_mem_post', 'new_v_norm_memkv', 'new_v_w_mq', 'new_v_w_mk', 'new_v_w_mv', 'new_v_w_mo', 'new_v_norm_mlp_pre', 'new_v_norm_mlp_post', 'new_v_w_up', 'new_v_w_down']
TWIN_LEAF_KINDS = {'loss': 'loss', 'grad_x': 'grad_x', 'grad_norm_mix_pre': 'grad_w', 'grad_norm_mix_post': 'grad_w', 'grad_w_in': 'grad_w', 'grad_b_forget': 'grad_w', 'grad_conv_w': 'grad_w', 'grad_conv_b': 'grad_w', 'grad_conv_ln_g': 'grad_w', 'grad_conv_ln_b': 'grad_w', 'grad_w_out': 'grad_w', 'grad_norm_mem_pre': 'grad_w', 'grad_norm_mem_post': 'grad_w', 'grad_norm_memkv': 'grad_w', 'grad_w_mq': 'grad_w', 'grad_w_mk': 'grad_w', 'grad_w_mv': 'grad_w', 'grad_w_mo': 'grad_w', 'grad_norm_mlp_pre': 'grad_w', 'grad_norm_mlp_post': 'grad_w', 'grad_w_up': 'grad_w', 'grad_w_down': 'grad_w', 'delta_norm_mix_pre': 'delta_w', 'delta_norm_mix_post': 'delta_w', 'delta_w_in': 'delta_w', 'delta_b_forget': 'delta_w', 'delta_conv_w': 'delta_w', 'delta_conv_b': 'delta_w', 'delta_conv_ln_g': 'delta_w', 'delta_conv_ln_b': 'delta_w', 'delta_w_out': 'delta_w', 'delta_norm_mem_pre': 'delta_w', 'delta_norm_mem_post': 'delta_w', 'delta_norm_memkv': 'delta_w', 'delta_w_mq': 'delta_w', 'delta_w_mk': 'delta_w', 'delta_w_mv': 'delta_w', 'delta_w_mo': 'delta_w', 'delta_norm_mlp_pre': 'delta_w', 'delta_norm_mlp_post': 'delta_w', 'delta_w_up': 'delta_w', 'delta_w_down': 'delta_w', 'new_m_norm_mix_pre': 'new_m', 'new_m_norm_mix_post': 'new_m', 'new_m_w_in': 'new_m', 'new_m_b_forget': 'new_m', 'new_m_conv_w': 'new_m', 'new_m_conv_b': 'new_m', 'new_m_conv_ln_g': 'new_m', 'new_m_conv_ln_b': 'new_m', 'new_m_w_out': 'new_m', 'new_m_norm_mem_pre': 'new_m', 'new_m_norm_mem_post': 'new_m', 'new_m_norm_memkv': 'new_m', 'new_m_w_mq': 'new_m', 'new_m_w_mk': 'new_m', 'new_m_w_mv': 'new_m', 'new_m_w_mo': 'new_m', 'new_m_norm_mlp_pre': 'new_m', 'new_m_norm_mlp_post': 'new_m', 'new_m_w_up': 'new_m', 'new_m_w_down': 'new_m', 'new_v_norm_mix_pre': 'new_v', 'new_v_norm_mix_post': 'new_v', 'new_v_w_in': 'new_v', 'new_v_b_forget': 'new_v', 'new_v_conv_w': 'new_v', 'new_v_conv_b': 'new_v', 'new_v_conv_ln_g': 'new_v', 'new_v_conv_ln_b': 'new_v', 'new_v_w_out': 'new_v', 'new_v_norm_mem_pre': 'new_v', 'new_v_norm_mem_post': 'new_v', 'new_v_norm_memkv': 'new_v', 'new_v_w_mq': 'new_v', 'new_v_w_mk': 'new_v', 'new_v_w_mv': 'new_v', 'new_v_w_mo': 'new_v', 'new_v_norm_mlp_pre': 'new_v', 'new_v_norm_mlp_post': 'new_v', 'new_v_w_up': 'new_v', 'new_v_w_down': 'new_v'}


def _forward(args):
    return _fwd_reference(*[args[k] for k in FWD_PARAMS])


def _output_shape():
    out = _jax.eval_shape(lambda: _forward(_fwd_setup_inputs(0)))
    return out.shape, out.dtype

N_MICROBATCH = 1
ADAM_LR = 0.001
ADAM_B1 = 0.9
ADAM_B2 = 0.999
ADAM_EPS = 1e-08
ADAM_WD = 0.01
ADAM_STEP = 10
PER_EXAMPLE_BATCH_AXIS = {'x': 0, 'mem': 0, 'loss_target': 0}
SHARED_INPUTS = []
_WEIGHT_DTYPES = {'norm_mix_pre': _jnp.float32, 'norm_mix_post': _jnp.float32, 'w_in': _jnp.float32, 'b_forget': _jnp.float32, 'conv_w': _jnp.float32, 'conv_b': _jnp.float32, 'conv_ln_g': _jnp.float32, 'conv_ln_b': _jnp.float32, 'w_out': _jnp.float32, 'norm_mem_pre': _jnp.float32, 'norm_mem_post': _jnp.float32, 'norm_memkv': _jnp.float32, 'w_mq': _jnp.float32, 'w_mk': _jnp.float32, 'w_mv': _jnp.float32, 'w_mo': _jnp.float32, 'norm_mlp_pre': _jnp.float32, 'norm_mlp_post': _jnp.float32, 'w_up': _jnp.float32, 'w_down': _jnp.float32}
MOMENT_SCALE = {'norm_mix_pre': 1.025264e+01, 'norm_mix_post': 3.589686e+01, 'w_in': 6.198954e+00, 'b_forget': 2.076434e+00, 'conv_w': 6.476032e+00, 'conv_b': 6.922379e+01, 'conv_ln_g': 2.754771e+01, 'conv_ln_b': 3.879388e+01, 'w_out': 1.478794e+01, 'norm_mem_pre': 5.415782e+00, 'norm_mem_post': 3.651262e+01, 'norm_memkv': 1.747475e+01, 'w_mq': 7.560895e+00, 'w_mk': 7.876411e+00, 'w_mv': 2.174568e+01, 'w_mo': 1.537081e+01, 'norm_mlp_pre': 8.656570e+00, 'norm_mlp_post': 3.581155e+01, 'w_up': 4.280795e+00, 'w_down': 1.561861e+01}


def _to_microbatches(a, axis):
    t = _jnp.moveaxis(a, axis, 0)
    t = t.reshape((N_MICROBATCH, t.shape[0] // N_MICROBATCH) + t.shape[1:])
    return _jnp.moveaxis(t, 1, axis + 1)


def setup_inputs(seed: int = 0) -> dict:
    inp = _fwd_setup_inputs(seed)
    key = _jax.random.fold_in(_jax.random.key(seed), 7919)
    shape, _ = _output_shape()
    out = dict(inp)
    out["loss_target"] = _jax.random.normal(_jax.random.fold_in(key, 0), shape, _jnp.float32)
    for i, name in enumerate(TWIN_WEIGHTS):
        w = inp[name].astype(_jnp.float32)
        if MOMENT_SCALE is None:
            s = _jnp.sqrt(_jnp.mean(_jnp.square(w)) + 1e-30)
        else:
            s = MOMENT_SCALE[name]
        km, kv = _jax.random.split(_jax.random.fold_in(key, i + 1))
        out[name] = w
        out["m_" + name] = s * _jax.random.normal(km, w.shape, _jnp.float32)
        out["v_" + name] = (s * s) * _jax.random.uniform(kv, w.shape, _jnp.float32, 0.5, 1.5)
    if N_MICROBATCH > 1:
        for name, axis in PER_EXAMPLE_BATCH_AXIS.items():
            out[name] = _to_microbatches(out[name], axis)
    return {'x': out['x'], 'mem': out['mem'], 'norm_mix_pre': out['norm_mix_pre'], 'norm_mix_post': out['norm_mix_post'], 'w_in': out['w_in'], 'b_forget': out['b_forget'], 'conv_w': out['conv_w'], 'conv_b': out['conv_b'], 'conv_ln_g': out['conv_ln_g'], 'conv_ln_b': out['conv_ln_b'], 'w_out': out['w_out'], 'norm_mem_pre': out['norm_mem_pre'], 'norm_mem_post': out['norm_mem_post'], 'norm_memkv': out['norm_memkv'], 'w_mq': out['w_mq'], 'w_mk': out['w_mk'], 'w_mv': out['w_mv'], 'w_mo': out['w_mo'], 'norm_mlp_pre': out['norm_mlp_pre'], 'norm_mlp_post': out['norm_mlp_post'], 'w_up': out['w_up'], 'w_down': out['w_down'], 'loss_target': out['loss_target'], 'm_norm_mix_pre': out['m_norm_mix_pre'], 'm_norm_mix_post': out['m_norm_mix_post'], 'm_w_in': out['m_w_in'], 'm_b_forget': out['m_b_forget'], 'm_conv_w': out['m_conv_w'], 'm_conv_b': out['m_conv_b'], 'm_conv_ln_g': out['m_conv_ln_g'], 'm_conv_ln_b': out['m_conv_ln_b'], 'm_w_out': out['m_w_out'], 'm_norm_mem_pre': out['m_norm_mem_pre'], 'm_norm_mem_post': out['m_norm_mem_post'], 'm_norm_memkv': out['m_norm_memkv'], 'm_w_mq': out['m_w_mq'], 'm_w_mk': out['m_w_mk'], 'm_w_mv': out['m_w_mv'], 'm_w_mo': out['m_w_mo'], 'm_norm_mlp_pre': out['m_norm_mlp_pre'], 'm_norm_mlp_post': out['m_norm_mlp_post'], 'm_w_up': out['m_w_up'], 'm_w_down': out['m_w_down'], 'v_norm_mix_pre': out['v_norm_mix_pre'], 'v_norm_mix_post': out['v_norm_mix_post'], 'v_w_in': out['v_w_in'], 'v_b_forget': out['v_b_forget'], 'v_conv_w': out['v_conv_w'], 'v_conv_b': out['v_conv_b'], 'v_conv_ln_g': out['v_conv_ln_g'], 'v_conv_ln_b': out['v_conv_ln_b'], 'v_w_out': out['v_w_out'], 'v_norm_mem_pre': out['v_norm_mem_pre'], 'v_norm_mem_post': out['v_norm_mem_post'], 'v_norm_memkv': out['v_norm_memkv'], 'v_w_mq': out['v_w_mq'], 'v_w_mk': out['v_w_mk'], 'v_w_mv': out['v_w_mv'], 'v_w_mo': out['v_w_mo'], 'v_norm_mlp_pre': out['v_norm_mlp_pre'], 'v_norm_mlp_post': out['v_norm_mlp_post'], 'v_w_up': out['v_w_up'], 'v_w_down': out['v_w_down']}


def _loss(weights, diff, rest, loss_target):
    with _jax.named_scope("forward"):
        args = {**rest, TWIN_DIFF_INPUT: diff, **{k: w.astype(_WEIGHT_DTYPES[k]) for k, w in weights.items()}}
        y = _forward(args)
    with _jax.named_scope("loss_head"):
        err = _jnp.square(y.astype(_jnp.float32) - loss_target)
        return 0.5 * _jnp.sum(_jnp.mean(err, axis=-1)) if err.ndim else 0.5 * err


def _adamw(w, g, m, v):
    m = ADAM_B1 * m + (1.0 - ADAM_B1) * g
    v = ADAM_B2 * v + (1.0 - ADAM_B2) * _jnp.square(g)
    m_hat = m / (1.0 - ADAM_B1 ** ADAM_STEP)
    v_hat = v / (1.0 - ADAM_B2 ** ADAM_STEP)
    delta = -ADAM_LR * (m_hat / (_jnp.sqrt(v_hat) + ADAM_EPS) + ADAM_WD * w)
    return delta, m, v


def reference(x, mem, norm_mix_pre, norm_mix_post, w_in, b_forget, conv_w, conv_b, conv_ln_g, conv_ln_b, w_out, norm_mem_pre, norm_mem_post, norm_memkv, w_mq, w_mk, w_mv, w_mo, norm_mlp_pre, norm_mlp_post, w_up, w_down, loss_target, m_norm_mix_pre, m_norm_mix_post, m_w_in, m_b_forget, m_conv_w, m_conv_b, m_conv_ln_g, m_conv_ln_b, m_w_out, m_norm_mem_pre, m_norm_mem_post, m_norm_memkv, m_w_mq, m_w_mk, m_w_mv, m_w_mo, m_norm_mlp_pre, m_norm_mlp_post, m_w_up, m_w_down, v_norm_mix_pre, v_norm_mix_post, v_w_in, v_b_forget, v_conv_w, v_conv_b, v_conv_ln_g, v_conv_ln_b, v_w_out, v_norm_mem_pre, v_norm_mem_post, v_norm_memkv, v_w_mq, v_w_mk, v_w_mv, v_w_mo, v_norm_mlp_pre, v_norm_mlp_post, v_w_up, v_w_down):
    given = dict(x=x, mem=mem, norm_mix_pre=norm_mix_pre, norm_mix_post=norm_mix_post, w_in=w_in, b_forget=b_forget, conv_w=conv_w, conv_b=conv_b, conv_ln_g=conv_ln_g, conv_ln_b=conv_ln_b, w_out=w_out, norm_mem_pre=norm_mem_pre, norm_mem_post=norm_mem_post, norm_memkv=norm_memkv, w_mq=w_mq, w_mk=w_mk, w_mv=w_mv, w_mo=w_mo, norm_mlp_pre=norm_mlp_pre, norm_mlp_post=norm_mlp_post, w_up=w_up, w_down=w_down, loss_target=loss_target, m_norm_mix_pre=m_norm_mix_pre, m_norm_mix_post=m_norm_mix_post, m_w_in=m_w_in, m_b_forget=m_b_forget, m_conv_w=m_conv_w, m_conv_b=m_conv_b, m_conv_ln_g=m_conv_ln_g, m_conv_ln_b=m_conv_ln_b, m_w_out=m_w_out, m_norm_mem_pre=m_norm_mem_pre, m_norm_mem_post=m_norm_mem_post, m_norm_memkv=m_norm_memkv, m_w_mq=m_w_mq, m_w_mk=m_w_mk, m_w_mv=m_w_mv, m_w_mo=m_w_mo, m_norm_mlp_pre=m_norm_mlp_pre, m_norm_mlp_post=m_norm_mlp_post, m_w_up=m_w_up, m_w_down=m_w_down, v_norm_mix_pre=v_norm_mix_pre, v_norm_mix_post=v_norm_mix_post, v_w_in=v_w_in, v_b_forget=v_b_forget, v_conv_w=v_conv_w, v_conv_b=v_conv_b, v_conv_ln_g=v_conv_ln_g, v_conv_ln_b=v_conv_ln_b, v_w_out=v_w_out, v_norm_mem_pre=v_norm_mem_pre, v_norm_mem_post=v_norm_mem_post, v_norm_memkv=v_norm_memkv, v_w_mq=v_w_mq, v_w_mk=v_w_mk, v_w_mv=v_w_mv, v_w_mo=v_w_mo, v_norm_mlp_pre=v_norm_mlp_pre, v_norm_mlp_post=v_norm_mlp_post, v_w_up=v_w_up, v_w_down=v_w_down)
    weights = {n: given[n] for n in TWIN_WEIGHTS}
    shared = {n: given[n] for n in SHARED_INPUTS}
    per_example = {n: given[n] for n in ['x', 'mem']}
    grad_fn = _jax.value_and_grad(_loss, argnums=(0, 1))

    def one_microbatch(ex, loss_target):
        ex = dict(ex)
        diff = ex.pop(TWIN_DIFF_INPUT)
        return grad_fn(weights, diff, {**shared, **ex}, loss_target)

    if N_MICROBATCH == 1:
        loss, (grad_w, grad_x) = one_microbatch(per_example, given["loss_target"])
    else:
        def body(carry, xs):
            loss_sum, grad_sum = carry
            l_k, (gw_k, gx_k) = one_microbatch(xs[0], xs[1])
            with _jax.named_scope("update"):
                return (loss_sum + l_k, _jax.tree.map(_jnp.add, grad_sum, gw_k)), gx_k

        init = (_jnp.zeros((), _jnp.float32), _jax.tree.map(_jnp.zeros_like, weights))
        (loss, grad_w), grad_x = _jax.lax.scan(body, init, (per_example, given["loss_target"]))
    with _jax.named_scope("update"):
        delta_w, new_m, new_v = {}, {}, {}
        for n in TWIN_WEIGHTS:
            delta_w[n], new_m[n], new_v[n] = _adamw(weights[n], grad_w[n], given["m_" + n], given["v_" + n])
    return (loss, grad_x, *[grad_w[n] for n in TWIN_WEIGHTS], *[delta_w[n] for n in TWIN_WEIGHTS],
            *[new_m[n] for n in TWIN_WEIGHTS], *[new_v[n] for n in TWIN_WEIGHTS])
```

```python
import functools

import jax
import jax.numpy as jnp
from jax import lax
from jax.experimental import pallas as pl
from jax.experimental.pallas import tpu as pltpu

F32, BF16 = jnp.float32, jnp.bfloat16
D_MODEL = 1024
CONV_CH = 512
CONV_WIDTH = 31
CONV_HALO = 32
FOX_WIDTH = 512
FOX_HEADS = 8
N_MEM = 256
MEM_HEADS = 4
MEM_HEAD_DIM = 128
MEM_INNER = 512
D_FF = 4096
IN_COLS = 2568
IN_PAD = 2688
EPS = 1e-6
NEG_INF = -1e30
FOX_SCALE = 0.125
MEM_SCALE = MEM_HEAD_DIM ** -0.5
ADAM_LR, ADAM_B1, ADAM_B2, ADAM_EPS, ADAM_WD, ADAM_STEP = 0.001, 0.9, 0.999, 1e-08, 0.01, 10
VMEM_LIMIT = 56 * 1024 * 1024
ROW_TILE = 512
MESH = pl.DeviceIdType.MESH
N_CHIPS = 4

BIG = ("w_in", "conv_w", "w_out", "w_mq", "w_mk", "w_mv", "w_mo", "w_up", "w_down")
NORMS = ("norm_mix_pre", "norm_mix_post", "norm_mem_pre", "norm_mem_post", "norm_memkv", "norm_mlp_pre", "norm_mlp_post")
SMALL = NORMS + ("conv_b", "conv_ln_g", "conv_ln_b", "b_forget")
WEIGHTS = ("norm_mix_pre", "norm_mix_post", "w_in", "b_forget", "conv_w", "conv_b", "conv_ln_g", "conv_ln_b", "w_out",
           "norm_mem_pre", "norm_mem_post", "norm_memkv", "w_mq", "w_mk", "w_mv", "w_mo", "norm_mlp_pre", "norm_mlp_post",
           "w_up", "w_down")
SMALL_ROWS = 24


def _params(*sem):
    return pltpu.CompilerParams(dimension_semantics=sem, vmem_limit_bytes=VMEM_LIMIT)


def rms_fwd(x, g, name):
    m, d = x.shape
    tm = min(m, ROW_TILE)

    def body(x_ref, g_ref, h_ref):
        xv = x_ref[...]
        r = lax.rsqrt(jnp.mean(xv * xv, axis=-1, keepdims=True) + EPS)
        h_ref[...] = ((xv * r) * g_ref[...]).astype(BF16)

    return pl.pallas_call(
        body, name=name, grid=(m // tm,),
        in_specs=[pl.BlockSpec((tm, d), lambda i: (i, 0)), pl.BlockSpec((1, d), lambda i: (0, 0))],
        out_specs=pl.BlockSpec((tm, d), lambda i: (i, 0)),
        out_shape=jax.ShapeDtypeStruct((m, d), BF16),
        compiler_params=_params("parallel"))(x, g)


def resid_post(x, y, g, name):
    m, d = x.shape
    tm = min(m, ROW_TILE)

    def body(x_ref, y_ref, g_ref, o_ref):
        yv = y_ref[...]
        r = lax.rsqrt(jnp.mean(yv * yv, axis=-1, keepdims=True) + EPS)
        o_ref[...] = x_ref[...] + (yv * r) * g_ref[...]

    row = pl.BlockSpec((tm, d), lambda i: (i, 0))
    return pl.pallas_call(
        body, name=name, grid=(m // tm,),
        in_specs=[row, row, pl.BlockSpec((1, d), lambda i: (0, 0))],
        out_specs=row, out_shape=jax.ShapeDtypeStruct((m, d), F32),
        compiler_params=_params("parallel"))(x, y, g)


def rms_bwd(y, g, dout, name, add=None, out_dtype=F32):
    m, d = y.shape
    tm = min(m, ROW_TILE)
    has_add = add is not None

    def body(*refs):
        y_ref, g_ref, d_ref = refs[:3]
        dy_ref, dg_ref = refs[-2:]
        i = pl.program_id(0)
        yv = y_ref[...]
        dv = d_ref[...].astype(F32)
        r = lax.rsqrt(jnp.mean(yv * yv, axis=-1, keepdims=True) + EPS)
        gy = g_ref[...] * dv
        t = jnp.mean(yv * gy, axis=-1, keepdims=True) * (r * r)
        dy = r * (gy - yv * t)
        if has_add:
            dy = dy + refs[3][...]
        dy_ref[...] = dy.astype(dy_ref.dtype)
        part = jnp.sum(dv * (yv * r), axis=0, keepdims=True)

        @pl.when(i == 0)
        def _():
            dg_ref[...] = part

        @pl.when(i > 0)
        def _():
            dg_ref[...] += part

    row = pl.BlockSpec((tm, d), lambda i: (i, 0))
    vec = pl.BlockSpec((1, d), lambda i: (0, 0))
    args = [y, g, dout] + ([add] if has_add else [])
    return pl.pallas_call(
        body, name=name, grid=(m // tm,),
        in_specs=[row, vec, row] + ([row] if has_add else []),
        out_specs=[row, vec],
        out_shape=[jax.ShapeDtypeStruct((m, d), out_dtype), jax.ShapeDtypeStruct((1, d), F32)],
        compiler_params=_params("arbitrary"))(*args)


def loss_head(xf, tgt, name):
    m, d = xf.shape
    tm = min(m, ROW_TILE)

    def body(x_ref, t_ref, s_ref, dx_ref):
        i = pl.program_id(0)
        err = x_ref[...] - t_ref[...]
        dx_ref[...] = err * (1.0 / d)
        part = jnp.sum(err * err, axis=0, keepdims=True)

        @pl.when(i == 0)
        def _():
            s_ref[...] = part

        @pl.when(i > 0)
        def _():
            s_ref[...] += part

    row = pl.BlockSpec((tm, d), lambda i: (i, 0))
    vec = pl.BlockSpec((1, d), lambda i: (0, 0))
    return pl.pallas_call(
        body, name=name, grid=(m // tm,), in_specs=[row, row], out_specs=[vec, row],
        out_shape=[jax.ShapeDtypeStruct((1, d), F32), jax.ShapeDtypeStruct((m, d), F32)],
        compiler_params=_params("arbitrary"))(xf, tgt)


_DOT_DIMS = {"nn": (((1,), (0,)), ((), ())), "nt": (((1,), (1,)), ((), ())), "tn": (((0,), (0,)), ((), ()))}


def matmul(a, b, *, mode, dims, tiles, name, out_shapes, out_specs=None, epi=None, extras=(), extra_specs=(),
           a_spec=None, b_spec=None):
    m, n, k = dims
    tm, tn, tk = tiles
    nk = k // tk
    assert m % tm == 0 and n % tn == 0 and k % tk == 0
    n_ex, n_out = len(extras), len(out_shapes)
    if a_spec is None:
        a_spec = pl.BlockSpec((tk, tm), lambda i, j, kk: (kk, i)) if mode == "tn" else pl.BlockSpec((tm, tk), lambda i, j, kk: (i, kk))
    if b_spec is None:
        b_spec = pl.BlockSpec((tn, tk), lambda i, j, kk: (j, kk)) if mode == "nt" else pl.BlockSpec((tk, tn), lambda i, j, kk: (kk, j))
    if out_specs is None:
        out_specs = [pl.BlockSpec((tm, tn), lambda i, j, kk: (i, j)) for _ in out_shapes]
    if epi is None:
        epi = lambda acc: (acc,)

    def body(*refs):
        a_ref, b_ref = refs[0], refs[1]
        ex = refs[2:2 + n_ex]
        outs = refs[2 + n_ex:2 + n_ex + n_out]
        part = lax.dot_general(a_ref[...], b_ref[...], _DOT_DIMS[mode], preferred_element_type=F32)

        def finish(acc):
            for o_ref, val in zip(outs, epi(acc, *ex)):
                o_ref[...] = val.astype(o_ref.dtype)

        if nk == 1:
            finish(part)
        else:
            acc_ref = refs[-1]
            kk = pl.program_id(2)

            @pl.when(kk == 0)
            def _():
                acc_ref[...] = part

            @pl.when(kk > 0)
            def _():
                acc_ref[...] += part

            @pl.when(kk == nk - 1)
            def _():
                finish(acc_ref[...])

    return pl.pallas_call(
        body, name=name, grid=(m // tm, n // tn, nk),
        in_specs=[a_spec, b_spec] + list(extra_specs),
        out_specs=list(out_specs), out_shape=list(out_shapes),
        scratch_shapes=[pltpu.VMEM((tm, tn), F32)] if nk > 1 else [],
        compiler_params=_params("parallel", "parallel", "arbitrary"))(a, b, *extras)


def _sds(shape, dtype):
    return jax.ShapeDtypeStruct(shape, dtype)


def _sigmoid(v):
    return 1.0 / (1.0 + jnp.exp(-v))


def conv_fwd(ag, cw, cb, lg, lb, name):
    s = ag.shape[0]
    tm = min(s, ROW_TILE)
    c = CONV_CH
    hb = tm // CONV_HALO

    def body(ag_ref, halo_ref, w_ref, cb_ref, lg_ref, lb_ref, u1_ref, u_ref, ext_ref):
        i = pl.program_id(0)
        u0 = ag_ref[:, :c] * _sigmoid(ag_ref[:, c:])
        h0 = halo_ref[:, :c] * _sigmoid(halo_ref[:, c:])
        ext_ref[0:CONV_HALO, :] = jnp.where(i == 0, 0.0, h0)
        ext_ref[CONV_HALO:, :] = u0
        acc = jnp.zeros((tm, c), F32) + cb_ref[...]
        for k in range(CONV_WIDTH):
            off = CONV_HALO - (CONV_WIDTH - 1) + k
            acc = acc + w_ref[k:k + 1, :] * ext_ref[off:off + tm, :]
        u1_ref[...] = acc
        mu = jnp.mean(acc, axis=-1, keepdims=True)
        xc = acc - mu
        rstd = lax.rsqrt(jnp.mean(xc * xc, axis=-1, keepdims=True) + EPS)
        u2 = (xc * rstd) * lg_ref[...] + lb_ref[...]
        u_ref[...] = (u2 * _sigmoid(u2)).astype(BF16)

    vec = pl.BlockSpec((1, c), lambda i: (0, 0))
    return pl.pallas_call(
        body, name=name, grid=(s // tm,),
        in_specs=[pl.BlockSpec((tm, 2 * c), lambda i: (i, 0)),
                  pl.BlockSpec((CONV_HALO, 2 * c), lambda i: (jnp.maximum(i * hb - 1, 0), 0)),
                  pl.BlockSpec((CONV_HALO, c), lambda i: (0, 0)), vec, vec, vec],
        out_specs=[pl.BlockSpec((tm, c), lambda i: (i, 0)), pl.BlockSpec((tm, c), lambda i: (i, 0))],
        out_shape=[_sds((s, c), F32), _sds((s, c), BF16)],
        scratch_shapes=[pltpu.VMEM((tm + CONV_HALO, c), F32)],
        compiler_params=_params("parallel"))(ag, ag, cw, cb, lg, lb)


def conv_bwd(du, u1, ag, cw, lg, lb, name):
    s = du.shape[0]
    tm = min(s, ROW_TILE)
    c = CONV_CH
    hb = tm // CONV_HALO
    nt = s // tm
    last_halo = s // CONV_HALO - 1

    def ln_silu_bwd(du_v, u1_v, lg_v, lb_v):
        mu = jnp.mean(u1_v, axis=-1, keepdims=True)
        xc = u1_v - mu
        rstd = lax.rsqrt(jnp.mean(xc * xc, axis=-1, keepdims=True) + EPS)
        xh = xc * rstd
        u2 = xh * lg_v + lb_v
        sg = _sigmoid(u2)
        du2 = du_v * (sg * (1.0 + u2 * (1.0 - sg)))
        dxh = du2 * lg_v
        du1 = rstd * (dxh - jnp.mean(dxh, axis=-1, keepdims=True) - xh * jnp.mean(dxh * xh, axis=-1, keepdims=True))
        return du1, du2, xh

    def body(du_ref, dun_ref, u1_ref, u1n_ref, ag_ref, agp_ref, w_ref, lg_ref, lb_ref, dag_ref, sm_ref, ext_ref, dext_ref):
        i = pl.program_id(0)
        lg_v, lb_v = lg_ref[...], lb_ref[...]
        du1, du2, xh = ln_silu_bwd(du_ref[...], u1_ref[...], lg_v, lb_v)
        du1n, _, _ = ln_silu_bwd(dun_ref[...], u1n_ref[...], lg_v, lb_v)
        dext_ref[0:tm, :] = du1
        dext_ref[tm:, :] = jnp.where(i == nt - 1, 0.0, du1n)
        a, g = ag_ref[:, :c], ag_ref[:, c:]
        sg = _sigmoid(g)
        ext_ref[0:CONV_HALO, :] = jnp.where(i == 0, 0.0, agp_ref[:, :c] * _sigmoid(agp_ref[:, c:]))
        ext_ref[CONV_HALO:, :] = a * sg

        @pl.when(i == 0)
        def _():
            sm_ref[...] = jnp.zeros_like(sm_ref)

        du0 = jnp.zeros((tm, c), F32)
        for k in range(CONV_WIDTH):
            back = CONV_WIDTH - 1 - k
            du0 = du0 + w_ref[k:k + 1, :] * dext_ref[back:back + tm, :]
            off = CONV_HALO - (CONV_WIDTH - 1) + k
            sm_ref[k:k + 1, :] += jnp.sum(du1 * ext_ref[off:off + tm, :], axis=0, keepdims=True)
        sm_ref[32:33, :] += jnp.sum(du1, axis=0, keepdims=True)
        sm_ref[33:34, :] += jnp.sum(du2 * xh, axis=0, keepdims=True)
        sm_ref[34:35, :] += jnp.sum(du2, axis=0, keepdims=True)
        dag_ref[:, :c] = (du0 * sg).astype(BF16)
        dag_ref[:, c:] = (du0 * a * (sg * (1.0 - sg))).astype(BF16)

    vec = pl.BlockSpec((1, c), lambda i: (0, 0))
    tile = pl.BlockSpec((tm, c), lambda i: (i, 0))
    nxt = pl.BlockSpec((CONV_HALO, c), lambda i: (jnp.minimum((i + 1) * hb, last_halo), 0))
    return pl.pallas_call(
        body, name=name, grid=(nt,),
        in_specs=[tile, nxt, tile, nxt,
                  pl.BlockSpec((tm, 2 * c), lambda i: (i, 0)),
                  pl.BlockSpec((CONV_HALO, 2 * c), lambda i: (jnp.maximum(i * hb - 1, 0), 0)),
                  pl.BlockSpec((CONV_HALO, c), lambda i: (0, 0)), vec, vec],
        out_specs=[pl.BlockSpec((tm, 2 * c), lambda i: (i, 0)), pl.BlockSpec((40, c), lambda i: (0, 0))],
        out_shape=[_sds((s, 2 * c), BF16), _sds((40, c), F32)],
        scratch_shapes=[pltpu.VMEM((tm + CONV_HALO, c), F32), pltpu.VMEM((tm + CONV_HALO, c), F32)],
        compiler_params=_params("arbitrary"))(du, du, u1, u1, ag, ag, cw, lg, lb)


CUM_BLOCK = 256


def _tri(n, upper):
    r = lax.broadcasted_iota(jnp.int32, (n, n), 0)
    cidx = lax.broadcasted_iota(jnp.int32, (n, n), 1)
    return jnp.where((r <= cidx) if upper else (r >= cidx), 1.0, 0.0).astype(F32)


def fox_gate_fwd(fl_t, bf, name):
    h, s = fl_t.shape
    nb = s // CUM_BLOCK

    def body(fl_ref, bf_ref, cum_ref):
        tri = _tri(CUM_BLOCK, True)
        carry = jnp.zeros((h, 1), F32)
        for b in range(nb):
            v = fl_ref[:, b * CUM_BLOCK:(b + 1) * CUM_BLOCK] + bf_ref[...]
            logf = jnp.minimum(v, 0.0) - jnp.log(1.0 + jnp.exp(-jnp.abs(v)))
            cs = jnp.dot(logf, tri, precision=lax.Precision.HIGHEST, preferred_element_type=F32) + carry
            cum_ref[:, b * CUM_BLOCK:(b + 1) * CUM_BLOCK] = cs
            carry = carry + jnp.sum(logf, axis=-1, keepdims=True)

    return pl.pallas_call(body, name=name, out_shape=_sds((h, s), F32),
                          compiler_params=pltpu.CompilerParams(vmem_limit_bytes=VMEM_LIMIT))(fl_t, bf)


def fox_gate_bwd(dcol_t, drow_t, fl_t, bf, name):
    h, s = fl_t.shape
    nb = s // CUM_BLOCK

    def body(dcol_ref, drow_ref, fl_ref, bf_ref, dfl_ref, dbf_ref):
        tri = _tri(CUM_BLOCK, False)
        carry = jnp.zeros((h, 1), F32)
        dbf = jnp.zeros((h, 1), F32)
        for b in reversed(range(nb)):
            sl = slice(b * CUM_BLOCK, (b + 1) * CUM_BLOCK)
            dcb = dcol_ref[:, sl] + drow_ref[:, sl]
            dlogf = jnp.dot(dcb, tri, precision=lax.Precision.HIGHEST, preferred_element_type=F32) + carry
            carry = carry + jnp.sum(dcb, axis=-1, keepdims=True)
            dfl = dlogf * _sigmoid(-(fl_ref[:, sl] + bf_ref[...]))
            dfl_ref[:, sl] = dfl
            dbf = dbf + jnp.sum(dfl, axis=-1, keepdims=True)
        dbf_ref[...] = dbf

    return pl.pallas_call(body, name=name, out_shape=[_sds((h, s), F32), _sds((h, 1), F32)],
                          compiler_params=pltpu.CompilerParams(vmem_limit_bytes=VMEM_LIMIT))(dcol_t, drow_t, fl_t, bf)


FOX_TILE = 512


def _causal_mask(tq, tk, q0, k0):
    row = lax.broadcasted_iota(jnp.int32, (tq, tk), 0) + q0
    col = lax.broadcasted_iota(jnp.int32, (tq, tk), 1) + k0
    return row >= col


def _fox_bias(c_ref, hh, q0, k0, t):
    c_q = jnp.max(c_ref[hh:hh + 1, pl.ds(q0, 128)], axis=-1, keepdims=True)
    return c_q - c_ref[hh:hh + 1, pl.ds(k0, t)]


def fox_fwd(qkv, cum4, name):
    s = qkv.shape[0]
    t = min(s, FOX_TILE)
    nq = s // t
    dn = _DOT_DIMS["nt"]

    def body(q_ref, k_ref, v_ref, c_ref, o_ref, lse_ref):
        i = pl.program_id(1)
        q0 = pl.multiple_of(i * t, t)
        lane = lax.broadcasted_iota(jnp.int32, (t, 128), 1)
        qv = q_ref[...]
        zero = jnp.zeros_like(qv)
        q_h = (jnp.where(lane < 64, qv, zero), jnp.where(lane >= 64, qv, zero))

        def step(j, carry, masked):
            k0 = pl.multiple_of(j * t, t)
            kj = k_ref[pl.ds(k0, t), :]
            vj = v_ref[pl.ds(k0, t), :]
            out = []
            for hh in range(2):
                m_prev, l_prev, acc_prev = carry[hh]
                bias = _fox_bias(c_ref, hh, q0, k0, t)
                sc = lax.dot_general(q_h[hh], kj, dn, preferred_element_type=F32) * FOX_SCALE + bias
                if masked:
                    sc = jnp.where(_causal_mask(t, t, q0, k0), sc, NEG_INF)
                m_new = jnp.maximum(m_prev, jnp.max(sc, axis=-1, keepdims=True))
                alpha = jnp.exp(m_prev - m_new)
                p = jnp.exp(sc - m_new)
                l_new = alpha * l_prev + jnp.sum(p, axis=-1, keepdims=True)
                acc_new = alpha * acc_prev + jnp.dot(p.astype(BF16), vj, preferred_element_type=F32)
                out.append((m_new, l_new, acc_new))
            return tuple(out)

        init = tuple((jnp.full((t, 1), NEG_INF, F32), jnp.zeros((t, 1), F32), jnp.zeros((t, 128), F32)) for _ in range(2))
        carry = lax.fori_loop(0, i, lambda j, cr: step(j, cr, False), init)
        carry = step(i, carry, True)
        (m_a, l_a, acc_a), (m_b, l_b, acc_b) = carry
        o_ref[...] = jnp.where(lane < 64, acc_a / l_a, acc_b / l_b).astype(BF16)
        lse_ref[0] = jnp.broadcast_to(m_a + jnp.log(l_a), (t, 128))
        lse_ref[1] = jnp.broadcast_to(m_b + jnp.log(l_b), (t, 128))

    return pl.pallas_call(
        body, name=name, grid=(4, nq),
        in_specs=[pl.BlockSpec((t, 128), lambda p, i: (i, p)),
                  pl.BlockSpec((s, 128), lambda p, i: (0, 4 + p)),
                  pl.BlockSpec((s, 128), lambda p, i: (0, 8 + p)),
                  pl.BlockSpec((None, 2, s), lambda p, i: (p, 0, 0))],
        out_specs=[pl.BlockSpec((t, 128), lambda p, i: (i, p)),
                   pl.BlockSpec((2, t, 128), lambda p, i: (p, i, 0))],
        out_shape=[_sds((s, FOX_WIDTH), BF16), _sds((FOX_HEADS, s, 128), F32)],
        compiler_params=_params("parallel", "arbitrary"))(qkv, qkv, qkv, cum4)


def fox_bwd(qkv, cum4, att, datt, lse, name):
    s = qkv.shape[0]
    t = min(s, FOX_TILE)
    nk = s // t
    nt_dims, tn_dims = _DOT_DIMS["nt"], _DOT_DIMS["tn"]

    def body(q_ref, k_ref, v_ref, c_ref, o_ref, do_ref, lse_ref, dq_ref, dk_ref, dv_ref, dc_ref, dr_ref,
             dq_acc, dk_acc, dv_acc, dr_acc):
        j = pl.program_id(1)
        k0 = pl.multiple_of(j * t, t)
        lane = lax.broadcasted_iota(jnp.int32, (t, 128), 1)
        lo = lane < 64
        kj, vj = k_ref[...], v_ref[...]
        zero = jnp.zeros_like(kj)
        k_h = (jnp.where(lo, kj, zero), jnp.where(lo, zero, kj))

        @pl.when(j == 0)
        def _():
            dq_acc[...] = jnp.zeros_like(dq_acc)
            dr_acc[...] = jnp.zeros_like(dr_acc)

        dk_acc[...] = jnp.zeros_like(dk_acc)
        dv_acc[...] = jnp.zeros_like(dv_acc)

        def step(i, dc, masked):
            q0 = pl.multiple_of(i * t, t)
            qi = q_ref[pl.ds(q0, t), :]
            doi = do_ref[pl.ds(q0, t), :]
            prod = doi.astype(F32) * o_ref[pl.ds(q0, t), :].astype(F32)
            zq = jnp.zeros_like(qi)
            dq_new = jnp.zeros((t, 128), F32)
            dc_out = []
            for hh in range(2):
                sel = lo if hh == 0 else jnp.logical_not(lo)
                q_m = jnp.where(sel, qi, zq)
                do_m = jnp.where(sel, doi, zq)
                delta = jnp.sum(jnp.where(sel, prod, 0.0), axis=-1, keepdims=True)
                bias = _fox_bias(c_ref, hh, q0, k0, t)
                sc = lax.dot_general(q_m, kj, nt_dims, preferred_element_type=F32) * FOX_SCALE + bias
                if masked:
                    sc = jnp.where(_causal_mask(t, t, q0, k0), sc, NEG_INF)
                lse_t = jnp.tile(lse_ref[hh, pl.ds(q0, t), :], (1, t // 128))
                p = jnp.exp(sc - lse_t)
                dv_acc[hh] += lax.dot_general(p.astype(BF16), doi, tn_dims, preferred_element_type=F32)
                dp = lax.dot_general(do_m, vj, nt_dims, preferred_element_type=F32)
                ds = p * (dp - delta)
                dc_out.append(dc[hh] - jnp.sum(ds, axis=0, keepdims=True))
                dr_acc[hh, pl.ds(q0, t), :] += jnp.sum(ds, axis=-1, keepdims=True)
                ds_b = ds.astype(BF16)
                dq_new = dq_new + jnp.dot(ds_b, k_h[hh], preferred_element_type=F32)
                dk_acc[hh] += lax.dot_general(ds_b, qi, tn_dims, preferred_element_type=F32)
            dq_acc[pl.ds(q0, t), :] += dq_new
            return tuple(dc_out)

        dc = step(j, (jnp.zeros((1, t), F32), jnp.zeros((1, t), F32)), True)
        dc = lax.fori_loop(j + 1, nk, lambda i, cr: step(i, cr, False), dc)
        dk_ref[...] = (jnp.where(lo, dk_acc[0], dk_acc[1]) * FOX_SCALE).astype(BF16)
        dv_ref[...] = jnp.where(lo, dv_acc[0], dv_acc[1]).astype(BF16)
        dc_ref[0:1, :] = dc[0]
        dc_ref[1:2, :] = dc[1]

        @pl.when(j == nk - 1)
        def _():
            dq_ref[...] = (dq_acc[...] * FOX_SCALE).astype(BF16)
            eye = lax.broadcasted_iota(jnp.int32, (t, t), 0) == lax.broadcasted_iota(jnp.int32, (t, t), 1)
            for hh in range(2):
                for b in range(nk):
                    col = dr_acc[hh, b * t:(b + 1) * t, :]
                    dr_ref[hh:hh + 1, b * t:(b + 1) * t] = jnp.sum(jnp.where(eye, col, 0.0), axis=0, keepdims=True)

    full = lambda col: pl.BlockSpec((s, 128), lambda p, j: (0, col(p)))
    blk = lambda col: pl.BlockSpec((t, 128), lambda p, j: (j, col(p)))
    return pl.pallas_call(
        body, name=name, grid=(4, nk),
        in_specs=[full(lambda p: p), blk(lambda p: 4 + p), blk(lambda p: 8 + p),
                  pl.BlockSpec((None, 2, s), lambda p, j: (p, 0, 0)),
                  full(lambda p: p), full(lambda p: p),
                  pl.BlockSpec((2, s, 128), lambda p, j: (p, 0, 0))],
        out_specs=[full(lambda p: p), blk(lambda p: p), blk(lambda p: p),
                   pl.BlockSpec((None, 2, t), lambda p, j: (p, 0, j)),
                   pl.BlockSpec((None, 2, s), lambda p, j: (p, 0, 0))],
        out_shape=[_sds((s, FOX_WIDTH), BF16), _sds((s, FOX_WIDTH), BF16), _sds((s, FOX_WIDTH), BF16),
                   _sds((4, 2, s), F32), _sds((4, 2, s), F32)],
        scratch_shapes=[pltpu.VMEM((s, 128), F32), pltpu.VMEM((2, t, 128), F32), pltpu.VMEM((2, t, 128), F32),
                        pltpu.VMEM((2, s, 1), F32)],
        compiler_params=_params("parallel", "arbitrary"))(qkv, qkv, qkv, cum4, att, datt, lse)


def _mem_probs(q_h, k_h):
    sc = lax.dot_general(q_h, k_h, _DOT_DIMS["nt"], preferred_element_type=F32) * MEM_SCALE
    e = jnp.exp(sc - jnp.max(sc, axis=-1, keepdims=True))
    return e / jnp.sum(e, axis=-1, keepdims=True)


def mem_attn_fwd(qm, km, vm, name):
    s = qm.shape[0]
    tm = min(s, ROW_TILE)

    def body(q_ref, k_ref, v_ref, o_ref):
        for h in range(MEM_HEADS):
            sl = slice(h * MEM_HEAD_DIM, (h + 1) * MEM_HEAD_DIM)
            p = _mem_probs(q_ref[:, sl], k_ref[:, sl])
            o_ref[:, sl] = jnp.dot(p.astype(BF16), v_ref[:, sl], preferred_element_type=F32).astype(BF16)

    kv = pl.BlockSpec((N_MEM, MEM_INNER), lambda i: (0, 0))
    row = pl.BlockSpec((tm, MEM_INNER), lambda i: (i, 0))
    return pl.pallas_call(body, name=name, grid=(s // tm,), in_specs=[row, kv, kv], out_specs=row,
                          out_shape=_sds((s, MEM_INNER), BF16), compiler_params=_params("parallel"))(qm, km, vm)


def mem_attn_bwd(qm, km, vm, dom, name):
    s = qm.shape[0]
    tm = min(s, ROW_TILE)
    tn_dims = _DOT_DIMS["tn"]

    def body(q_ref, k_ref, v_ref, do_ref, dq_ref, dk_ref, dv_ref):
        i = pl.program_id(0)

        @pl.when(i == 0)
        def _():
            dk_ref[...] = jnp.zeros_like(dk_ref)
            dv_ref[...] = jnp.zeros_like(dv_ref)

        for h in range(MEM_HEADS):
            sl = slice(h * MEM_HEAD_DIM, (h + 1) * MEM_HEAD_DIM)
            q_h, k_h, do_h = q_ref[:, sl], k_ref[:, sl], do_ref[:, sl]
            p = _mem_probs(q_h, k_h)
            dp = lax.dot_general(do_h, v_ref[:, sl], _DOT_DIMS["nt"], preferred_element_type=F32)
            ds = p * (dp - jnp.sum(p * dp, axis=-1, keepdims=True))
            ds_b = (ds * MEM_SCALE).astype(BF16)
            dq_ref[:, sl] = jnp.dot(ds_b, k_h, preferred_element_type=F32).astype(BF16)
            dk_ref[:, sl] += lax.dot_general(ds_b, q_h, tn_dims, preferred_element_type=F32)
            dv_ref[:, sl] += lax.dot_general(p.astype(BF16), do_h, tn_dims, preferred_element_type=F32)

    kv = pl.BlockSpec((N_MEM, MEM_INNER), lambda i: (0, 0))
    row = pl.BlockSpec((tm, MEM_INNER), lambda i: (i, 0))
    return pl.pallas_call(
        body, name=name, grid=(s // tm,), in_specs=[row, kv, kv, row], out_specs=[row, kv, kv],
        out_shape=[_sds((s, MEM_INNER), BF16), _sds((N_MEM, MEM_INNER), F32), _sds((N_MEM, MEM_INNER), F32)],
        compiler_params=_params("arbitrary"))(qm, km, vm, dom)


def adamw(w, g, m, v, name):
    r, c = w.shape
    tm = ROW_TILE if r % ROW_TILE == 0 else r
    c1 = 1.0 - ADAM_B1 ** ADAM_STEP
    c2 = 1.0 - ADAM_B2 ** ADAM_STEP

    def body(w_ref, g_ref, m_ref, v_ref, d_ref, nm_ref, nv_ref):
        gv = g_ref[...]
        nm = ADAM_B1 * m_ref[...] + (1.0 - ADAM_B1) * gv
        nv = ADAM_B2 * v_ref[...] + (1.0 - ADAM_B2) * (gv * gv)
        nm_ref[...] = nm
        nv_ref[...] = nv
        d_ref[...] = -ADAM_LR * ((nm / c1) / (jnp.sqrt(nv / c2) + ADAM_EPS) + ADAM_WD * w_ref[...])

    blk = pl.BlockSpec((tm, c), lambda i: (i, 0))
    return pl.pallas_call(body, name=name, grid=(r // tm,), in_specs=[blk] * 4, out_specs=[blk] * 3,
                          out_shape=[_sds((r, c), F32)] * 3, compiler_params=_params("parallel"))(w, g, m, v)


def _vec(v):
    return v.reshape(1, -1)


def layer_fwd(x0, mem, w, sm):
    s = x0.shape[0]
    h1 = rms_fwd(x0, _vec(sm["norm_mix_pre"]), "rms_mix_pre")
    ag, qkv, flp = matmul(
        h1, w["w_in"], mode="nn", dims=(s, IN_PAD, D_MODEL), tiles=(min(s, ROW_TILE), IN_PAD, D_MODEL), name="mix_in",
        out_shapes=[_sds((s, 2 * CONV_CH), F32), _sds((s, 3 * FOX_WIDTH), BF16), _sds((s, 128), F32)],
        out_specs=[pl.BlockSpec((min(s, ROW_TILE), 2 * CONV_CH), lambda i, j, k: (i, 0)),
                   pl.BlockSpec((min(s, ROW_TILE), 3 * FOX_WIDTH), lambda i, j, k: (i, 0)),
                   pl.BlockSpec((min(s, ROW_TILE), 128), lambda i, j, k: (i, 0))],
        epi=lambda acc: (acc[:, :2 * CONV_CH], acc[:, 2 * CONV_CH:2 * CONV_CH + 3 * FOX_WIDTH], acc[:, 2 * CONV_CH + 3 * FOX_WIDTH:]))
    u1, u = conv_fwd(ag, w["conv_w"], _vec(sm["conv_b"]), _vec(sm["conv_ln_g"]), _vec(sm["conv_ln_b"]), "conv_fwd")
    fl_t = flp[:, :FOX_HEADS].T
    cum = fox_gate_fwd(fl_t, sm["b_forget"].reshape(FOX_HEADS, 1), "fox_gate_fwd")
    cum4 = cum.reshape(4, 2, s)
    att, lse = fox_fwd(qkv, cum4, "fox_fwd")
    cat = jnp.concatenate([u, att], axis=1)
    tm = min(s, ROW_TILE)
    y1, = matmul(cat, w["w_out"], mode="nn", dims=(s, D_MODEL, D_MODEL), tiles=(tm, D_MODEL, D_MODEL), name="mix_out",
                 out_shapes=[_sds((s, D_MODEL), F32)])
    x1 = resid_post(x0, y1, _vec(sm["norm_mix_post"]), "post_mix")

    h2 = rms_fwd(x1, _vec(sm["norm_mem_pre"]), "rms_mem_pre")
    mn = rms_fwd(mem, _vec(sm["norm_memkv"]), "rms_memkv")
    qm, = matmul(h2, w["w_mq"], mode="nn", dims=(s, MEM_INNER, D_MODEL), tiles=(tm, MEM_INNER, D_MODEL), name="mem_q",
                 out_shapes=[_sds((s, MEM_INNER), BF16)])
    km, = matmul(mn, w["w_mk"], mode="nn", dims=(N_MEM, MEM_INNER, D_MODEL), tiles=(N_MEM, MEM_INNER, D_MODEL), name="mem_k",
                 out_shapes=[_sds((N_MEM, MEM_INNER), BF16)])
    vm, = matmul(mn, w["w_mv"], mode="nn", dims=(N_MEM, MEM_INNER, D_MODEL), tiles=(N_MEM, MEM_INNER, D_MODEL), name="mem_v",
                 out_shapes=[_sds((N_MEM, MEM_INNER), BF16)])
    om = mem_attn_fwd(qm, km, vm, "mem_attn_fwd")
    y2, = matmul(om, w["w_mo"], mode="nn", dims=(s, D_MODEL, MEM_INNER), tiles=(tm, D_MODEL, MEM_INNER), name="mem_o",
                 out_shapes=[_sds((s, D_MODEL), F32)])
    x2 = resid_post(x1, y2, _vec(sm["norm_mem_post"]), "post_mem")

    h3 = rms_fwd(x2, _vec(sm["norm_mlp_pre"]), "rms_mlp_pre")

    def relu2(acc):
        r = jnp.maximum(acc, 0.0)
        return (r * r,)

    act, = matmul(h3, w["w_up"], mode="nn", dims=(s, D_FF, D_MODEL), tiles=(tm, 1024, D_MODEL), name="mlp_up",
                  out_shapes=[_sds((s, D_FF), BF16)], epi=relu2)
    y3, = matmul(act, w["w_down"], mode="nn", dims=(s, D_MODEL, D_FF), tiles=(tm, D_MODEL, 1024), name="mlp_down",
                 out_shapes=[_sds((s, D_MODEL), F32)])
    x3 = resid_post(x2, y3, _vec(sm["norm_mlp_post"]), "post_mlp")
    saved = dict(x0=x0, h1=h1, ag=ag, qkv=qkv, fl_t=fl_t, u1=u1, cum4=cum4, att=att, lse=lse, cat=cat, y1=y1, x1=x1,
                 h2=h2, mn=mn, qm=qm, km=km, vm=vm, om=om, y2=y2, x2=x2, h3=h3, act=act, y3=y3)
    return x3, saved


def layer_bwd(dx3, mem, w, sm, sv):
    s = dx3.shape[0]
    tm = min(s, ROW_TILE)
    gw, gs = {}, {}
    dy3, gs["norm_mlp_post"] = rms_bwd(sv["y3"], _vec(sm["norm_mlp_post"]), dx3, "bwd_post_mlp", out_dtype=BF16)
    gw["w_down"], = matmul(sv["act"], dy3, mode="tn", dims=(D_FF, D_MODEL, s), tiles=(1024, D_MODEL, tm), name="dw_down",
                           out_shapes=[_sds((D_FF, D_MODEL), F32)])

    def dup_epi(acc, act_ref):
        return (acc * (2.0 * jnp.sqrt(act_ref[...].astype(F32))),)

    dup, = matmul(dy3, w["w_down"], mode="nt", dims=(s, D_FF, D_MODEL), tiles=(tm, 1024, D_MODEL), name="d_act",
                  out_shapes=[_sds((s, D_FF), BF16)], epi=dup_epi, extras=(sv["act"],),
                  extra_specs=(pl.BlockSpec((tm, 1024), lambda i, j, k: (i, j)),))
    gw["w_up"], = matmul(sv["h3"], dup, mode="tn", dims=(D_MODEL, D_FF, s), tiles=(512, 2048, tm), name="dw_up",
                         out_shapes=[_sds((D_MODEL, D_FF), F32)])
    dh3, = matmul(dup, w["w_up"], mode="nt", dims=(s, D_MODEL, D_FF), tiles=(tm, D_MODEL, 1024), name="d_h3",
                  out_shapes=[_sds((s, D_MODEL), F32)])
    dx2, gs["norm_mlp_pre"] = rms_bwd(sv["x2"], _vec(sm["norm_mlp_pre"]), dh3, "bwd_pre_mlp", add=dx3)

    dy2, gs["norm_mem_post"] = rms_bwd(sv["y2"], _vec(sm["norm_mem_post"]), dx2, "bwd_post_mem", out_dtype=BF16)
    gw["w_mo"], = matmul(sv["om"], dy2, mode="tn", dims=(MEM_INNER, D_MODEL, s), tiles=(MEM_INNER, D_MODEL, tm), name="dw_mo",
                         out_shapes=[_sds((MEM_INNER, D_MODEL), F32)])
    dom, = matmul(dy2, w["w_mo"], mode="nt", dims=(s, MEM_INNER, D_MODEL), tiles=(tm, MEM_INNER, D_MODEL), name="d_om",
                  out_shapes=[_sds((s, MEM_INNER), BF16)])
    dqm, dkm, dvm = mem_attn_bwd(sv["qm"], sv["km"], sv["vm"], dom, "mem_attn_bwd")
    gw["w_mq"], = matmul(sv["h2"], dqm, mode="tn", dims=(D_MODEL, MEM_INNER, s), tiles=(D_MODEL, MEM_INNER, tm), name="dw_mq",
                         out_shapes=[_sds((D_MODEL, MEM_INNER), F32)])
    dkm_b, dvm_b = dkm.astype(BF16), dvm.astype(BF16)
    gw["w_mk"], = matmul(sv["mn"], dkm_b, mode="tn", dims=(D_MODEL, MEM_INNER, N_MEM), tiles=(D_MODEL, MEM_INNER, N_MEM),
                         name="dw_mk", out_shapes=[_sds((D_MODEL, MEM_INNER), F32)])
    gw["w_mv"], = matmul(sv["mn"], dvm_b, mode="tn", dims=(D_MODEL, MEM_INNER, N_MEM), tiles=(D_MODEL, MEM_INNER, N_MEM),
                         name="dw_mv", out_shapes=[_sds((D_MODEL, MEM_INNER), F32)])
    dmn_k, = matmul(dkm_b, w["w_mk"], mode="nt", dims=(N_MEM, D_MODEL, MEM_INNER), tiles=(N_MEM, D_MODEL, MEM_INNER),
                    name="d_mn_k", out_shapes=[_sds((N_MEM, D_MODEL), F32)])
    dmn, = matmul(dvm_b, w["w_mv"], mode="nt", dims=(N_MEM, D_MODEL, MEM_INNER), tiles=(N_MEM, D_MODEL, MEM_INNER),
                  name="d_mn_v", out_shapes=[_sds((N_MEM, D_MODEL), F32)],
                  epi=lambda acc, other: (acc + other[...],), extras=(dmn_k,),
                  extra_specs=(pl.BlockSpec((N_MEM, D_MODEL), lambda i, j, k: (0, 0)),))
    _, gs["norm_memkv"] = rms_bwd(mem, _vec(sm["norm_memkv"]), dmn, "bwd_memkv")
    dh2, = matmul(dqm, w["w_mq"], mode="nt", dims=(s, D_MODEL, MEM_INNER), tiles=(tm, D_MODEL, MEM_INNER), name="d_h2",
                  out_shapes=[_sds((s, D_MODEL), F32)])
    dx1, gs["norm_mem_pre"] = rms_bwd(sv["x1"], _vec(sm["norm_mem_pre"]), dh2, "bwd_pre_mem", add=dx2)

    dy1, gs["norm_mix_post"] = rms_bwd(sv["y1"], _vec(sm["norm_mix_post"]), dx1, "bwd_post_mix", out_dtype=BF16)
    gw["w_out"], = matmul(sv["cat"], dy1, mode="tn", dims=(D_MODEL, D_MODEL, s), tiles=(D_MODEL, D_MODEL, tm), name="dw_out",
                          out_shapes=[_sds((D_MODEL, D_MODEL), F32)])
    du, datt = matmul(dy1, w["w_out"], mode="nt", dims=(s, D_MODEL, D_MODEL), tiles=(tm, D_MODEL, D_MODEL), name="d_cat",
                      out_shapes=[_sds((s, CONV_CH), F32), _sds((s, FOX_WIDTH), BF16)],
                      out_specs=[pl.BlockSpec((tm, CONV_CH), lambda i, j, k: (i, 0)), pl.BlockSpec((tm, FOX_WIDTH), lambda i, j, k: (i, 0))],
                      epi=lambda acc: (acc[:, :CONV_CH], acc[:, CONV_CH:]))
    dag, csm = conv_bwd(du, sv["u1"], sv["ag"], w["conv_w"], _vec(sm["conv_ln_g"]), _vec(sm["conv_ln_b"]), "conv_bwd")
    gw["conv_w"] = csm[:CONV_WIDTH]
    gs["conv_b"], gs["conv_ln_g"], gs["conv_ln_b"] = csm[32:33], csm[33:34], csm[34:35]
    dq, dk, dv, dcol4, drow4 = fox_bwd(sv["qkv"], sv["cum4"], sv["att"], datt, sv["lse"], "fox_bwd")
    dfl_t, dbf = fox_gate_bwd(dcol4.reshape(FOX_HEADS, s), drow4.reshape(FOX_HEADS, s), sv["fl_t"],
                              sm["b_forget"].reshape(FOX_HEADS, 1), "fox_gate_bwd")
    gs["b_forget"] = dbf.reshape(1, FOX_HEADS)
    dflp = jnp.pad(dfl_t.T.astype(BF16), ((0, 0), (0, 128 - FOX_HEADS)))
    dz = jnp.concatenate([dag, dq, dk, dv, dflp], axis=1)
    gw["w_in"], = matmul(sv["h1"], dz, mode="tn", dims=(D_MODEL, IN_PAD, s), tiles=(512, IN_PAD, tm), name="dw_in",
                         out_shapes=[_sds((D_MODEL, IN_PAD), F32)])
    dh1, = matmul(dz, w["w_in"], mode="nt", dims=(s, D_MODEL, IN_PAD), tiles=(tm, D_MODEL, IN_PAD), name="d_h1",
                  out_shapes=[_sds((s, D_MODEL), F32)])
    dx0, gs["norm_mix_pre"] = rms_bwd(sv["x0"], _vec(sm["norm_mix_pre"]), dh1, "bwd_pre_mix", add=dx1)
    return dx0, gw, gs


def local_step(x, mem, tgt, wl, sml):
    depth = len(wl)
    saved = []
    h = x
    for l in range(depth):
        h, sv = layer_fwd(h, mem, wl[l], sml[l])
        saved.append(sv)
    sq, dx = loss_head(h, tgt, "loss_head")
    gws, gss = [None] * depth, [None] * depth
    for l in reversed(range(depth)):
        dx, gws[l], gss[l] = layer_bwd(dx, mem, wl[l], sml[l], saved[l])
    return sq, dx, gws, gss


ANY = pl.BlockSpec(memory_space=pl.ANY)


def _place():
    x, y, c = lax.axis_index("x"), lax.axis_index("y"), lax.axis_index("c")
    chips = [(1 - x, y), (x, 1 - y), (1 - x, 1 - y)]
    return x, y, c, chips


def gather_weights(shards):
    n = len(shards)

    def body(*refs):
        ins, outs = refs[:n], refs[n:2 * n]
        send_sems, recv_sems, local_sems = refs[2 * n:]
        x, y, c, chips = _place()
        own = 2 * x + y
        sib = (x, y, 1 - c)
        local = [pltpu.make_async_copy(ins[t], outs[t].at[own], local_sems.at[t]) for t in range(n)]
        for cp in local:
            cp.start()

        def copy(t, k, chip_id, layer, to, src=None):
            dst = outs[t].at[chip_id, layer]
            return pltpu.make_async_remote_copy(src_ref=dst if src is None else src, dst_ref=dst,
                                                send_sem=send_sems.at[t, k], recv_sem=recv_sems.at[t, k],
                                                device_id=to, device_id_type=MESH)

        first = [copy(t, k, own, c, (*chip, c), src=ins[t].at[c]) for t in range(n) for k, chip in enumerate(chips)]
        for cp in first:
            cp.start()
        passed = []
        for t in range(n):
            for k, (px, py) in enumerate(chips):
                copy(t, k, 2 * px + py, c, sib).wait_recv()
                fw = copy(t, 3 + k, 2 * px + py, c, sib)
                fw.start()
                passed.append(fw)
        for t in range(n):
            for k, (px, py) in enumerate(chips):
                copy(t, 3 + k, 2 * px + py, 1 - c, sib).wait_recv()
        for cp in first + passed:
            cp.wait_send()
        for cp in local:
            cp.wait()

    return pl.pallas_call(
        body, name="gather_weights",
        out_shape=[_sds((N_CHIPS,) + a.shape, a.dtype) for a in shards],
        in_specs=[ANY] * n, out_specs=[ANY] * n,
        scratch_shapes=[pltpu.SemaphoreType.DMA((n, 6)), pltpu.SemaphoreType.DMA((n, 6)), pltpu.SemaphoreType.DMA((n,))],
    )(*shards)


def swap_layers(grads):
    n = len(grads)

    def body(*refs):
        ins, outs = refs[:n], refs[n:2 * n]
        send_sems, recv_sems = refs[2 * n:]
        x, y, c, _ = _place()
        cps = [pltpu.make_async_remote_copy(src_ref=ins[t].at[1 - c], dst_ref=outs[t], send_sem=send_sems.at[t],
                                            recv_sem=recv_sems.at[t], device_id=(x, y, 1 - c), device_id_type=MESH)
               for t in range(n)]
        for cp in cps:
            cp.start()
        for cp in cps:
            cp.wait()

    return pl.pallas_call(
        body, name="swap_layers", out_shape=[_sds(g.shape[1:], g.dtype) for g in grads],
        in_specs=[ANY] * n, out_specs=[ANY] * n,
        scratch_shapes=[pltpu.SemaphoreType.DMA((n,)), pltpu.SemaphoreType.DMA((n,))])(*grads)


def scatter_partials(parts):
    n = len(parts)

    def body(*refs):
        ins, outs = refs[:n], refs[n:2 * n]
        send_sems, recv_sems = refs[2 * n:]
        x, y, c, chips = _place()
        own = 2 * x + y
        cps = [pltpu.make_async_remote_copy(src_ref=ins[t].at[2 * px + py], dst_ref=outs[t].at[own],
                                            send_sem=send_sems.at[t, k], recv_sem=recv_sems.at[t, k],
                                            device_id=(px, py, c), device_id_type=MESH)
               for t in range(n) for k, (px, py) in enumerate(chips)]
        for cp in cps:
            cp.start()
        for t in range(n):
            for k, (px, py) in enumerate(chips):
                pltpu.make_async_remote_copy(src_ref=ins[t].at[own], dst_ref=outs[t].at[2 * px + py],
                                             send_sem=send_sems.at[t, k], recv_sem=recv_sems.at[t, k],
                                             device_id=(px, py, c), device_id_type=MESH).wait_recv()
        for cp in cps:
            cp.wait_send()

    return pl.pallas_call(
        body, name="scatter_partials", out_shape=[_sds(p.shape, p.dtype) for p in parts],
        in_specs=[ANY] * n, out_specs=[ANY] * n,
        scratch_shapes=[pltpu.SemaphoreType.DMA((n, 3)), pltpu.SemaphoreType.DMA((n, 3))])(*parts)


def share_reduced(reds):
    n = len(reds)

    def body(*refs):
        ins, outs = refs[:n], refs[n:2 * n]
        send_sems, recv_sems, local_sems = refs[2 * n:]
        x, y, c, _ = _place()
        local = [pltpu.make_async_copy(ins[t], outs[t].at[c], local_sems.at[t]) for t in range(n)]
        cps = [pltpu.make_async_remote_copy(src_ref=ins[t], dst_ref=outs[t].at[c], send_sem=send_sems.at[t],
                                            recv_sem=recv_sems.at[t], device_id=(x, y, 1 - c), device_id_type=MESH)
               for t in range(n)]
        for cp in local + cps:
            cp.start()
        for t in range(n):
            pltpu.make_async_remote_copy(src_ref=ins[t], dst_ref=outs[t].at[1 - c], send_sem=send_sems.at[t],
                                         recv_sem=recv_sems.at[t], device_id=(x, y, 1 - c), device_id_type=MESH).wait_recv()
        for cp in cps:
            cp.wait_send()
        for cp in local:
            cp.wait()

    return pl.pallas_call(
        body, name="share_reduced", out_shape=[_sds((2,) + r.shape, r.dtype) for r in reds],
        in_specs=[ANY] * n, out_specs=[ANY] * n,
        scratch_shapes=[pltpu.SemaphoreType.DMA((n,)), pltpu.SemaphoreType.DMA((n,)), pltpu.SemaphoreType.DMA((n,))])(*reds)


def pair_sum(gs, other, place, name, wire_dtype):
    _, _, r, c = gs.shape
    tm = ROW_TILE if r % ROW_TILE == 0 else r

    def body(pl_ref, g_ref, o_ref, own_ref, wire_ref):
        k = pl.program_id(1)
        val = g_ref[...] + o_ref[...]
        wire_ref[...] = val.astype(wire_dtype)

        @pl.when(k == pl_ref[1])
        def _():
            own_ref[...] = val

    return pl.pallas_call(
        body, name=name,
        grid_spec=pltpu.PrefetchScalarGridSpec(
            num_scalar_prefetch=1, grid=(r // tm, N_CHIPS),
            in_specs=[pl.BlockSpec((None, None, tm, c), lambda i, k, p: (p[0], k, i, 0)),
                      pl.BlockSpec((None, tm, c), lambda i, k, p: (k, i, 0))],
            out_specs=[pl.BlockSpec((tm, c), lambda i, k, p: (i, 0)),
                       pl.BlockSpec((None, tm, c), lambda i, k, p: (k, i, 0))]),
        out_shape=[_sds((r, c), F32), _sds((N_CHIPS, r, c), wire_dtype)],
        compiler_params=_params("parallel", "arbitrary"))(place, gs, other)


def chip_sum(own, recv, place, name):
    r, c = own.shape
    tm = ROW_TILE if r % ROW_TILE == 0 else r

    def body(pl_ref, own_ref, a_ref, b_ref, c_ref, out_ref):
        out_ref[...] = ((own_ref[...] + a_ref[...].astype(F32)) + b_ref[...].astype(F32)) + c_ref[...].astype(F32)

    slot = lambda d: pl.BlockSpec((None, tm, c), lambda i, p: ((p[1] + d) % N_CHIPS, i, 0))
    return pl.pallas_call(
        body, name=name,
        grid_spec=pltpu.PrefetchScalarGridSpec(
            num_scalar_prefetch=1, grid=(r // tm,),
            in_specs=[pl.BlockSpec((tm, c), lambda i, p: (i, 0)), slot(1), slot(2), slot(3)],
            out_specs=pl.BlockSpec((tm, c), lambda i, p: (i, 0))),
        out_shape=_sds((r, c), F32),
        compiler_params=_params("parallel"))(place, own, recv, recv, recv)


def allreduce_small(packed):
    rows, cols = packed.shape

    def body(in_ref, out_ref, buf_ref, send_sems, recv_sems):
        x, y, c, _ = _place()
        me = 4 * x + 2 * y + c
        buf_ref[me] = in_ref[...]
        peers = [(x ^ (k >> 2), y ^ ((k >> 1) & 1), c ^ (k & 1)) for k in range(1, 8)]
        cps = [pltpu.make_async_remote_copy(src_ref=in_ref, dst_ref=buf_ref.at[me], send_sem=send_sems.at[k],
                                            recv_sem=recv_sems.at[k], device_id=peer, device_id_type=MESH)
               for k, peer in enumerate(peers)]
        for cp in cps:
            cp.start()
        for k, (px, py, pc) in enumerate(peers):
            pltpu.make_async_remote_copy(src_ref=in_ref, dst_ref=buf_ref.at[4 * px + 2 * py + pc], send_sem=send_sems.at[k],
                                         recv_sem=recv_sems.at[k], device_id=(px, py, pc), device_id_type=MESH).wait_recv()
        for cp in cps:
            cp.wait_send()
        total = buf_ref[0]
        for d in range(1, 8):
            total = total + buf_ref[d]
        out_ref[...] = total

    vm = pl.BlockSpec(memory_space=pltpu.VMEM)
    return pl.pallas_call(
        body, name="allreduce_small", out_shape=_sds((rows, cols), F32), in_specs=[vm], out_specs=vm,
        scratch_shapes=[pltpu.VMEM((8, rows, cols), F32), pltpu.SemaphoreType.DMA((7,)), pltpu.SemaphoreType.DMA((7,))])(packed)


def pack_small(get):
    rows = []
    for l in range(2):
        for nm in NORMS:
            rows.append(get(nm, l).reshape(-1))
        rows.append(jnp.concatenate([get("conv_b", l).reshape(-1), get("conv_ln_g", l).reshape(-1)]))
        rows.append(jnp.concatenate([get("conv_ln_b", l).reshape(-1), get("b_forget", l).reshape(-1),
                                     jnp.zeros((D_MODEL - CONV_CH - FOX_HEADS,), F32)]))
    rows += [jnp.zeros((D_MODEL,), F32)] * (SMALL_ROWS - len(rows))
    return jnp.stack(rows)


def unpack_small(packed):
    out = {}
    for idx, nm in enumerate(NORMS):
        out[nm] = jnp.stack([packed[9 * l + idx] for l in range(2)])
    out["conv_b"] = jnp.stack([packed[9 * l + 7, :CONV_CH] for l in range(2)])
    out["conv_ln_g"] = jnp.stack([packed[9 * l + 7, CONV_CH:] for l in range(2)])
    out["conv_ln_b"] = jnp.stack([packed[9 * l + 8, :CONV_CH] for l in range(2)])
    out["b_forget"] = jnp.stack([packed[9 * l + 8, CONV_CH:CONV_CH + FOX_HEADS] for l in range(2)])
    return out


def assemble_weights(gathered, l):
    cols = lambda a: jnp.moveaxis(a[:, l], 0, 1).reshape(a.shape[2], N_CHIPS * a.shape[3])
    rows = lambda a: a[:, l].reshape(N_CHIPS * a.shape[2], a.shape[3])
    w_in = jnp.pad(cols(gathered["w_in"]), ((0, 0), (0, IN_PAD - IN_COLS)))
    conv_w = jnp.pad(cols(gathered["conv_w"]), ((0, CONV_HALO - CONV_WIDTH), (0, 0)))
    return dict(w_in=w_in, conv_w=conv_w, w_out=rows(gathered["w_out"]), w_mq=rows(gathered["w_mq"]),
                w_mk=rows(gathered["w_mk"]), w_mv=rows(gathered["w_mv"]), w_mo=cols(gathered["w_mo"]),
                w_up=cols(gathered["w_up"]), w_down=rows(gathered["w_down"]))


def shard_layout(name, g):
    if name == "w_in":
        g = g[:, :IN_COLS]
    if name in ("w_in", "conv_w", "w_mo", "w_up"):
        return jnp.moveaxis(g.reshape(g.shape[0], N_CHIPS, g.shape[1] // N_CHIPS), 1, 0)
    return g.reshape(N_CHIPS, g.shape[0] // N_CHIPS, g.shape[1])


def kernel(x, mem, norm_mix_pre, norm_mix_post, w_in, b_forget, conv_w, conv_b, conv_ln_g, conv_ln_b, w_out, norm_mem_pre, norm_mem_post, norm_memkv, w_mq, w_mk, w_mv, w_mo, norm_mlp_pre, norm_mlp_post, w_up, w_down, loss_target, m_norm_mix_pre, m_norm_mix_post, m_w_in, m_b_forget, m_conv_w, m_conv_b, m_conv_ln_g, m_conv_ln_b, m_w_out, m_norm_mem_pre, m_norm_mem_post, m_norm_memkv, m_w_mq, m_w_mk, m_w_mv, m_w_mo, m_norm_mlp_pre, m_norm_mlp_post, m_w_up, m_w_down, v_norm_mix_pre, v_norm_mix_post, v_w_in, v_b_forget, v_conv_w, v_conv_b, v_conv_ln_g, v_conv_ln_b, v_w_out, v_norm_mem_pre, v_norm_mem_post, v_norm_memkv, v_w_mq, v_w_mk, v_w_mv, v_w_mo, v_norm_mlp_pre, v_norm_mlp_post, v_w_up, v_w_down):
    args = dict(locals())
    wts = {n: args[n] for n in WEIGHTS}
    mom = {n: args["m_" + n] for n in WEIGHTS}
    var = {n: args["v_" + n] for n in WEIGHTS}
    place = jnp.stack([lax.axis_index("c"), 2 * lax.axis_index("x") + lax.axis_index("y")]).astype(jnp.int32)

    wire = [wts[n] if n == "conv_w" else wts[n].astype(BF16) for n in BIG]
    gathered = dict(zip(BIG, gather_weights(wire)))
    wl = [assemble_weights(gathered, l) for l in range(2)]
    sml = [{n: wts[n][l] for n in SMALL} for l in range(2)]

    sq, grad_x, gws, gss = local_step(x[0], mem[0], loss_target[0], wl, sml)
    loss = lax.psum(0.5 * jnp.sum(sq) / D_MODEL, ("x", "y", "c"))

    stacked = [jnp.stack([shard_layout(n, gws[l][n]) for l in range(2)]) for n in BIG]
    others = swap_layers(stacked)
    owns, wires = [], []
    for n, gs_, ot in zip(BIG, stacked, others):
        o, wv = pair_sum(gs_, ot, place, "pair_sum_" + n, F32 if n == "conv_w" else BF16)
        owns.append(o)
        wires.append(wv)
    recvd = scatter_partials(wires)
    reds = [chip_sum(o, rv, place, "chip_sum_" + n) for n, o, rv in zip(BIG, owns, recvd)]
    grads = dict(zip(BIG, share_reduced(reds)))

    g_small = unpack_small(allreduce_small(pack_small(lambda nm, l: gss[l][nm])))
    grads.update(g_small)

    delta, new_m, new_v = {}, {}, {}
    for n in BIG:
        shp = wts[n].shape
        two_d = lambda a: a.reshape(shp[0] * shp[1], shp[2])
        d_, m_, v_ = adamw(two_d(wts[n]), two_d(grads[n]), two_d(mom[n]), two_d(var[n]), "adamw_" + n)
        delta[n], new_m[n], new_v[n] = d_.reshape(shp), m_.reshape(shp), v_.reshape(shp)
    pk = lambda src: pack_small(lambda nm, l: src[nm][l])
    d_, m_, v_ = adamw(pk(wts), pk(grads), pk(mom), pk(var), "adamw_small")
    for res, packed in ((delta, d_), (new_m, m_), (new_v, v_)):
        res.update(unpack_small(packed))

    return (loss, grad_x[None], *[grads[n] for n in WEIGHTS], *[delta[n] for n in WEIGHTS],
            *[new_m[n] for n in WEIGHTS], *[new_v[n] for n in WEIGHTS])
```

```python
import functools

import jax
import jax.numpy as jnp
from jax import lax
from jax.experimental import pallas as pl
from jax.experimental.pallas import tpu as pltpu

F32, BF16 = jnp.float32, jnp.bfloat16
D_MODEL = 1024
CONV_CH = 512
CONV_WIDTH = 31
CONV_HALO = 32
FOX_WIDTH = 512
FOX_HEADS = 8
N_MEM = 256
MEM_HEADS = 4
MEM_HEAD_DIM = 128
MEM_INNER = 512
D_FF = 4096
IN_COLS = 2568
IN_PAD = 2688
EPS = 1e-6
NEG_INF = -1e30
FOX_SCALE = 0.125
MEM_SCALE = MEM_HEAD_DIM ** -0.5
ADAM_LR, ADAM_B1, ADAM_B2, ADAM_EPS, ADAM_WD, ADAM_STEP = 0.001, 0.9, 0.999, 1e-08, 0.01, 10
VMEM_LIMIT = 56 * 1024 * 1024
ROW_TILE = 512
MESH = pl.DeviceIdType.MESH
N_CHIPS = 4

BIG = ("w_in", "conv_w", "w_out", "w_mq", "w_mk", "w_mv", "w_mo", "w_up", "w_down")
NORMS = ("norm_mix_pre", "norm_mix_post", "norm_mem_pre", "norm_mem_post", "norm_memkv", "norm_mlp_pre", "norm_mlp_post")
SMALL = NORMS + ("conv_b", "conv_ln_g", "conv_ln_b", "b_forget")
WEIGHTS = ("norm_mix_pre", "norm_mix_post", "w_in", "b_forget", "conv_w", "conv_b", "conv_ln_g", "conv_ln_b", "w_out",
           "norm_mem_pre", "norm_mem_post", "norm_memkv", "w_mq", "w_mk", "w_mv", "w_mo", "norm_mlp_pre", "norm_mlp_post",
           "w_up", "w_down")
SMALL_ROWS = 24


def _params(*sem):
    return pltpu.CompilerParams(dimension_semantics=sem, vmem_limit_bytes=VMEM_LIMIT)


def rms_fwd(x, g, name):
    m, d = x.shape
    tm = min(m, ROW_TILE)

    def body(x_ref, g_ref, h_ref):
        xv = x_ref[...]
        r = lax.rsqrt(jnp.mean(xv * xv, axis=-1, keepdims=True) + EPS)
        h_ref[...] = ((xv * r) * g_ref[...]).astype(BF16)

    return pl.pallas_call(
        body, name=name, grid=(m // tm,),
        in_specs=[pl.BlockSpec((tm, d), lambda i: (i, 0)), pl.BlockSpec((1, d), lambda i: (0, 0))],
        out_specs=pl.BlockSpec((tm, d), lambda i: (i, 0)),
        out_shape=jax.ShapeDtypeStruct((m, d), BF16),
        compiler_params=_params("parallel"))(x, g)


def resid_post(x, y, g, name):
    m, d = x.shape
    tm = min(m, ROW_TILE)

    def body(x_ref, y_ref, g_ref, o_ref):
        yv = y_ref[...]
        r = lax.rsqrt(jnp.mean(yv * yv, axis=-1, keepdims=True) + EPS)
        o_ref[...] = x_ref[...] + (yv * r) * g_ref[...]

    row = pl.BlockSpec((tm, d), lambda i: (i, 0))
    return pl.pallas_call(
        body, name=name, grid=(m // tm,),
        in_specs=[row, row, pl.BlockSpec((1, d), lambda i: (0, 0))],
        out_specs=row, out_shape=jax.ShapeDtypeStruct((m, d), F32),
        compiler_params=_params("parallel"))(x, y, g)


def rms_bwd(y, g, dout, name, add=None, out_dtype=F32):
    m, d = y.shape
    tm = min(m, ROW_TILE)
    has_add = add is not None

    def body(*refs):
        y_ref, g_ref, d_ref = refs[:3]
        dy_ref, dg_ref = refs[-2:]
        i = pl.program_id(0)
        yv = y_ref[...]
        dv = d_ref[...].astype(F32)
        r = lax.rsqrt(jnp.mean(yv * yv, axis=-1, keepdims=True) + EPS)
        gy = g_ref[...] * dv
        t = jnp.mean(yv * gy, axis=-1, keepdims=True) * (r * r)
        dy = r * (gy - yv * t)
        if has_add:
            dy = dy + refs[3][...]
        dy_ref[...] = dy.astype(dy_ref.dtype)
        part = jnp.sum(dv * (yv * r), axis=0, keepdims=True)

        @pl.when(i == 0)
        def _():
            dg_ref[...] = part

        @pl.when(i > 0)
        def _():
            dg_ref[...] += part

    row = pl.BlockSpec((tm, d), lambda i: (i, 0))
    vec = pl.BlockSpec((1, d), lambda i: (0, 0))
    args = [y, g, dout] + ([add] if has_add else [])
    return pl.pallas_call(
        body, name=name, grid=(m // tm,),
        in_specs=[row, vec, row] + ([row] if has_add else []),
        out_specs=[row, vec],
        out_shape=[jax.ShapeDtypeStruct((m, d), out_dtype), jax.ShapeDtypeStruct((1, d), F32)],
        compiler_params=_params("arbitrary"))(*args)


def loss_head(xf, tgt, name):
    m, d = xf.shape
    tm = min(m, ROW_TILE)

    def body(x_ref, t_ref, s_ref, dx_ref):
        i = pl.program_id(0)
        err = x_ref[...] - t_ref[...]
        dx_ref[...] = err * (1.0 / d)
        part = jnp.sum(err * err, axis=0, keepdims=True)

        @pl.when(i == 0)
        def _():
            s_ref[...] = part

        @pl.when(i > 0)
        def _():
            s_ref[...] += part

    row = pl.BlockSpec((tm, d), lambda i: (i, 0))
    vec = pl.BlockSpec((1, d), lambda i: (0, 0))
    return pl.pallas_call(
        body, name=name, grid=(m // tm,), in_specs=[row, row], out_specs=[vec, row],
        out_shape=[jax.ShapeDtypeStruct((1, d), F32), jax.ShapeDtypeStruct((m, d), F32)],
        compiler_params=_params("arbitrary"))(xf, tgt)


_DOT_DIMS = {"nn": (((1,), (0,)), ((), ())), "nt": (((1,), (1,)), ((), ())), "tn": (((0,), (0,)), ((), ()))}


def matmul(a, b, *, mode, dims, tiles, name, out_shapes, out_specs=None, epi=None, extras=(), extra_specs=(),
           a_spec=None, b_spec=None):
    m, n, k = dims
    tm, tn, tk = tiles
    nk = k // tk
    assert m % tm == 0 and n % tn == 0 and k % tk == 0
    n_ex, n_out = len(extras), len(out_shapes)
    if a_spec is None:
        a_spec = pl.BlockSpec((tk, tm), lambda i, j, kk: (kk, i)) if mode == "tn" else pl.BlockSpec((tm, tk), lambda i, j, kk: (i, kk))
    if b_spec is None:
        b_spec = pl.BlockSpec((tn, tk), lambda i, j, kk: (j, kk)) if mode == "nt" else pl.BlockSpec((tk, tn), lambda i, j, kk: (kk, j))
    if out_specs is None:
        out_specs = [pl.BlockSpec((tm, tn), lambda i, j, kk: (i, j)) for _ in out_shapes]
    if epi is None:
        epi = lambda acc: (acc,)

    def body(*refs):
        a_ref, b_ref = refs[0], refs[1]
        ex = refs[2:2 + n_ex]
        outs = refs[2 + n_ex:2 + n_ex + n_out]
        part = lax.dot_general(a_ref[...], b_ref[...], _DOT_DIMS[mode], preferred_element_type=F32)

        def finish(acc):
            for o_ref, val in zip(outs, epi(acc, *ex)):
                o_ref[...] = val.astype(o_ref.dtype)

        if nk == 1:
            finish(part)
        else:
            acc_ref = refs[-1]
            kk = pl.program_id(2)

            @pl.when(kk == 0)
            def _():
                acc_ref[...] = part

            @pl.when(kk > 0)
            def _():
                acc_ref[...] += part

            @pl.when(kk == nk - 1)
            def _():
                finish(acc_ref[...])

    return pl.pallas_call(
        body, name=name, grid=(m // tm, n // tn, nk),
        in_specs=[a_spec, b_spec] + list(extra_specs),
        out_specs=list(out_specs), out_shape=list(out_shapes),
        scratch_shapes=[pltpu.VMEM((tm, tn), F32)] if nk > 1 else [],
        compiler_params=_params("parallel", "parallel", "arbitrary"))(a, b, *extras)


def _sds(shape, dtype):
    return jax.ShapeDtypeStruct(shape, dtype)


def _sigmoid(v):
    return 1.0 / (1.0 + jnp.exp(-v))


def conv_fwd(ag, cw, cb, lg, lb, name):
    s = ag.shape[0]
    tm = min(s, ROW_TILE)
    c = CONV_CH
    hb = tm // CONV_HALO

    def body(ag_ref, halo_ref, w_ref, cb_ref, lg_ref, lb_ref, u1_ref, u_ref, ext_ref):
        i = pl.program_id(0)
        u0 = ag_ref[:, :c] * _sigmoid(ag_ref[:, c:])
        h0 = halo_ref[:, :c] * _sigmoid(halo_ref[:, c:])
        ext_ref[0:CONV_HALO, :] = jnp.where(i == 0, 0.0, h0)
        ext_ref[CONV_HALO:, :] = u0
        acc = jnp.zeros((tm, c), F32) + cb_ref[...]
        for k in range(CONV_WIDTH):
            off = CONV_HALO - (CONV_WIDTH - 1) + k
            acc = acc + w_ref[k:k + 1, :] * ext_ref[off:off + tm, :]
        u1_ref[...] = acc
        mu = jnp.mean(acc, axis=-1, keepdims=True)
        xc = acc - mu
        rstd = lax.rsqrt(jnp.mean(xc * xc, axis=-1, keepdims=True) + EPS)
        u2 = (xc * rstd) * lg_ref[...] + lb_ref[...]
        u_ref[...] = (u2 * _sigmoid(u2)).astype(BF16)

    vec = pl.BlockSpec((1, c), lambda i: (0, 0))
    return pl.pallas_call(
        body, name=name, grid=(s // tm,),
        in_specs=[pl.BlockSpec((tm, 2 * c), lambda i: (i, 0)),
                  pl.BlockSpec((CONV_HALO, 2 * c), lambda i: (jnp.maximum(i * hb - 1, 0), 0)),
                  pl.BlockSpec((CONV_HALO, c), lambda i: (0, 0)), vec, vec, vec],
        out_specs=[pl.BlockSpec((tm, c), lambda i: (i, 0)), pl.BlockSpec((tm, c), lambda i: (i, 0))],
        out_shape=[_sds((s, c), F32), _sds((s, c), BF16)],
        scratch_shapes=[pltpu.VMEM((tm + CONV_HALO, c), F32)],
        compiler_params=_params("parallel"))(ag, ag, cw, cb, lg, lb)


def conv_bwd(du, u1, ag, cw, lg, lb, name):
    s = du.shape[0]
    tm = min(s, ROW_TILE)
    c = CONV_CH
    hb = tm // CONV_HALO
    nt = s // tm
    last_halo = s // CONV_HALO - 1

    def ln_silu_bwd(du_v, u1_v, lg_v, lb_v):
        mu = jnp.mean(u1_v, axis=-1, keepdims=True)
        xc = u1_v - mu
        rstd = lax.rsqrt(jnp.mean(xc * xc, axis=-1, keepdims=True) + EPS)
        xh = xc * rstd
        u2 = xh * lg_v + lb_v
        sg = _sigmoid(u2)
        du2 = du_v * (sg * (1.0 + u2 * (1.0 - sg)))
        dxh = du2 * lg_v
        du1 = rstd * (dxh - jnp.mean(dxh, axis=-1, keepdims=True) - xh * jnp.mean(dxh * xh, axis=-1, keepdims=True))
        return du1, du2, xh

    def body(du_ref, dun_ref, u1_ref, u1n_ref, ag_ref, agp_ref, w_ref, lg_ref, lb_ref, dag_ref, sm_ref, ext_ref, dext_ref):
        i = pl.program_id(0)
        lg_v, lb_v = lg_ref[...], lb_ref[...]
        du1, du2, xh = ln_silu_bwd(du_ref[...], u1_ref[...], lg_v, lb_v)
        du1n, _, _ = ln_silu_bwd(dun_ref[...], u1n_ref[...], lg_v, lb_v)
        dext_ref[0:tm, :] = du1
        dext_ref[tm:, :] = jnp.where(i == nt - 1, 0.0, du1n)
        a, g = ag_ref[:, :c], ag_ref[:, c:]
        sg = _sigmoid(g)
        ext_ref[0:CONV_HALO, :] = jnp.where(i == 0, 0.0, agp_ref[:, :c] * _sigmoid(agp_ref[:, c:]))
        ext_ref[CONV_HALO:, :] = a * sg

        @pl.when(i == 0)
        def _():
            sm_ref[...] = jnp.zeros_like(sm_ref)

        du0 = jnp.zeros((tm, c), F32)
        for k in range(CONV_WIDTH):
            back = CONV_WIDTH - 1 - k
            du0 = du0 + w_ref[k:k + 1, :] * dext_ref[back:back + tm, :]
            off = CONV_HALO - (CONV_WIDTH - 1) + k
            sm_ref[k:k + 1, :] += jnp.sum(du1 * ext_ref[off:off + tm, :], axis=0, keepdims=True)
        sm_ref[32:33, :] += jnp.sum(du1, axis=0, keepdims=True)
        sm_ref[33:34, :] += jnp.sum(du2 * xh, axis=0, keepdims=True)
        sm_ref[34:35, :] += jnp.sum(du2, axis=0, keepdims=True)
        dag_ref[:, :c] = (du0 * sg).astype(BF16)
        dag_ref[:, c:] = (du0 * a * (sg * (1.0 - sg))).astype(BF16)

    vec = pl.BlockSpec((1, c), lambda i: (0, 0))
    tile = pl.BlockSpec((tm, c), lambda i: (i, 0))
    nxt = pl.BlockSpec((CONV_HALO, c), lambda i: (jnp.minimum((i + 1) * hb, last_halo), 0))
    return pl.pallas_call(
        body, name=name, grid=(nt,),
        in_specs=[tile, nxt, tile, nxt,
                  pl.BlockSpec((tm, 2 * c), lambda i: (i, 0)),
                  pl.BlockSpec((CONV_HALO, 2 * c), lambda i: (jnp.maximum(i * hb - 1, 0), 0)),
                  pl.BlockSpec((CONV_HALO, c), lambda i: (0, 0)), vec, vec],
        out_specs=[pl.BlockSpec((tm, 2 * c), lambda i: (i, 0)), pl.BlockSpec((40, c), lambda i: (0, 0))],
        out_shape=[_sds((s, 2 * c), BF16), _sds((40, c), F32)],
        scratch_shapes=[pltpu.VMEM((tm + CONV_HALO, c), F32), pltpu.VMEM((tm + CONV_HALO, c), F32)],
        compiler_params=_params("arbitrary"))(du, du, u1, u1, ag, ag, cw, lg, lb)


CUM_BLOCK = 256


def _tri(n, upper):
    r = lax.broadcasted_iota(jnp.int32, (n, n), 0)
    cidx = lax.broadcasted_iota(jnp.int32, (n, n), 1)
    return jnp.where((r <= cidx) if upper else (r >= cidx), 1.0, 0.0).astype(F32)


def fox_gate_fwd(fl_t, bf, name):
    h, s = fl_t.shape
    nb = s // CUM_BLOCK

    def body(fl_ref, bf_ref, cum_ref):
        tri = _tri(CUM_BLOCK, True)
        carry = jnp.zeros((h, 1), F32)
        for b in range(nb):
            v = fl_ref[:, b * CUM_BLOCK:(b + 1) * CUM_BLOCK] + bf_ref[...]
            logf = jnp.minimum(v, 0.0) - jnp.log(1.0 + jnp.exp(-jnp.abs(v)))
            cs = jnp.dot(logf, tri, precision=lax.Precision.HIGHEST, preferred_element_type=F32) + carry
            cum_ref[:, b * CUM_BLOCK:(b + 1) * CUM_BLOCK] = cs
            carry = carry + jnp.sum(logf, axis=-1, keepdims=True)

    return pl.pallas_call(body, name=name, out_shape=_sds((h, s), F32),
                          compiler_params=pltpu.CompilerParams(vmem_limit_bytes=VMEM_LIMIT))(fl_t, bf)


def fox_gate_bwd(dcol_t, drow_t, fl_t, bf, name):
    h, s = fl_t.shape
    nb = s // CUM_BLOCK

    def body(dcol_ref, drow_ref, fl_ref, bf_ref, dfl_ref, dbf_ref):
        tri = _tri(CUM_BLOCK, False)
        carry = jnp.zeros((h, 1), F32)
        dbf = jnp.zeros((h, 1), F32)
        for b in reversed(range(nb)):
            sl = slice(b * CUM_BLOCK, (b + 1) * CUM_BLOCK)
            dcb = dcol_ref[:, sl] + drow_ref[:, sl]
            dlogf = jnp.dot(dcb, tri, precision=lax.Precision.HIGHEST, preferred_element_type=F32) + carry
            carry = carry + jnp.sum(dcb, axis=-1, keepdims=True)
            dfl = dlogf * _sigmoid(-(fl_ref[:, sl] + bf_ref[...]))
            dfl_ref[:, sl] = dfl
            dbf = dbf + jnp.sum(dfl, axis=-1, keepdims=True)
        dbf_ref[...] = dbf

    return pl.pallas_call(body, name=name, out_shape=[_sds((h, s), F32), _sds((h, 1), F32)],
                          compiler_params=pltpu.CompilerParams(vmem_limit_bytes=VMEM_LIMIT))(dcol_t, drow_t, fl_t, bf)


FOX_TILE = 512


def _causal_mask(tq, tk, q0, k0):
    row = lax.broadcasted_iota(jnp.int32, (tq, tk), 0) + q0
    col = lax.broadcasted_iota(jnp.int32, (tq, tk), 1) + k0
    return row >= col


def _fox_bias(c_ref, hh, q0, k0, t):
    c_q = jnp.max(c_ref[hh:hh + 1, pl.ds(q0, 128)], axis=-1, keepdims=True)
    return c_q - c_ref[hh:hh + 1, pl.ds(k0, t)]


def fox_fwd(qkv, cum4, name):
    s = qkv.shape[0]
    t = min(s, FOX_TILE)
    nq = s // t
    dn = _DOT_DIMS["nt"]

    def body(q_ref, k_ref, v_ref, c_ref, o_ref, lse_ref):
        i = pl.program_id(1)
        q0 = pl.multiple_of(i * t, t)
        lane = lax.broadcasted_iota(jnp.int32, (t, 128), 1)
        qv = q_ref[...]
        zero = jnp.zeros_like(qv)
        q_h = (jnp.where(lane < 64, qv, zero), jnp.where(lane >= 64, qv, zero))

        def step(j, carry, masked):
            k0 = pl.multiple_of(j * t, t)
            kj = k_ref[pl.ds(k0, t), :]
            vj = v_ref[pl.ds(k0, t), :]
            out = []
            for hh in range(2):
                m_prev, l_prev, acc_prev = carry[hh]
                bias = _fox_bias(c_ref, hh, q0, k0, t)
                sc = lax.dot_general(q_h[hh], kj, dn, preferred_element_type=F32) * FOX_SCALE + bias
                if masked:
                    sc = jnp.where(_causal_mask(t, t, q0, k0), sc, NEG_INF)
                m_new = jnp.maximum(m_prev, jnp.max(sc, axis=-1, keepdims=True))
                alpha = jnp.exp(m_prev - m_new)
                p = jnp.exp(sc - m_new)
                l_new = alpha * l_prev + jnp.sum(p, axis=-1, keepdims=True)
                acc_new = alpha * acc_prev + jnp.dot(p.astype(BF16), vj, preferred_element_type=F32)
                out.append((m_new, l_new, acc_new))
            return tuple(out)

        init = tuple((jnp.full((t, 1), NEG_INF, F32), jnp.zeros((t, 1), F32), jnp.zeros((t, 128), F32)) for _ in range(2))
        carry = lax.fori_loop(0, i, lambda j, cr: step(j, cr, False), init)
        carry = step(i, carry, True)
        (m_a, l_a, acc_a), (m_b, l_b, acc_b) = carry
        o_ref[...] = jnp.where(lane < 64, acc_a / l_a, acc_b / l_b).astype(BF16)
        lse_ref[0] = jnp.broadcast_to(m_a + jnp.log(l_a), (t, 128))
        lse_ref[1] = jnp.broadcast_to(m_b + jnp.log(l_b), (t, 128))

    return pl.pallas_call(
        body, name=name, grid=(4, nq),
        in_specs=[pl.BlockSpec((t, 128), lambda p, i: (i, p)),
                  pl.BlockSpec((s, 128), lambda p, i: (0, 4 + p)),
                  pl.BlockSpec((s, 128), lambda p, i: (0, 8 + p)),
                  pl.BlockSpec((None, 2, s), lambda p, i: (p, 0, 0))],
        out_specs=[pl.BlockSpec((t, 128), lambda p, i: (i, p)),
                   pl.BlockSpec((2, t, 128), lambda p, i: (p, i, 0))],
        out_shape=[_sds((s, FOX_WIDTH), BF16), _sds((FOX_HEADS, s, 128), F32)],
        compiler_params=_params("parallel", "arbitrary"))(qkv, qkv, qkv, cum4)


def fox_bwd(qkv, cum4, att, datt, lse, name):
    s = qkv.shape[0]
    t = min(s, FOX_TILE)
    nk = s // t
    nt_dims, tn_dims = _DOT_DIMS["nt"], _DOT_DIMS["tn"]

    def body(q_ref, k_ref, v_ref, c_ref, o_ref, do_ref, lse_ref, dq_ref, dk_ref, dv_ref, dc_ref, dr_ref,
             dq_acc, dk_acc, dv_acc, dr_acc):
        j = pl.program_id(1)
        k0 = pl.multiple_of(j * t, t)
        lane = lax.broadcasted_iota(jnp.int32, (t, 128), 1)
        lo = lane < 64
        kj, vj = k_ref[...], v_ref[...]
        zero = jnp.zeros_like(kj)
        k_h = (jnp.where(lo, kj, zero), jnp.where(lo, zero, kj))

        @pl.when(j == 0)
        def _():
            dq_acc[...] = jnp.zeros_like(dq_acc)
            dr_acc[...] = jnp.zeros_like(dr_acc)

        dk_acc[...] = jnp.zeros_like(dk_acc)
        dv_acc[...] = jnp.zeros_like(dv_acc)

        def step(i, dc, masked):
            q0 = pl.multiple_of(i * t, t)
            qi = q_ref[pl.ds(q0, t), :]
            doi = do_ref[pl.ds(q0, t), :]
            prod = doi.astype(F32) * o_ref[pl.ds(q0, t), :].astype(F32)
            zq = jnp.zeros_like(qi)
            dq_new = jnp.zeros((t, 128), F32)
            dc_out = []
            for hh in range(2):
                sel = lo if hh == 0 else jnp.logical_not(lo)
                q_m = jnp.where(sel, qi, zq)
                do_m = jnp.where(sel, doi, zq)
                delta = jnp.sum(jnp.where(sel, prod, 0.0), axis=-1, keepdims=True)
                bias = _fox_bias(c_ref, hh, q0, k0, t)
                sc = lax.dot_general(q_m, kj, nt_dims, preferred_element_type=F32) * FOX_SCALE + bias
                if masked:
                    sc = jnp.where(_causal_mask(t, t, q0, k0), sc, NEG_INF)
                lse_t = jnp.tile(lse_ref[hh, pl.ds(q0, t), :], (1, t // 128))
                p = jnp.exp(sc - lse_t)
                dv_acc[hh] += lax.dot_general(p.astype(BF16), doi, tn_dims, preferred_element_type=F32)
                dp = lax.dot_general(do_m, vj, nt_dims, preferred_element_type=F32)
                ds = p * (dp - delta)
                dc_out.append(dc[hh] - jnp.sum(ds, axis=0, keepdims=True))
                dr_acc[hh, pl.ds(q0, t), :] += jnp.sum(ds, axis=-1, keepdims=True)
                ds_b = ds.astype(BF16)
                dq_new = dq_new + jnp.dot(ds_b, k_h[hh], preferred_element_type=F32)
                dk_acc[hh] += lax.dot_general(ds_b, qi, tn_dims, preferred_element_type=F32)
            dq_acc[pl.ds(q0, t), :] += dq_new
            return tuple(dc_out)

        dc = step(j, (jnp.zeros((1, t), F32), jnp.zeros((1, t), F32)), True)
        dc = lax.fori_loop(j + 1, nk, lambda i, cr: step(i, cr, False), dc)
        dk_ref[...] = (jnp.where(lo, dk_acc[0], dk_acc[1]) * FOX_SCALE).astype(BF16)
        dv_ref[...] = jnp.where(lo, dv_acc[0], dv_acc[1]).astype(BF16)
        dc_ref[0:1, :] = dc[0]
        dc_ref[1:2, :] = dc[1]

        @pl.when(j == nk - 1)
        def _():
            dq_ref[...] = (dq_acc[...] * FOX_SCALE).astype(BF16)
            eye = lax.broadcasted_iota(jnp.int32, (t, t), 0) == lax.broadcasted_iota(jnp.int32, (t, t), 1)
            for hh in range(2):
                for b in range(nk):
                    col = dr_acc[hh, b * t:(b + 1) * t, :]
                    dr_ref[hh:hh + 1, b * t:(b + 1) * t] = jnp.sum(jnp.where(eye, col, 0.0), axis=0, keepdims=True)

    full = lambda col: pl.BlockSpec((s, 128), lambda p, j: (0, col(p)))
    blk = lambda col: pl.BlockSpec((t, 128), lambda p, j: (j, col(p)))
    return pl.pallas_call(
        body, name=name, grid=(4, nk),
        in_specs=[full(lambda p: p), blk(lambda p: 4 + p), blk(lambda p: 8 + p),
                  pl.BlockSpec((None, 2, s), lambda p, j: (p, 0, 0)),
                  full(lambda p: p), full(lambda p: p),
                  pl.BlockSpec((2, s, 128), lambda p, j: (p, 0, 0))],
        out_specs=[full(lambda p: p), blk(lambda p: p), blk(lambda p: p),
                   pl.BlockSpec((None, 2, t), lambda p, j: (p, 0, j)),
                   pl.BlockSpec((None, 2, s), lambda p, j: (p, 0, 0))],
        out_shape=[_sds((s, FOX_WIDTH), BF16), _sds((s, FOX_WIDTH), BF16), _sds((s, FOX_WIDTH), BF16),
                   _sds((4, 2, s), F32), _sds((4, 2, s), F32)],
        scratch_shapes=[pltpu.VMEM((s, 128), F32), pltpu.VMEM((2, t, 128), F32), pltpu.VMEM((2, t, 128), F32),
                        pltpu.VMEM((2, s, 1), F32)],
        compiler_params=_params("parallel", "arbitrary"))(qkv, qkv, qkv, cum4, att, datt, lse)


def _mem_probs(q_h, k_h):
    sc = lax.dot_general(q_h, k_h, _DOT_DIMS["nt"], preferred_element_type=F32) * MEM_SCALE
    e = jnp.exp(sc - jnp.max(sc, axis=-1, keepdims=True))
    return e / jnp.sum(e, axis=-1, keepdims=True)


def mem_attn_fwd(qm, km, vm, name):
    s = qm.shape[0]
    tm = min(s, ROW_TILE)

    def body(q_ref, k_ref, v_ref, o_ref):
        for h in range(MEM_HEADS):
            sl = slice(h * MEM_HEAD_DIM, (h + 1) * MEM_HEAD_DIM)
            p = _mem_probs(q_ref[:, sl], k_ref[:, sl])
            o_ref[:, sl] = jnp.dot(p.astype(BF16), v_ref[:, sl], preferred_element_type=F32).astype(BF16)

    kv = pl.BlockSpec((N_MEM, MEM_INNER), lambda i: (0, 0))
    row = pl.BlockSpec((tm, MEM_INNER), lambda i: (i, 0))
    return pl.pallas_call(body, name=name, grid=(s // tm,), in_specs=[row, kv, kv], out_specs=row,
                          out_shape=_sds((s, MEM_INNER), BF16), compiler_params=_params("parallel"))(qm, km, vm)


def mem_attn_bwd(qm, km, vm, dom, name):
    s = qm.shape[0]
    tm = min(s, ROW_TILE)
    tn_dims = _DOT_DIMS["tn"]

    def body(q_ref, k_ref, v_ref, do_ref, dq_ref, dk_ref, dv_ref):
        i = pl.program_id(0)

        @pl.when(i == 0)
        def _():
            dk_ref[...] = jnp.zeros_like(dk_ref)
            dv_ref[...] = jnp.zeros_like(dv_ref)

        for h in range(MEM_HEADS):
            sl = slice(h * MEM_HEAD_DIM, (h + 1) * MEM_HEAD_DIM)
            q_h, k_h, do_h = q_ref[:, sl], k_ref[:, sl], do_ref[:, sl]
            p = _mem_probs(q_h, k_h)
            dp = lax.dot_general(do_h, v_ref[:, sl], _DOT_DIMS["nt"], preferred_element_type=F32)
            ds = p * (dp - jnp.sum(p * dp, axis=-1, keepdims=True))
            ds_b = (ds * MEM_SCALE).astype(BF16)
            dq_ref[:, sl] = jnp.dot(ds_b, k_h, preferred_element_type=F32).astype(BF16)
            dk_ref[:, sl] += lax.dot_general(ds_b, q_h, tn_dims, preferred_element_type=F32)
            dv_ref[:, sl] += lax.dot_general(p.astype(BF16), do_h, tn_dims, preferred_element_type=F32)

    kv = pl.BlockSpec((N_MEM, MEM_INNER), lambda i: (0, 0))
    row = pl.BlockSpec((tm, MEM_INNER), lambda i: (i, 0))
    return pl.pallas_call(
        body, name=name, grid=(s // tm,), in_specs=[row, kv, kv, row], out_specs=[row, kv, kv],
        out_shape=[_sds((s, MEM_INNER), BF16), _sds((N_MEM, MEM_INNER), F32), _sds((N_MEM, MEM_INNER), F32)],
        compiler_params=_params("arbitrary"))(qm, km, vm, dom)


def adamw(w, g, m, v, name):
    r, c = w.shape
    tm = ROW_TILE if r % ROW_TILE == 0 else r
    c1 = 1.0 - ADAM_B1 ** ADAM_STEP
    c2 = 1.0 - ADAM_B2 ** ADAM_STEP

    def body(w_ref, g_ref, m_ref, v_ref, d_ref, nm_ref, nv_ref):
        gv = g_ref[...]
        nm = ADAM_B1 * m_ref[...] + (1.0 - ADAM_B1) * gv
        nv = ADAM_B2 * v_ref[...] + (1.0 - ADAM_B2) * (gv * gv)
        nm_ref[...] = nm
        nv_ref[...] = nv
        d_ref[...] = -ADAM_LR * ((nm / c1) / (jnp.sqrt(nv / c2) + ADAM_EPS) + ADAM_WD * w_ref[...])

    blk = pl.BlockSpec((tm, c), lambda i: (i, 0))
    return pl.pallas_call(body, name=name, grid=(r // tm,), in_specs=[blk] * 4, out_specs=[blk] * 3,
                          out_shape=[_sds((r, c), F32)] * 3, compiler_params=_params("parallel"))(w, g, m, v)


def _vec(v):
    return v.reshape(1, -1)


def layer_fwd(x0, mem, w, sm):
    s = x0.shape[0]
    h1 = rms_fwd(x0, _vec(sm["norm_mix_pre"]), "rms_mix_pre")
    ag, qkv, flp = matmul(
        h1, w["w_in"], mode="nn", dims=(s, IN_PAD, D_MODEL), tiles=(min(s, ROW_TILE), IN_PAD, D_MODEL), name="mix_in",
        out_shapes=[_sds((s, 2 * CONV_CH), F32), _sds((s, 3 * FOX_WIDTH), BF16), _sds((s, 128), F32)],
        out_specs=[pl.BlockSpec((min(s, ROW_TILE), 2 * CONV_CH), lambda i, j, k: (i, 0)),
                   pl.BlockSpec((min(s, ROW_TILE), 3 * FOX_WIDTH), lambda i, j, k: (i, 0)),
                   pl.BlockSpec((min(s, ROW_TILE), 128), lambda i, j, k: (i, 0))],
        epi=lambda acc: (acc[:, :2 * CONV_CH], acc[:, 2 * CONV_CH:2 * CONV_CH + 3 * FOX_WIDTH], acc[:, 2 * CONV_CH + 3 * FOX_WIDTH:]))
    u1, u = conv_fwd(ag, w["conv_w"], _vec(sm["conv_b"]), _vec(sm["conv_ln_g"]), _vec(sm["conv_ln_b"]), "conv_fwd")
    fl_t = flp[:, :FOX_HEADS].T
    cum = fox_gate_fwd(fl_t, sm["b_forget"].reshape(FOX_HEADS, 1), "fox_gate_fwd")
    cum4 = cum.reshape(4, 2, s)
    att, lse = fox_fwd(qkv, cum4, "fox_fwd")
    cat = jnp.concatenate([u, att], axis=1)
    tm = min(s, ROW_TILE)
    y1, = matmul(cat, w["w_out"], mode="nn", dims=(s, D_MODEL, D_MODEL), tiles=(tm, D_MODEL, D_MODEL), name="mix_out",
                 out_shapes=[_sds((s, D_MODEL), F32)])
    x1 = resid_post(x0, y1, _vec(sm["norm_mix_post"]), "post_mix")

    h2 = rms_fwd(x1, _vec(sm["norm_mem_pre"]), "rms_mem_pre")
    mn = rms_fwd(mem, _vec(sm["norm_memkv"]), "rms_memkv")
    qm, = matmul(h2, w["w_mq"], mode="nn", dims=(s, MEM_INNER, D_MODEL), tiles=(tm, MEM_INNER, D_MODEL), name="mem_q",
                 out_shapes=[_sds((s, MEM_INNER), BF16)])
    km, = matmul(mn, w["w_mk"], mode="nn", dims=(N_MEM, MEM_INNER, D_MODEL), tiles=(N_MEM, MEM_INNER, D_MODEL), name="mem_k",
                 out_shapes=[_sds((N_MEM, MEM_INNER), BF16)])
    vm, = matmul(mn, w["w_mv"], mode="nn", dims=(N_MEM, MEM_INNER, D_MODEL), tiles=(N_MEM, MEM_INNER, D_MODEL), name="mem_v",
                 out_shapes=[_sds((N_MEM, MEM_INNER), BF16)])
    om = mem_attn_fwd(qm, km, vm, "mem_attn_fwd")
    y2, = matmul(om, w["w_mo"], mode="nn", dims=(s, D_MODEL, MEM_INNER), tiles=(tm, D_MODEL, MEM_INNER), name="mem_o",
                 out_shapes=[_sds((s, D_MODEL), F32)])
    x2 = resid_post(x1, y2, _vec(sm["norm_mem_post"]), "post_mem")

    h3 = rms_fwd(x2, _vec(sm["norm_mlp_pre"]), "rms_mlp_pre")

    def relu2(acc):
        r = jnp.maximum(acc, 0.0)
        return (r * r,)

    act, = matmul(h3, w["w_up"], mode="nn", dims=(s, D_FF, D_MODEL), tiles=(tm, 1024, D_MODEL), name="mlp_up",
                  out_shapes=[_sds((s, D_FF), BF16)], epi=relu2)
    y3, = matmul(act, w["w_down"], mode="nn", dims=(s, D_MODEL, D_FF), tiles=(tm, D_MODEL, 1024), name="mlp_down",
                 out_shapes=[_sds((s, D_MODEL), F32)])
    x3 = resid_post(x2, y3, _vec(sm["norm_mlp_post"]), "post_mlp")
    saved = dict(x0=x0, h1=h1, ag=ag, qkv=qkv, fl_t=fl_t, u1=u1, cum4=cum4, att=att, lse=lse, cat=cat, y1=y1, x1=x1,
                 h2=h2, mn=mn, qm=qm, km=km, vm=vm, om=om, y2=y2, x2=x2, h3=h3, act=act, y3=y3)
    return x3, saved


def layer_bwd(dx3, mem, w, sm, sv):
    s = dx3.shape[0]
    tm = min(s, ROW_TILE)
    gw, gs = {}, {}
    dy3, gs["norm_mlp_post"] = rms_bwd(sv["y3"], _vec(sm["norm_mlp_post"]), dx3, "bwd_post_mlp", out_dtype=BF16)
    gw["w_down"], = matmul(sv["act"], dy3, mode="tn", dims=(D_FF, D_MODEL, s), tiles=(1024, D_MODEL, tm), name="dw_down",
                           out_shapes=[_sds((D_FF, D_MODEL), F32)])

    def dup_epi(acc, act_ref):
        return (acc * (2.0 * jnp.sqrt(act_ref[...].astype(F32))),)

    dup, = matmul(dy3, w["w_down"], mode="nt", dims=(s, D_FF, D_MODEL), tiles=(tm, 1024, D_MODEL), name="d_act",
                  out_shapes=[_sds((s, D_FF), BF16)], epi=dup_epi, extras=(sv["act"],),
                  extra_specs=(pl.BlockSpec((tm, 1024), lambda i, j, k: (i, j)),))
    gw["w_up"], = matmul(sv["h3"], dup, mode="tn", dims=(D_MODEL, D_FF, s), tiles=(512, 2048, tm), name="dw_up",
                         out_shapes=[_sds((D_MODEL, D_FF), F32)])
    dh3, = matmul(dup, w["w_up"], mode="nt", dims=(s, D_MODEL, D_FF), tiles=(tm, D_MODEL, 1024), name="d_h3",
                  out_shapes=[_sds((s, D_MODEL), F32)])
    dx2, gs["norm_mlp_pre"] = rms_bwd(sv["x2"], _vec(sm["norm_mlp_pre"]), dh3, "bwd_pre_mlp", add=dx3)

    dy2, gs["norm_mem_post"] = rms_bwd(sv["y2"], _vec(sm["norm_mem_post"]), dx2, "bwd_post_mem", out_dtype=BF16)
    gw["w_mo"], = matmul(sv["om"], dy2, mode="tn", dims=(MEM_INNER, D_MODEL, s), tiles=(MEM_INNER, D_MODEL, tm), name="dw_mo",
                         out_shapes=[_sds((MEM_INNER, D_MODEL), F32)])
    dom, = matmul(dy2, w["w_mo"], mode="nt", dims=(s, MEM_INNER, D_MODEL), tiles=(tm, MEM_INNER, D_MODEL), name="d_om",
                  out_shapes=[_sds((s, MEM_INNER), BF16)])
    dqm, dkm, dvm = mem_attn_bwd(sv["qm"], sv["km"], sv["vm"], dom, "mem_attn_bwd")
    gw["w_mq"], = matmul(sv["h2"], dqm, mode="tn", dims=(D_MODEL, MEM_INNER, s), tiles=(D_MODEL, MEM_INNER, tm), name="dw_mq",
                         out_shapes=[_sds((D_MODEL, MEM_INNER), F32)])
    dkm_b, dvm_b = dkm.astype(BF16), dvm.astype(BF16)
    gw["w_mk"], = matmul(sv["mn"], dkm_b, mode="tn", dims=(D_MODEL, MEM_INNER, N_MEM), tiles=(D_MODEL, MEM_INNER, N_MEM),
                         name="dw_mk", out_shapes=[_sds((D_MODEL, MEM_INNER), F32)])
    gw["w_mv"], = matmul(sv["mn"], dvm_b, mode="tn", dims=(D_MODEL, MEM_INNER, N_MEM), tiles=(D_MODEL, MEM_INNER, N_MEM),
                         name="dw_mv", out_shapes=[_sds((D_MODEL, MEM_INNER), F32)])
    dmn_k, = matmul(dkm_b, w["w_mk"], mode="nt", dims=(N_MEM, D_MODEL, MEM_INNER), tiles=(N_MEM, D_MODEL, MEM_INNER),
                    name="d_mn_k", out_shapes=[_sds((N_MEM, D_MODEL), F32)])
    dmn, = matmul(dvm_b, w["w_mv"], mode="nt", dims=(N_MEM, D_MODEL, MEM_INNER), tiles=(N_MEM, D_MODEL, MEM_INNER),
                  name="d_mn_v", out_shapes=[_sds((N_MEM, D_MODEL), F32)],
                  epi=lambda acc, other: (acc + other[...],), extras=(dmn_k,),
                  extra_specs=(pl.BlockSpec((N_MEM, D_MODEL), lambda i, j, k: (0, 0)),))
    _, gs["norm_memkv"] = rms_bwd(mem, _vec(sm["norm_memkv"]), dmn, "bwd_memkv")
    dh2, = matmul(dqm, w["w_mq"], mode="nt", dims=(s, D_MODEL, MEM_INNER), tiles=(tm, D_MODEL, MEM_INNER), name="d_h2",
                  out_shapes=[_sds((s, D_MODEL), F32)])
    dx1, gs["norm_mem_pre"] = rms_bwd(sv["x1"], _vec(sm["norm_mem_pre"]), dh2, "bwd_pre_mem", add=dx2)

    dy1, gs["norm_mix_post"] = rms_bwd(sv["y1"], _vec(sm["norm_mix_post"]), dx1, "bwd_post_mix", out_dtype=BF16)
    gw["w_out"], = matmul(sv["cat"], dy1, mode="tn", dims=(D_MODEL, D_MODEL, s), tiles=(D_MODEL, D_MODEL, tm), name="dw_out",
                          out_shapes=[_sds((D_MODEL, D_MODEL), F32)])
    du, datt = matmul(dy1, w["w_out"], mode="nt", dims=(s, D_MODEL, D_MODEL), tiles=(tm, D_MODEL, D_MODEL), name="d_cat",
                      out_shapes=[_sds((s, CONV_CH), F32), _sds((s, FOX_WIDTH), BF16)],
                      out_specs=[pl.BlockSpec((tm, CONV_CH), lambda i, j, k: (i, 0)), pl.BlockSpec((tm, FOX_WIDTH), lambda i, j, k: (i, 0))],
                      epi=lambda acc: (acc[:, :CONV_CH], acc[:, CONV_CH:]))
    dag, csm = conv_bwd(du, sv["u1"], sv["ag"], w["conv_w"], _vec(sm["conv_ln_g"]), _vec(sm["conv_ln_b"]), "conv_bwd")
    gw["conv_w"] = csm[:CONV_WIDTH]
    gs["conv_b"], gs["conv_ln_g"], gs["conv_ln_b"] = csm[32:33], csm[33:34], csm[34:35]
    dq, dk, dv, dcol4, drow4 = fox_bwd(sv["qkv"], sv["cum4"], sv["att"], datt, sv["lse"], "fox_bwd")
    dfl_t, dbf = fox_gate_bwd(dcol4.reshape(FOX_HEADS, s), drow4.reshape(FOX_HEADS, s), sv["fl_t"],
                              sm["b_forget"].reshape(FOX_HEADS, 1), "fox_gate_bwd")
    gs["b_forget"] = dbf.reshape(1, FOX_HEADS)
    dflp = jnp.pad(dfl_t.T.astype(BF16), ((0, 0), (0, 128 - FOX_HEADS)))
    dz = jnp.concatenate([dag, dq, dk, dv, dflp], axis=1)
    gw["w_in"], = matmul(sv["h1"], dz, mode="tn", dims=(D_MODEL, IN_PAD, s), tiles=(512, IN_PAD, tm), name="dw_in",
                         out_shapes=[_sds((D_MODEL, IN_PAD), F32)])
    dh1, = matmul(dz, w["w_in"], mode="nt", dims=(s, D_MODEL, IN_PAD), tiles=(tm, D_MODEL, IN_PAD), name="d_h1",
                  out_shapes=[_sds((s, D_MODEL), F32)])
    dx0, gs["norm_mix_pre"] = rms_bwd(sv["x0"], _vec(sm["norm_mix_pre"]), dh1, "bwd_pre_mix", add=dx1)
    return dx0, gw, gs


def local_step(x, mem, tgt, wl, sml):
    depth = len(wl)
    saved = []
    h = x
    for l in range(depth):
        h, sv = layer_fwd(h, mem, wl[l], sml[l])
        saved.append(sv)
    sq, dx = loss_head(h, tgt, "loss_head")
    gws, gss = [None] * depth, [None] * depth
    for l in reversed(range(depth)):
        dx, gws[l], gss[l] = layer_bwd(dx, mem, wl[l], sml[l], saved[l])
    return sq, dx, gws, gss


ANY = pl.BlockSpec(memory_space=pl.ANY)


def _place():
    x, y, c = lax.axis_index("x"), lax.axis_index("y"), lax.axis_index("c")
    chips = [(1 - x, y), (x, 1 - y), (1 - x, 1 - y)]
    return x, y, c, chips


def gather_weights(shards):
    n = len(shards)

    def body(*refs):
        ins, outs = refs[:n], refs[n:2 * n]
        send_sems, recv_sems = refs[2 * n:]
        x, y, c, chips = _place()
        own = 2 * x + y
        sib = (x, y, 1 - c)

        def copy(t, k, chip_id, layer, to, src=None):
            dst = outs[t].at[chip_id, layer]
            return pltpu.make_async_remote_copy(src_ref=dst if src is None else src, dst_ref=dst,
                                                send_sem=send_sems.at[t, k], recv_sem=recv_sems.at[t, k],
                                                device_id=to, device_id_type=MESH)

        first = [copy(t, k, own, c, (*chip, c), src=ins[t].at[c]) for t in range(n) for k, chip in enumerate(chips)]
        for cp in first:
            cp.start()
        passed = []
        for t in range(n):
            for k, (px, py) in enumerate(chips):
                copy(t, k, 2 * px + py, c, sib).wait_recv()
                fw = copy(t, 3 + k, 2 * px + py, c, sib)
                fw.start()
                passed.append(fw)
        for t in range(n):
            for k, (px, py) in enumerate(chips):
                copy(t, 3 + k, 2 * px + py, 1 - c, sib).wait_recv()
        for cp in first + passed:
            cp.wait_send()

    return pl.pallas_call(
        body, name="gather_weights",
        out_shape=[_sds((N_CHIPS,) + a.shape, a.dtype) for a in shards],
        in_specs=[ANY] * n, out_specs=[ANY] * n,
        scratch_shapes=[pltpu.SemaphoreType.DMA((n, 6)), pltpu.SemaphoreType.DMA((n, 6))],
    )(*shards)


def swap_layers(grads):
    n = len(grads)

    def body(*refs):
        ins, outs = refs[:n], refs[n:2 * n]
        send_sems, recv_sems = refs[2 * n:]
        x, y, c, _ = _place()
        cps = [pltpu.make_async_remote_copy(src_ref=ins[t].at[1 - c], dst_ref=outs[t], send_sem=send_sems.at[t],
                                            recv_sem=recv_sems.at[t], device_id=(x, y, 1 - c), device_id_type=MESH)
               for t in range(n)]
        for cp in cps:
            cp.start()
        for cp in cps:
            cp.wait()

    return pl.pallas_call(
        body, name="swap_layers", out_shape=[_sds(g.shape[1:], g.dtype) for g in grads],
        in_specs=[ANY] * n, out_specs=[ANY] * n,
        scratch_shapes=[pltpu.SemaphoreType.DMA((n,)), pltpu.SemaphoreType.DMA((n,))])(*grads)


def scatter_partials(parts):
    n = len(parts)

    def body(*refs):
        ins, outs = refs[:n], refs[n:2 * n]
        send_sems, recv_sems = refs[2 * n:]
        x, y, c, chips = _place()
        own = 2 * x + y
        cps = [pltpu.make_async_remote_copy(src_ref=ins[t].at[2 * px + py], dst_ref=outs[t].at[own],
                                            send_sem=send_sems.at[t, k], recv_sem=recv_sems.at[t, k],
                                            device_id=(px, py, c), device_id_type=MESH)
               for t in range(n) for k, (px, py) in enumerate(chips)]
        for cp in cps:
            cp.start()
        for t in range(n):
            for k, (px, py) in enumerate(chips):
                pltpu.make_async_remote_copy(src_ref=ins[t].at[own], dst_ref=outs[t].at[2 * px + py],
                                             send_sem=send_sems.at[t, k], recv_sem=recv_sems.at[t, k],
                                             device_id=(px, py, c), device_id_type=MESH).wait_recv()
        for cp in cps:
            cp.wait_send()

    return pl.pallas_call(
        body, name="scatter_partials", out_shape=[_sds(p.shape, p.dtype) for p in parts],
        in_specs=[ANY] * n, out_specs=[ANY] * n,
        scratch_shapes=[pltpu.SemaphoreType.DMA((n, 3)), pltpu.SemaphoreType.DMA((n, 3))])(*parts)


def share_reduced(reds):
    n = len(reds)

    def body(*refs):
        outs = refs[n:2 * n]
        send_sems, recv_sems = refs[2 * n:]
        x, y, c, _ = _place()
        cps = [pltpu.make_async_remote_copy(src_ref=outs[t].at[c], dst_ref=outs[t].at[c], send_sem=send_sems.at[t],
                                            recv_sem=recv_sems.at[t], device_id=(x, y, 1 - c), device_id_type=MESH)
               for t in range(n)]
        for cp in cps:
            cp.start()
        for t in range(n):
            pltpu.make_async_remote_copy(src_ref=outs[t].at[c], dst_ref=outs[t].at[1 - c], send_sem=send_sems.at[t],
                                         recv_sem=recv_sems.at[t], device_id=(x, y, 1 - c), device_id_type=MESH).wait_recv()
        for cp in cps:
            cp.wait_send()

    return pl.pallas_call(
        body, name="share_reduced", out_shape=[_sds(r.shape, r.dtype) for r in reds],
        in_specs=[ANY] * n, out_specs=[ANY] * n, input_output_aliases={t: t for t in range(n)},
        scratch_shapes=[pltpu.SemaphoreType.DMA((n,)), pltpu.SemaphoreType.DMA((n,))])(*reds)


def pair_sum(gs, other, place, name, wire_dtype):
    _, _, r, c = gs.shape
    tm = ROW_TILE if r % ROW_TILE == 0 else r

    def body(pl_ref, g_ref, o_ref, own_ref, wire_ref):
        k = pl.program_id(1)
        val = g_ref[...] + o_ref[...]
        wire_ref[...] = val.astype(wire_dtype)

        @pl.when(k == pl_ref[1])
        def _():
            own_ref[...] = val

    return pl.pallas_call(
        body, name=name,
        grid_spec=pltpu.PrefetchScalarGridSpec(
            num_scalar_prefetch=1, grid=(r // tm, N_CHIPS),
            in_specs=[pl.BlockSpec((None, None, tm, c), lambda i, k, p: (p[0], k, i, 0)),
                      pl.BlockSpec((None, tm, c), lambda i, k, p: (k, i, 0))],
            out_specs=[pl.BlockSpec((tm, c), lambda i, k, p: (i, 0)),
                       pl.BlockSpec((None, tm, c), lambda i, k, p: (k, i, 0))]),
        out_shape=[_sds((r, c), F32), _sds((N_CHIPS, r, c), wire_dtype)],
        compiler_params=_params("parallel", "arbitrary"))(place, gs, other)


def chip_sum(own, recv, place, name):
    r, c = own.shape
    tm = ROW_TILE if r % ROW_TILE == 0 else r

    def body(pl_ref, own_ref, a_ref, b_ref, c_ref, out_ref):
        out_ref[...] = ((own_ref[...] + a_ref[...].astype(F32)) + b_ref[...].astype(F32)) + c_ref[...].astype(F32)

    slot = lambda d: pl.BlockSpec((None, tm, c), lambda i, p: ((p[1] + d) % N_CHIPS, i, 0))
    return pl.pallas_call(
        body, name=name,
        grid_spec=pltpu.PrefetchScalarGridSpec(
            num_scalar_prefetch=1, grid=(r // tm,),
            in_specs=[pl.BlockSpec((tm, c), lambda i, p: (i, 0)), slot(1), slot(2), slot(3)],
            out_specs=pl.BlockSpec((None, tm, c), lambda i, p: (p[0], i, 0))),
        out_shape=_sds((2, r, c), F32),
        compiler_params=_params("parallel"))(place, own, recv, recv, recv)


def allreduce_small(packed):
    rows, cols = packed.shape

    def body(in_ref, out_ref, buf_ref, send_sems, recv_sems):
        x, y, c, _ = _place()
        me = 4 * x + 2 * y + c
        buf_ref[me] = in_ref[...]
        peers = [(x ^ (k >> 2), y ^ ((k >> 1) & 1), c ^ (k & 1)) for k in range(1, 8)]
        cps = [pltpu.make_async_remote_copy(src_ref=in_ref, dst_ref=buf_ref.at[me], send_sem=send_sems.at[k],
                                            recv_sem=recv_sems.at[k], device_id=peer, device_id_type=MESH)
               for k, peer in enumerate(peers)]
        for cp in cps:
            cp.start()
        for k, (px, py, pc) in enumerate(peers):
            pltpu.make_async_remote_copy(src_ref=in_ref, dst_ref=buf_ref.at[4 * px + 2 * py + pc], send_sem=send_sems.at[k],
                                         recv_sem=recv_sems.at[k], device_id=(px, py, pc), device_id_type=MESH).wait_recv()
        for cp in cps:
            cp.wait_send()
        total = buf_ref[0]
        for d in range(1, 8):
            total = total + buf_ref[d]
        out_ref[...] = total

    vm = pl.BlockSpec(memory_space=pltpu.VMEM)
    return pl.pallas_call(
        body, name="allreduce_small", out_shape=_sds((rows, cols), F32), in_specs=[vm], out_specs=vm,
        scratch_shapes=[pltpu.VMEM((8, rows, cols), F32), pltpu.SemaphoreType.DMA((7,)), pltpu.SemaphoreType.DMA((7,))])(packed)


def pack_small(get):
    rows = []
    for l in range(2):
        for nm in NORMS:
            rows.append(get(nm, l).reshape(-1))
        rows.append(jnp.concatenate([get("conv_b", l).reshape(-1), get("conv_ln_g", l).reshape(-1)]))
        rows.append(jnp.concatenate([get("conv_ln_b", l).reshape(-1), get("b_forget", l).reshape(-1),
                                     jnp.zeros((D_MODEL - CONV_CH - FOX_HEADS,), F32)]))
    rows += [jnp.zeros((D_MODEL,), F32)] * (SMALL_ROWS - len(rows))
    return jnp.stack(rows)


def unpack_small(packed):
    out = {}
    for idx, nm in enumerate(NORMS):
        out[nm] = jnp.stack([packed[9 * l + idx] for l in range(2)])
    out["conv_b"] = jnp.stack([packed[9 * l + 7, :CONV_CH] for l in range(2)])
    out["conv_ln_g"] = jnp.stack([packed[9 * l + 7, CONV_CH:] for l in range(2)])
    out["conv_ln_b"] = jnp.stack([packed[9 * l + 8, :CONV_CH] for l in range(2)])
    out["b_forget"] = jnp.stack([packed[9 * l + 8, CONV_CH:CONV_CH + FOX_HEADS] for l in range(2)])
    return out


def assemble_weights(gathered, l):
    cols = lambda a: jnp.moveaxis(a[:, l], 0, 1).reshape(a.shape[2], N_CHIPS * a.shape[3])
    rows = lambda a: a[:, l].reshape(N_CHIPS * a.shape[2], a.shape[3])
    w_in = jnp.pad(cols(gathered["w_in"]), ((0, 0), (0, IN_PAD - IN_COLS)))
    conv_w = jnp.pad(cols(gathered["conv_w"]), ((0, CONV_HALO - CONV_WIDTH), (0, 0)))
    return dict(w_in=w_in, conv_w=conv_w, w_out=rows(gathered["w_out"]), w_mq=rows(gathered["w_mq"]),
                w_mk=rows(gathered["w_mk"]), w_mv=rows(gathered["w_mv"]), w_mo=cols(gathered["w_mo"]),
                w_up=cols(gathered["w_up"]), w_down=rows(gathered["w_down"]))


def shard_layout(name, g):
    if name == "w_in":
        g = g[:, :IN_COLS]
    if name in ("w_in", "conv_w", "w_mo", "w_up"):
        return jnp.moveaxis(g.reshape(g.shape[0], N_CHIPS, g.shape[1] // N_CHIPS), 1, 0)
    return g.reshape(N_CHIPS, g.shape[0] // N_CHIPS, g.shape[1])


def kernel(x, mem, norm_mix_pre, norm_mix_post, w_in, b_forget, conv_w, conv_b, conv_ln_g, conv_ln_b, w_out, norm_mem_pre, norm_mem_post, norm_memkv, w_mq, w_mk, w_mv, w_mo, norm_mlp_pre, norm_mlp_post, w_up, w_down, loss_target, m_norm_mix_pre, m_norm_mix_post, m_w_in, m_b_forget, m_conv_w, m_conv_b, m_conv_ln_g, m_conv_ln_b, m_w_out, m_norm_mem_pre, m_norm_mem_post, m_norm_memkv, m_w_mq, m_w_mk, m_w_mv, m_w_mo, m_norm_mlp_pre, m_norm_mlp_post, m_w_up, m_w_down, v_norm_mix_pre, v_norm_mix_post, v_w_in, v_b_forget, v_conv_w, v_conv_b, v_conv_ln_g, v_conv_ln_b, v_w_out, v_norm_mem_pre, v_norm_mem_post, v_norm_memkv, v_w_mq, v_w_mk, v_w_mv, v_w_mo, v_norm_mlp_pre, v_norm_mlp_post, v_w_up, v_w_down):
    args = dict(locals())
    wts = {n: args[n] for n in WEIGHTS}
    mom = {n: args["m_" + n] for n in WEIGHTS}
    var = {n: args["v_" + n] for n in WEIGHTS}
    place = jnp.stack([lax.axis_index("c"), 2 * lax.axis_index("x") + lax.axis_index("y")]).astype(jnp.int32)

    wire = [wts[n] if n == "conv_w" else wts[n].astype(BF16) for n in BIG]
    gathered = {n: lax.dynamic_update_slice(g, own[None], (place[1], 0, 0, 0))
                for n, g, own in zip(BIG, gather_weights(wire), wire)}
    wl = [assemble_weights(gathered, l) for l in range(2)]
    sml = [{n: wts[n][l] for n in SMALL} for l in range(2)]

    sq, grad_x, gws, gss = local_step(x[0], mem[0], loss_target[0], wl, sml)
    loss = lax.psum(0.5 * jnp.sum(sq) / D_MODEL, ("x", "y", "c"))

    stacked = [jnp.stack([shard_layout(n, gws[l][n]) for l in range(2)]) for n in BIG]
    others = swap_layers(stacked)
    owns, wires = [], []
    for n, gs_, ot in zip(BIG, stacked, others):
        o, wv = pair_sum(gs_, ot, place, "pair_sum_" + n, F32 if n == "conv_w" else BF16)
        owns.append(o)
        wires.append(wv)
    recvd = scatter_partials(wires)
    reds = [chip_sum(o, rv, place, "chip_sum_" + n) for n, o, rv in zip(BIG, owns, recvd)]
    grads = dict(zip(BIG, share_reduced(reds)))

    g_small = unpack_small(allreduce_small(pack_small(lambda nm, l: gss[l][nm])))
    grads.update(g_small)

    delta, new_m, new_v = {}, {}, {}
    for n in BIG:
        shp = wts[n].shape
        two_d = lambda a: a.reshape(shp[0] * shp[1], shp[2])
        d_, m_, v_ = adamw(two_d(wts[n]), two_d(grads[n]), two_d(mom[n]), two_d(var[n]), "adamw_" + n)
        delta[n], new_m[n], new_v[n] = d_.reshape(shp), m_.reshape(shp), v_.reshape(shp)
    pk = lambda src: pack_small(lambda nm, l: src[nm][l])
    d_, m_, v_ = adamw(pk(wts), pk(grads), pk(mom), pk(var), "adamw_small")
    for res, packed in ((delta, d_), (new_m, m_), (new_v, v_)):
        res.update(unpack_small(packed))

    return (loss, grad_x[None], *[grads[n] for n in WEIGHTS], *[delta[n] for n in WEIGHTS],
            *[new_m[n] for n in WEIGHTS], *[new_v[n] for n in WEIGHTS])
```

```python
import jax
import jax.numpy as jnp
from jax import lax
from jax.experimental import pallas as pl
from jax.experimental.pallas import tpu as pltpu

F32, BF16 = jnp.float32, jnp.bfloat16
D_MODEL = 1024
CONV_CH = 512
CONV_WIDTH = 31
CONV_HALO = 32
FOX_WIDTH = 512
FOX_HEADS = 8
N_MEM = 256
MEM_HEADS = 4
MEM_HEAD_DIM = 128
MEM_INNER = 512
D_FF = 4096
IN_COLS = 2568
IN_PAD = 2688
GATE_COL = 2 * CONV_CH
QKV_COL = GATE_COL + 128
EPS = 1e-6
NEG_INF = -1e30
FOX_SCALE = 0.125
MEM_SCALE = MEM_HEAD_DIM ** -0.5
ADAM_LR, ADAM_B1, ADAM_B2, ADAM_EPS, ADAM_WD, ADAM_STEP = 0.001, 0.9, 0.999, 1e-08, 0.01, 10
VMEM_LIMIT = 56 * 1024 * 1024
ROW_TILE = 512
MESH = pl.DeviceIdType.MESH
N_CHIPS = 4

GROUP_A = ("w_in", "w_out")
GROUP_B = ("w_mq", "w_mk", "w_mv", "w_mo")
GROUP_C = ("w_up", "w_down")
BIG = GROUP_A + GROUP_B + GROUP_C
NORMS = ("norm_mix_pre", "norm_mix_post", "norm_mem_pre", "norm_mem_post", "norm_memkv", "norm_mlp_pre", "norm_mlp_post")
SMALL = NORMS + ("conv_b", "conv_ln_g", "conv_ln_b", "b_forget")
WEIGHTS = ("norm_mix_pre", "norm_mix_post", "w_in", "b_forget", "conv_w", "conv_b", "conv_ln_g", "conv_ln_b", "w_out",
           "norm_mem_pre", "norm_mem_post", "norm_memkv", "w_mq", "w_mk", "w_mv", "w_mo", "norm_mlp_pre", "norm_mlp_post",
           "w_up", "w_down")
SMALL_ROWS = 56

ANY = pl.BlockSpec(memory_space=pl.ANY)


def _sds(shape, dtype):
    return jax.ShapeDtypeStruct(shape, dtype)


class Comm:
    def __init__(self, ins, out_shapes, n_sems, start, finish, aliases=(), on_done=None):
        self.ins, self.out_shapes, self.n_sems = list(ins), list(out_shapes), n_sems
        self.start, self.finish, self.aliases, self.on_done = start, finish, tuple(aliases), on_done


def _call(body, *, name, grid, in_specs, out_specs, out_shape, args, scratch=(), sem=(), comms=(), aliases=None):
    comms = list(comms)
    n_in, n_out, n_scr = len(in_specs), len(out_specs), len(scratch)
    io_alias = dict(aliases or {})
    c_args, c_shapes, c_scratch = [], [], []
    for cm in comms:
        for i, o in cm.aliases:
            io_alias[n_in + len(c_args) + i] = n_out + len(c_shapes) + o
        c_args += cm.ins
        c_shapes += cm.out_shapes
        c_scratch += [pltpu.SemaphoreType.DMA((cm.n_sems,)), pltpu.SemaphoreType.DMA((cm.n_sems,))]
    rank = len(grid)

    def hosted(*refs):
        ins, c_in = refs[:n_in], refs[n_in:n_in + len(c_args)]
        outs = refs[n_in + len(c_args):n_in + len(c_args) + n_out]
        c_out = refs[n_in + len(c_args) + n_out:n_in + len(c_args) + n_out + len(c_shapes)]
        scr = refs[len(refs) - n_scr - 2 * len(comms):len(refs) - 2 * len(comms)]
        sems = refs[len(refs) - 2 * len(comms):]
        first = last = None
        for d in range(rank):
            f, e = pl.program_id(d) == 0, pl.program_id(d) == grid[d] - 1
            first = f if first is None else jnp.logical_and(first, f)
            last = e if last is None else jnp.logical_and(last, e)

        def each(which):
            a = b = 0
            for k, cm in enumerate(comms):
                getattr(cm, which)(c_in[a:a + len(cm.ins)], c_out[b:b + len(cm.out_shapes)], sems[2 * k], sems[2 * k + 1])
                a, b = a + len(cm.ins), b + len(cm.out_shapes)

        if rank:
            pl.when(first)(lambda: each("start"))
        else:
            each("start")
        body(*ins, *outs, *scr)
        if rank:
            pl.when(last)(lambda: each("finish"))
        else:
            each("finish")

    if comms:
        sem = ("arbitrary",) * rank
    params = pltpu.CompilerParams(dimension_semantics=sem, vmem_limit_bytes=VMEM_LIMIT) if rank else \
        pltpu.CompilerParams(vmem_limit_bytes=VMEM_LIMIT)
    kw = dict(grid=grid) if rank else {}
    res = pl.pallas_call(
        hosted if comms else body, name=name, in_specs=list(in_specs) + [ANY] * len(c_args),
        out_specs=list(out_specs) + [ANY] * len(c_shapes), out_shape=list(out_shape) + c_shapes,
        scratch_shapes=list(scratch) + c_scratch, input_output_aliases=io_alias, compiler_params=params, **kw)(*args, *c_args)
    base, rest = list(res[:n_out]), list(res[n_out:])
    for cm in comms:
        got, rest = rest[:len(cm.out_shapes)], rest[len(cm.out_shapes):]
        if cm.on_done is not None:
            cm.on_done(got)
    return base


def run_comm(comms, name):
    _call(lambda: None, name=name, grid=(), in_specs=[], out_specs=[], out_shape=[], args=[], comms=comms)


def _place():
    x, y, c = lax.axis_index("x"), lax.axis_index("y"), lax.axis_index("c")
    chips = [(1 - x, y), (x, 1 - y), (1 - x, 1 - y)]
    return x, y, c, chips


def _half(ref, h, lead=()):
    n = ref.shape[len(lead)] // 2
    return ref.at[(*lead, pl.ds(h * n, n))]


def _remote(src, dst, send_sems, recv_sems, k, to):
    return pltpu.make_async_remote_copy(src_ref=src, dst_ref=dst, send_sem=send_sems.at[k], recv_sem=recv_sems.at[k],
                                        device_id=to, device_id_type=MESH)


def gather_ici(shards, on_done):
    n = len(shards)

    def start(ins, outs, ss, rs):
        x, y, c, chips = _place()
        own = 2 * x + y
        for t in range(n):
            for k, chip in enumerate(chips):
                _remote(_half(ins[t], c), _half(outs[t], c, (own,)), ss, rs, 3 * t + k, (*chip, c)).start()

    def finish(ins, outs, ss, rs):
        x, y, c, chips = _place()
        for t in range(n):
            for k, (px, py) in enumerate(chips):
                cp = _remote(_half(ins[t], c), _half(outs[t], c, (2 * px + py,)), ss, rs, 3 * t + k, (px, py, c))
                cp.wait_recv()
                cp.wait_send()

    return Comm(shards, [_sds((N_CHIPS,) + a.shape, a.dtype) for a in shards], 3 * n, start, finish, on_done=on_done)


def gather_forward(bufs, on_done):
    n = len(bufs)

    def start(ins, outs, ss, rs):
        x, y, c, chips = _place()
        for t in range(n):
            for k, (px, py) in enumerate(chips):
                blk = _half(outs[t], c, (2 * px + py,))
                _remote(blk, blk, ss, rs, 3 * t + k, (x, y, 1 - c)).start()

    def finish(ins, outs, ss, rs):
        x, y, c, chips = _place()
        for t in range(n):
            for k, (px, py) in enumerate(chips):
                cp = _remote(_half(outs[t], c, (2 * px + py,)), _half(outs[t], 1 - c, (2 * px + py,)), ss, rs, 3 * t + k, (x, y, 1 - c))
                cp.wait_recv()
                cp.wait_send()

    return Comm(bufs, [_sds(b.shape, b.dtype) for b in bufs], 3 * n, start, finish,
                aliases=[(t, t) for t in range(n)], on_done=on_done)


def swap_halves(grads, on_done):
    n = len(grads)

    def copies(ins, outs, ss, rs):
        x, y, c, _ = _place()
        out = []
        for t in range(n):
            for k in range(N_CHIPS):
                out.append(_remote(_half(ins[t], 1 - c, (k,)), outs[t].at[k], ss, rs, N_CHIPS * t + k, (x, y, 1 - c)))
        return out

    def start(ins, outs, ss, rs):
        for cp in copies(ins, outs, ss, rs):
            cp.start()

    def finish(ins, outs, ss, rs):
        for cp in copies(ins, outs, ss, rs):
            cp.wait()

    return Comm(grads, [_sds((N_CHIPS, g.shape[1] // 2, g.shape[2]), g.dtype) for g in grads], N_CHIPS * n, start, finish,
                on_done=on_done)


def scatter_partials(parts, on_done):
    n = len(parts)

    def start(ins, outs, ss, rs):
        x, y, c, chips = _place()
        own = 2 * x + y
        for t in range(n):
            for k, (px, py) in enumerate(chips):
                _remote(ins[t].at[2 * px + py], outs[t].at[own], ss, rs, 3 * t + k, (px, py, c)).start()

    def finish(ins, outs, ss, rs):
        x, y, c, chips = _place()
        own = 2 * x + y
        for t in range(n):
            for k, (px, py) in enumerate(chips):
                cp = _remote(ins[t].at[own], outs[t].at[2 * px + py], ss, rs, 3 * t + k, (px, py, c))
                cp.wait_recv()
                cp.wait_send()

    return Comm(parts, [_sds(p.shape, p.dtype) for p in parts], 3 * n, start, finish, on_done=on_done)


def share_halves(fins, on_done):
    n = len(fins)

    def copies(outs, ss, rs, c, to):
        return [_remote(_half(outs[t], c), _half(outs[t], c), ss, rs, t, to) for t in range(n)]

    def start(ins, outs, ss, rs):
        x, y, c, _ = _place()
        for cp in copies(outs, ss, rs, c, (x, y, 1 - c)):
            cp.start()

    def finish(ins, outs, ss, rs):
        x, y, c, _ = _place()
        for t in range(n):
            cp = _remote(_half(outs[t], c), _half(outs[t], 1 - c), ss, rs, t, (x, y, 1 - c))
            cp.wait_recv()
            cp.wait_send()

    return Comm(fins, [_sds(f.shape, f.dtype) for f in fins], n, start, finish, aliases=[(t, t) for t in range(n)], on_done=on_done)


def pair_sum(g, other, place, name):
    _, r, c = g.shape
    h = r // 2
    tm = ROW_TILE if h % ROW_TILE == 0 else h
    nb = h // tm

    def body(pl_ref, g_ref, o_ref, own_ref, wire_ref):
        k = pl.program_id(1)
        val = g_ref[...] + o_ref[...]
        wire_ref[...] = val.astype(BF16)

        @pl.when(k == pl_ref[1])
        def _():
            own_ref[...] = val

    return pl.pallas_call(
        body, name=name,
        grid_spec=pltpu.PrefetchScalarGridSpec(
            num_scalar_prefetch=1, grid=(nb, N_CHIPS),
            in_specs=[pl.BlockSpec((None, tm, c), lambda i, k, p: (k, p[0] * nb + i, 0)),
                      pl.BlockSpec((None, tm, c), lambda i, k, p: (k, i, 0))],
            out_specs=[pl.BlockSpec((tm, c), lambda i, k, p: (i, 0)),
                       pl.BlockSpec((None, tm, c), lambda i, k, p: (k, i, 0))]),
        out_shape=[_sds((h, c), F32), _sds((N_CHIPS, h, c), BF16)],
        compiler_params=pltpu.CompilerParams(dimension_semantics=("parallel", "arbitrary"), vmem_limit_bytes=VMEM_LIMIT))(place, g, other)


def chip_sum(own, recv, place, name):
    h, c = own.shape
    tm = ROW_TILE if h % ROW_TILE == 0 else h
    nb = h // tm

    def body(pl_ref, own_ref, a_ref, b_ref, c_ref, out_ref):
        out_ref[...] = ((own_ref[...] + a_ref[...].astype(F32)) + b_ref[...].astype(F32)) + c_ref[...].astype(F32)

    slot = lambda d: pl.BlockSpec((None, tm, c), lambda i, p: ((p[1] + d) % N_CHIPS, i, 0))
    return pl.pallas_call(
        body, name=name,
        grid_spec=pltpu.PrefetchScalarGridSpec(
            num_scalar_prefetch=1, grid=(nb,),
            in_specs=[pl.BlockSpec((tm, c), lambda i, p: (i, 0)), slot(1), slot(2), slot(3)],
            out_specs=pl.BlockSpec((tm, c), lambda i, p: (p[0] * nb + i, 0))),
        out_shape=_sds((2 * h, c), F32),
        compiler_params=pltpu.CompilerParams(dimension_semantics=("parallel",), vmem_limit_bytes=VMEM_LIMIT))(place, own, recv, recv, recv)


def allreduce_small(packed):
    rows, cols = packed.shape

    def body(in_ref, out_ref, buf_ref, send_sems, recv_sems):
        x, y, c, _ = _place()
        me = 4 * x + 2 * y + c
        buf_ref[me] = in_ref[...]
        peers = [(x ^ (k >> 2), y ^ ((k >> 1) & 1), c ^ (k & 1)) for k in range(1, 8)]
        cps = [_remote(in_ref, buf_ref.at[me], send_sems, recv_sems, k, peer) for k, peer in enumerate(peers)]
        for cp in cps:
            cp.start()
        for k, (px, py, pc) in enumerate(peers):
            _remote(in_ref, buf_ref.at[4 * px + 2 * py + pc], send_sems, recv_sems, k, (px, py, pc)).wait_recv()
        for cp in cps:
            cp.wait_send()
        total = buf_ref[0]
        for d in range(1, 8):
            total = total + buf_ref[d]
        out_ref[...] = total

    vm = pl.BlockSpec(memory_space=pltpu.VMEM)
    return pl.pallas_call(
        body, name="allreduce_small", out_shape=_sds((rows, cols), F32), in_specs=[vm], out_specs=vm,
        scratch_shapes=[pltpu.VMEM((8, rows, cols), F32), pltpu.SemaphoreType.DMA((7,)), pltpu.SemaphoreType.DMA((7,))])(packed)


def rms_fwd(x, g, name):
    m, d = x.shape
    tm = min(m, ROW_TILE)

    def body(x_ref, g_ref, h_ref):
        xv = x_ref[...]
        r = lax.rsqrt(jnp.mean(xv * xv, axis=-1, keepdims=True) + EPS)
        h_ref[...] = ((xv * r) * g_ref[...]).astype(BF16)

    return _call(body, name=name, grid=(m // tm,),
                 in_specs=[pl.BlockSpec((tm, d), lambda i: (i, 0)), pl.BlockSpec((1, d), lambda i: (0, 0))],
                 out_specs=[pl.BlockSpec((tm, d), lambda i: (i, 0))], out_shape=[_sds((m, d), BF16)],
                 args=[x, g], sem=("parallel",))[0]


def resid_post(x, y, g, name):
    m, d = x.shape
    tm = min(m, ROW_TILE)

    def body(x_ref, y_ref, g_ref, o_ref):
        yv = y_ref[...]
        r = lax.rsqrt(jnp.mean(yv * yv, axis=-1, keepdims=True) + EPS)
        o_ref[...] = x_ref[...] + (yv * r) * g_ref[...]

    row = pl.BlockSpec((tm, d), lambda i: (i, 0))
    return _call(body, name=name, grid=(m // tm,), in_specs=[row, row, pl.BlockSpec((1, d), lambda i: (0, 0))],
                 out_specs=[row], out_shape=[_sds((m, d), F32)], args=[x, y, g], sem=("parallel",))[0]


def rms_bwd(y, g, dout, name, add=None, out_dtype=F32, comms=()):
    m, d = y.shape
    tm = min(m, ROW_TILE)
    has_add = add is not None

    def body(*refs):
        y_ref, g_ref, d_ref = refs[:3]
        dy_ref, dg_ref = refs[-2:]
        i = pl.program_id(0)
        yv = y_ref[...]
        dv = d_ref[...].astype(F32)
        r = lax.rsqrt(jnp.mean(yv * yv, axis=-1, keepdims=True) + EPS)
        gy = g_ref[...] * dv
        t = jnp.mean(yv * gy, axis=-1, keepdims=True) * (r * r)
        dy = r * (gy - yv * t)
        if has_add:
            dy = dy + refs[3][...]
        dy_ref[...] = dy.astype(dy_ref.dtype)
        part = jnp.sum(dv * (yv * r), axis=0, keepdims=True)

        @pl.when(i == 0)
        def _():
            dg_ref[...] = part

        @pl.when(i > 0)
        def _():
            dg_ref[...] += part

    row = pl.BlockSpec((tm, d), lambda i: (i, 0))
    vec = pl.BlockSpec((1, d), lambda i: (0, 0))
    return _call(body, name=name, grid=(m // tm,), in_specs=[row, vec, row] + ([row] if has_add else []),
                 out_specs=[row, vec], out_shape=[_sds((m, d), out_dtype), _sds((1, d), F32)],
                 args=[y, g, dout] + ([add] if has_add else []), sem=("arbitrary",), comms=comms)


def loss_head(xf, tgt, name):
    m, d = xf.shape
    tm = min(m, ROW_TILE)

    def body(x_ref, t_ref, s_ref, dx_ref):
        i = pl.program_id(0)
        err = x_ref[...] - t_ref[...]
        dx_ref[...] = err * (1.0 / d)
        part = jnp.sum(err * err, axis=0, keepdims=True)

        @pl.when(i == 0)
        def _():
            s_ref[...] = part

        @pl.when(i > 0)
        def _():
            s_ref[...] += part

    row = pl.BlockSpec((tm, d), lambda i: (i, 0))
    vec = pl.BlockSpec((1, d), lambda i: (0, 0))
    return _call(body, name=name, grid=(m // tm,), in_specs=[row, row], out_specs=[vec, row],
                 out_shape=[_sds((1, d), F32), _sds((m, d), F32)], args=[xf, tgt], sem=("arbitrary",))


_DOT_DIMS = {"nn": (((1,), (0,)), ((), ())), "nt": (((1,), (1,)), ((), ())), "tn": (((0,), (0,)), ((), ()))}


def matmul(a, b, *, mode, dims, tiles, name, out_shapes, out_specs=None, epi=None, extras=(), extra_specs=(),
           a_spec=None, b_spec=None, comms=(), aliases=None):
    m, n, k = dims
    tm, tn, tk = tiles
    nk = k // tk
    assert m % tm == 0 and n % tn == 0 and k % tk == 0
    n_ex, n_out = len(extras), len(out_shapes)
    if a_spec is None:
        a_spec = pl.BlockSpec((tk, tm), lambda i, j, kk: (kk, i)) if mode == "tn" else pl.BlockSpec((tm, tk), lambda i, j, kk: (i, kk))
    if b_spec is None:
        b_spec = pl.BlockSpec((tn, tk), lambda i, j, kk: (j, kk)) if mode == "nt" else pl.BlockSpec((tk, tn), lambda i, j, kk: (kk, j))
    if out_specs is None:
        out_specs = [pl.BlockSpec((tm, tn), lambda i, j, kk: (i, j)) for _ in out_shapes]
    if epi is None:
        epi = lambda acc: (acc,)

    def body(*refs):
        a_ref, b_ref = refs[0], refs[1]
        ex = refs[2:2 + n_ex]
        outs = refs[2 + n_ex:2 + n_ex + n_out]
        part = lax.dot_general(a_ref[...], b_ref[...], _DOT_DIMS[mode], preferred_element_type=F32)

        def finish(acc):
            for o_ref, val in zip(outs, epi(acc, *ex)):
                o_ref[...] = val.astype(o_ref.dtype)

        if nk == 1:
            finish(part)
        else:
            acc_ref = refs[-1]
            kk = pl.program_id(2)

            @pl.when(kk == 0)
            def _():
                acc_ref[...] = part

            @pl.when(kk > 0)
            def _():
                acc_ref[...] += part

            @pl.when(kk == nk - 1)
            def _():
                finish(acc_ref[...])

    return _call(body, name=name, grid=(m // tm, n // tn, nk), in_specs=[a_spec, b_spec] + list(extra_specs),
                 out_specs=list(out_specs), out_shape=list(out_shapes), args=[a, b, *extras],
                 scratch=[pltpu.VMEM((tm, tn), F32)] if nk > 1 else [], sem=("parallel", "parallel", "arbitrary"),
                 comms=comms, aliases=aliases)


def _sigmoid(v):
    return 1.0 / (1.0 + jnp.exp(-v))


def conv_fwd(ag, cw, cb, lg, lb, name, comms=()):
    s = ag.shape[0]
    tm = min(s, ROW_TILE)
    c = CONV_CH
    hb = tm // CONV_HALO

    def body(ag_ref, halo_ref, w_ref, cb_ref, lg_ref, lb_ref, u1_ref, u_ref, ext_ref):
        i = pl.program_id(0)
        u0 = ag_ref[:, :c] * _sigmoid(ag_ref[:, c:])
        h0 = halo_ref[:, :c] * _sigmoid(halo_ref[:, c:])
        ext_ref[0:CONV_HALO, :] = jnp.where(i == 0, 0.0, h0)
        ext_ref[CONV_HALO:, :] = u0
        acc = jnp.zeros((tm, c), F32) + cb_ref[...]
        for k in range(CONV_WIDTH):
            off = CONV_HALO - (CONV_WIDTH - 1) + k
            acc = acc + w_ref[k:k + 1, :] * ext_ref[off:off + tm, :]
        u1_ref[...] = acc
        mu = jnp.mean(acc, axis=-1, keepdims=True)
        xc = acc - mu
        rstd = lax.rsqrt(jnp.mean(xc * xc, axis=-1, keepdims=True) + EPS)
        u2 = (xc * rstd) * lg_ref[...] + lb_ref[...]
        u_ref[...] = (u2 * _sigmoid(u2)).astype(BF16)

    vec = pl.BlockSpec((1, c), lambda i: (0, 0))
    return _call(body, name=name, grid=(s // tm,),
                 in_specs=[pl.BlockSpec((tm, 2 * c), lambda i: (i, 0)),
                           pl.BlockSpec((CONV_HALO, 2 * c), lambda i: (jnp.maximum(i * hb - 1, 0), 0)),
                           pl.BlockSpec((CONV_HALO, c), lambda i: (0, 0)), vec, vec, vec],
                 out_specs=[pl.BlockSpec((tm, c), lambda i: (i, 0)), pl.BlockSpec((tm, c), lambda i: (i, 0))],
                 out_shape=[_sds((s, c), F32), _sds((s, 2 * c), BF16)],
                 args=[ag, ag, cw, cb, lg, lb], scratch=[pltpu.VMEM((tm + CONV_HALO, c), F32)], sem=("parallel",), comms=comms)


def conv_bwd(du, u1, ag, cw, lg, lb, name, comms=()):
    s = du.shape[0]
    tm = min(s, ROW_TILE)
    c = CONV_CH
    hb = tm // CONV_HALO
    nt = s // tm
    last_halo = s // CONV_HALO - 1

    def ln_silu_bwd(du_v, u1_v, lg_v, lb_v):
        mu = jnp.mean(u1_v, axis=-1, keepdims=True)
        xc = u1_v - mu
        rstd = lax.rsqrt(jnp.mean(xc * xc, axis=-1, keepdims=True) + EPS)
        xh = xc * rstd
        u2 = xh * lg_v + lb_v
        sg = _sigmoid(u2)
        du2 = du_v * (sg * (1.0 + u2 * (1.0 - sg)))
        dxh = du2 * lg_v
        du1 = rstd * (dxh - jnp.mean(dxh, axis=-1, keepdims=True) - xh * jnp.mean(dxh * xh, axis=-1, keepdims=True))
        return du1, du2, xh

    def body(du_ref, dun_ref, u1_ref, u1n_ref, ag_ref, agp_ref, w_ref, lg_ref, lb_ref, dag_ref, sm_ref, ext_ref, dext_ref):
        i = pl.program_id(0)
        lg_v, lb_v = lg_ref[...], lb_ref[...]
        du1, du2, xh = ln_silu_bwd(du_ref[...], u1_ref[...], lg_v, lb_v)
        du1n, _, _ = ln_silu_bwd(dun_ref[...], u1n_ref[...], lg_v, lb_v)
        dext_ref[0:tm, :] = du1
        dext_ref[tm:, :] = jnp.where(i == nt - 1, 0.0, du1n)
        a, g = ag_ref[:, :c], ag_ref[:, c:]
        sg = _sigmoid(g)
        ext_ref[0:CONV_HALO, :] = jnp.where(i == 0, 0.0, agp_ref[:, :c] * _sigmoid(agp_ref[:, c:]))
        ext_ref[CONV_HALO:, :] = a * sg

        @pl.when(i == 0)
        def _():
            sm_ref[...] = jnp.zeros_like(sm_ref)

        du0 = jnp.zeros((tm, c), F32)
        for k in range(CONV_WIDTH):
            back = CONV_WIDTH - 1 - k
            du0 = du0 + w_ref[k:k + 1, :] * dext_ref[back:back + tm, :]
            off = CONV_HALO - (CONV_WIDTH - 1) + k
            sm_ref[k:k + 1, :] += jnp.sum(du1 * ext_ref[off:off + tm, :], axis=0, keepdims=True)
        sm_ref[32:33, :] += jnp.sum(du1, axis=0, keepdims=True)
        sm_ref[33:34, :] += jnp.sum(du2 * xh, axis=0, keepdims=True)
        sm_ref[34:35, :] += jnp.sum(du2, axis=0, keepdims=True)
        dag_ref[:, :c] = (du0 * sg).astype(BF16)
        dag_ref[:, c:] = (du0 * a * (sg * (1.0 - sg))).astype(BF16)

    vec = pl.BlockSpec((1, c), lambda i: (0, 0))
    tile = pl.BlockSpec((tm, c), lambda i: (i, 0))
    nxt = pl.BlockSpec((CONV_HALO, c), lambda i: (jnp.minimum((i + 1) * hb, last_halo), 0))
    return _call(body, name=name, grid=(nt,),
                 in_specs=[tile, nxt, tile, nxt,
                           pl.BlockSpec((tm, 2 * c), lambda i: (i, 0)),
                           pl.BlockSpec((CONV_HALO, 2 * c), lambda i: (jnp.maximum(i * hb - 1, 0), 0)),
                           pl.BlockSpec((CONV_HALO, c), lambda i: (0, 0)), vec, vec],
                 out_specs=[pl.BlockSpec((tm, 2 * c), lambda i: (i, 0)), pl.BlockSpec((40, c), lambda i: (0, 0))],
                 out_shape=[_sds((s, IN_PAD), BF16), _sds((40, c), F32)],
                 args=[du, du, u1, u1, ag, ag, cw, lg, lb],
                 scratch=[pltpu.VMEM((tm + CONV_HALO, c), F32), pltpu.VMEM((tm + CONV_HALO, c), F32)], sem=("arbitrary",), comms=comms)


CUM_BLOCK = 256


def _tri(n, upper):
    r = lax.broadcasted_iota(jnp.int32, (n, n), 0)
    cidx = lax.broadcasted_iota(jnp.int32, (n, n), 1)
    return jnp.where((r <= cidx) if upper else (r >= cidx), 1.0, 0.0).astype(F32)


def fox_gate_fwd(fl_t, bf, name):
    h, s = fl_t.shape
    nb = s // CUM_BLOCK

    def body(fl_ref, bf_ref, cum_ref):
        tri = _tri(CUM_BLOCK, True)
        carry = jnp.zeros((h, 1), F32)
        for b in range(nb):
            v = fl_ref[:, b * CUM_BLOCK:(b + 1) * CUM_BLOCK] + bf_ref[...]
            logf = jnp.minimum(v, 0.0) - jnp.log(1.0 + jnp.exp(-jnp.abs(v)))
            cs = jnp.dot(logf, tri, precision=lax.Precision.HIGHEST, preferred_element_type=F32) + carry
            cum_ref[:, b * CUM_BLOCK:(b + 1) * CUM_BLOCK] = cs
            carry = carry + jnp.sum(logf, axis=-1, keepdims=True)

    return pl.pallas_call(body, name=name, out_shape=_sds((h, s), F32),
                          compiler_params=pltpu.CompilerParams(vmem_limit_bytes=VMEM_LIMIT))(fl_t, bf)


def fox_gate_bwd(dcol_t, drow_t, fl_t, bf, name):
    h, s = fl_t.shape
    nb = s // CUM_BLOCK

    def body(dcol_ref, drow_ref, fl_ref, bf_ref, dfl_ref, dbf_ref):
        tri = _tri(CUM_BLOCK, False)
        carry = jnp.zeros((h, 1), F32)
        dbf = jnp.zeros((h, 1), F32)
        for b in reversed(range(nb)):
            sl = slice(b * CUM_BLOCK, (b + 1) * CUM_BLOCK)
            dcb = dcol_ref[:, sl] + drow_ref[:, sl]
            dlogf = jnp.dot(dcb, tri, precision=lax.Precision.HIGHEST, preferred_element_type=F32) + carry
            carry = carry + jnp.sum(dcb, axis=-1, keepdims=True)
            dfl = dlogf * _sigmoid(-(fl_ref[:, sl] + bf_ref[...]))
            dfl_ref[:, sl] = dfl
            dbf = dbf + jnp.sum(dfl, axis=-1, keepdims=True)
        dbf_ref[...] = dbf

    return pl.pallas_call(body, name=name, out_shape=[_sds((h, s), F32), _sds((h, 1), F32)],
                          compiler_params=pltpu.CompilerParams(vmem_limit_bytes=VMEM_LIMIT))(dcol_t, drow_t, fl_t, bf)


FOX_TILE = 512


def _causal_mask(tq, tk, q0, k0):
    row = lax.broadcasted_iota(jnp.int32, (tq, tk), 0) + q0
    col = lax.broadcasted_iota(jnp.int32, (tq, tk), 1) + k0
    return row >= col


def _fox_bias(c_ref, hh, q0, k0, t):
    c_q = jnp.max(c_ref[hh:hh + 1, pl.ds(q0, 128)], axis=-1, keepdims=True)
    return c_q - c_ref[hh:hh + 1, pl.ds(k0, t)]


def fox_fwd(qkv, cum4, cat, name, comms=()):
    s = qkv.shape[0]
    t = min(s, FOX_TILE)
    nq = s // t
    dn = _DOT_DIMS["nt"]

    def body(q_ref, k_ref, v_ref, c_ref, cat_ref, o_ref, lse_ref):
        i = pl.program_id(1)
        q0 = pl.multiple_of(i * t, t)
        lane = lax.broadcasted_iota(jnp.int32, (t, 128), 1)
        qv = q_ref[...]
        zero = jnp.zeros_like(qv)
        q_h = (jnp.where(lane < 64, qv, zero), jnp.where(lane >= 64, qv, zero))

        def step(j, carry, masked):
            k0 = pl.multiple_of(j * t, t)
            kj = k_ref[pl.ds(k0, t), :]
            vj = v_ref[pl.ds(k0, t), :]
            out = []
            for hh in range(2):
                m_prev, l_prev, acc_prev = carry[hh]
                bias = _fox_bias(c_ref, hh, q0, k0, t)
                sc = lax.dot_general(q_h[hh], kj, dn, preferred_element_type=F32) * FOX_SCALE + bias
                if masked:
                    sc = jnp.where(_causal_mask(t, t, q0, k0), sc, NEG_INF)
                m_new = jnp.maximum(m_prev, jnp.max(sc, axis=-1, keepdims=True))
                alpha = jnp.exp(m_prev - m_new)
                p = jnp.exp(sc - m_new)
                l_new = alpha * l_prev + jnp.sum(p, axis=-1, keepdims=True)
                acc_new = alpha * acc_prev + jnp.dot(p.astype(BF16), vj, preferred_element_type=F32)
                out.append((m_new, l_new, acc_new))
            return tuple(out)

        init = tuple((jnp.full((t, 1), NEG_INF, F32), jnp.zeros((t, 1), F32), jnp.zeros((t, 128), F32)) for _ in range(2))
        carry = lax.fori_loop(0, i, lambda j, cr: step(j, cr, False), init)
        carry = step(i, carry, True)
        (m_a, l_a, acc_a), (m_b, l_b, acc_b) = carry
        o_ref[...] = jnp.where(lane < 64, acc_a / l_a, acc_b / l_b).astype(BF16)
        lse_ref[0] = jnp.broadcast_to(m_a + jnp.log(l_a), (t, 128))
        lse_ref[1] = jnp.broadcast_to(m_b + jnp.log(l_b), (t, 128))

    return _call(body, name=name, grid=(4, nq),
                 in_specs=[pl.BlockSpec((t, 128), lambda p, i: (i, 3 * p)),
                           pl.BlockSpec((s, 128), lambda p, i: (0, 3 * p + 1)),
                           pl.BlockSpec((s, 128), lambda p, i: (0, 3 * p + 2)),
                           pl.BlockSpec((None, 2, s), lambda p, i: (p, 0, 0)), ANY],
                 out_specs=[pl.BlockSpec((t, 128), lambda p, i: (i, 4 + p)),
                            pl.BlockSpec((2, t, 128), lambda p, i: (p, i, 0))],
                 out_shape=[_sds((s, 2 * FOX_WIDTH), BF16), _sds((FOX_HEADS, s, 128), F32)],
                 args=[qkv, qkv, qkv, cum4, cat], sem=("parallel", "arbitrary"), comms=comms, aliases={4: 0})


def fox_bwd(qkv, cum4, cat, datt, lse, dz, name, comms=()):
    s = qkv.shape[0]
    t = min(s, FOX_TILE)
    nk = s // t
    nt_dims, tn_dims = _DOT_DIMS["nt"], _DOT_DIMS["tn"]

    def body(q_ref, k_ref, v_ref, c_ref, o_ref, do_ref, lse_ref, dz_in, dz_ref, dc_ref, dr_ref, dq_acc, dk_acc, dv_acc, dr_acc):
        j = pl.program_id(1)
        k0 = pl.multiple_of(j * t, t)
        lane = lax.broadcasted_iota(jnp.int32, (t, 128), 1)
        lo = lane < 64
        kj, vj = k_ref[...], v_ref[...]
        zero = jnp.zeros_like(kj)
        k_h = (jnp.where(lo, kj, zero), jnp.where(lo, zero, kj))

        @pl.when(j == 0)
        def _():
            dq_acc[...] = jnp.zeros_like(dq_acc)
            dr_acc[...] = jnp.zeros_like(dr_acc)

        dk_acc[...] = jnp.zeros_like(dk_acc)
        dv_acc[...] = jnp.zeros_like(dv_acc)

        def step(i, dc, masked):
            q0 = pl.multiple_of(i * t, t)
            qi = q_ref[pl.ds(q0, t), :]
            doi = do_ref[pl.ds(q0, t), :]
            prod = doi.astype(F32) * o_ref[pl.ds(q0, t), :].astype(F32)
            zq = jnp.zeros_like(qi)
            dq_new = jnp.zeros((t, 128), F32)
            dc_out = []
            for hh in range(2):
                sel = lo if hh == 0 else jnp.logical_not(lo)
                q_m = jnp.where(sel, qi, zq)
                do_m = jnp.where(sel, doi, zq)
                delta = jnp.sum(jnp.where(sel, prod, 0.0), axis=-1, keepdims=True)
                bias = _fox_bias(c_ref, hh, q0, k0, t)
                sc = lax.dot_general(q_m, kj, nt_dims, preferred_element_type=F32) * FOX_SCALE + bias
                if masked:
                    sc = jnp.where(_causal_mask(t, t, q0, k0), sc, NEG_INF)
                lse_t = jnp.tile(lse_ref[hh, pl.ds(q0, t), :], (1, t // 128))
                p = jnp.exp(sc - lse_t)
                dv_acc[hh] += lax.dot_general(p.astype(BF16), doi, tn_dims, preferred_element_type=F32)
                dp = lax.dot_general(do_m, vj, nt_dims, preferred_element_type=F32)
                ds = p * (dp - delta)
                dc_out.append(dc[hh] - jnp.sum(ds, axis=0, keepdims=True))
                dr_acc[hh, pl.ds(q0, t), :] += jnp.sum(ds, axis=-1, keepdims=True)
                ds_b = ds.astype(BF16)
                dq_new = dq_new + jnp.dot(ds_b, k_h[hh], preferred_element_type=F32)
                dk_acc[hh] += lax.dot_general(ds_b, qi, tn_dims, preferred_element_type=F32)
            dq_acc[pl.ds(q0, t), :] += dq_new
            return tuple(dc_out)

        dc = step(j, (jnp.zeros((1, t), F32), jnp.zeros((1, t), F32)), True)
        dc = lax.fori_loop(j + 1, nk, lambda i, cr: step(i, cr, False), dc)
        dz_ref[pl.ds(k0, t), 128:256] = (jnp.where(lo, dk_acc[0], dk_acc[1]) * FOX_SCALE).astype(BF16)
        dz_ref[pl.ds(k0, t), 256:384] = jnp.where(lo, dv_acc[0], dv_acc[1]).astype(BF16)
        dc_ref[0:1, :] = dc[0]
        dc_ref[1:2, :] = dc[1]

        @pl.when(j == nk - 1)
        def _():
            dz_ref[:, 0:128] = (dq_acc[...] * FOX_SCALE).astype(BF16)
            eye = lax.broadcasted_iota(jnp.int32, (t, t), 0) == lax.broadcasted_iota(jnp.int32, (t, t), 1)
            for hh in range(2):
                for b in range(nk):
                    col = dr_acc[hh, b * t:(b + 1) * t, :]
                    dr_ref[hh:hh + 1, b * t:(b + 1) * t] = jnp.sum(jnp.where(eye, col, 0.0), axis=0, keepdims=True)

    return _call(body, name=name, grid=(4, nk),
                 in_specs=[pl.BlockSpec((s, 128), lambda p, j: (0, 3 * p)),
                           pl.BlockSpec((t, 128), lambda p, j: (j, 3 * p + 1)),
                           pl.BlockSpec((t, 128), lambda p, j: (j, 3 * p + 2)),
                           pl.BlockSpec((None, 2, s), lambda p, j: (p, 0, 0)),
                           pl.BlockSpec((s, 128), lambda p, j: (0, 4 + p)),
                           pl.BlockSpec((s, 128), lambda p, j: (0, p)),
                           pl.BlockSpec((2, s, 128), lambda p, j: (p, 0, 0)), ANY],
                 out_specs=[pl.BlockSpec((s, 384), lambda p, j: (0, 3 + p)),
                            pl.BlockSpec((None, 2, t), lambda p, j: (p, 0, j)),
                            pl.BlockSpec((None, 2, s), lambda p, j: (p, 0, 0))],
                 out_shape=[_sds((s, IN_PAD), BF16), _sds((4, 2, s), F32), _sds((4, 2, s), F32)],
                 args=[qkv, qkv, qkv, cum4, cat, datt, lse, dz],
                 scratch=[pltpu.VMEM((s, 128), F32), pltpu.VMEM((2, t, 128), F32), pltpu.VMEM((2, t, 128), F32),
                          pltpu.VMEM((2, s, 1), F32)],
                 sem=("parallel", "arbitrary"), comms=comms, aliases={7: 0})


def _mem_probs(q_h, k_h):
    sc = lax.dot_general(q_h, k_h, _DOT_DIMS["nt"], preferred_element_type=F32) * MEM_SCALE
    e = jnp.exp(sc - jnp.max(sc, axis=-1, keepdims=True))
    return e / jnp.sum(e, axis=-1, keepdims=True)


def mem_attn_fwd(qm, km, vm, name):
    s = qm.shape[0]
    tm = min(s, ROW_TILE)

    def body(q_ref, k_ref, v_ref, o_ref):
        for h in range(MEM_HEADS):
            sl = slice(h * MEM_HEAD_DIM, (h + 1) * MEM_HEAD_DIM)
            p = _mem_probs(q_ref[:, sl], k_ref[:, sl])
            o_ref[:, sl] = jnp.dot(p.astype(BF16), v_ref[:, sl], preferred_element_type=F32).astype(BF16)

    kv = pl.BlockSpec((N_MEM, MEM_INNER), lambda i: (0, 0))
    row = pl.BlockSpec((tm, MEM_INNER), lambda i: (i, 0))
    return _call(body, name=name, grid=(s // tm,), in_specs=[row, kv, kv], out_specs=[row],
                 out_shape=[_sds((s, MEM_INNER), BF16)], args=[qm, km, vm], sem=("parallel",))[0]


def mem_attn_bwd(qm, km, vm, dom, name):
    s = qm.shape[0]
    tm = min(s, ROW_TILE)
    tn_dims = _DOT_DIMS["tn"]

    def body(q_ref, k_ref, v_ref, do_ref, dq_ref, dk_ref, dv_ref):
        i = pl.program_id(0)

        @pl.when(i == 0)
        def _():
            dk_ref[...] = jnp.zeros_like(dk_ref)
            dv_ref[...] = jnp.zeros_like(dv_ref)

        for h in range(MEM_HEADS):
            sl = slice(h * MEM_HEAD_DIM, (h + 1) * MEM_HEAD_DIM)
            q_h, k_h, do_h = q_ref[:, sl], k_ref[:, sl], do_ref[:, sl]
            p = _mem_probs(q_h, k_h)
            dp = lax.dot_general(do_h, v_ref[:, sl], _DOT_DIMS["nt"], preferred_element_type=F32)
            ds = p * (dp - jnp.sum(p * dp, axis=-1, keepdims=True))
            ds_b = (ds * MEM_SCALE).astype(BF16)
            dq_ref[:, sl] = jnp.dot(ds_b, k_h, preferred_element_type=F32).astype(BF16)
            dk_ref[:, sl] += lax.dot_general(ds_b, q_h, tn_dims, preferred_element_type=F32)
            dv_ref[:, sl] += lax.dot_general(p.astype(BF16), do_h, tn_dims, preferred_element_type=F32)

    kv = pl.BlockSpec((N_MEM, MEM_INNER), lambda i: (0, 0))
    row = pl.BlockSpec((tm, MEM_INNER), lambda i: (i, 0))
    return _call(body, name=name, grid=(s // tm,), in_specs=[row, kv, kv, row], out_specs=[row, kv, kv],
                 out_shape=[_sds((s, MEM_INNER), BF16), _sds((N_MEM, MEM_INNER), F32), _sds((N_MEM, MEM_INNER), F32)],
                 args=[qm, km, vm, dom], sem=("arbitrary",))


def _adam_update(w, g, m, v):
    c1 = 1.0 - ADAM_B1 ** ADAM_STEP
    c2 = 1.0 - ADAM_B2 ** ADAM_STEP
    nm = ADAM_B1 * m + (1.0 - ADAM_B1) * g
    nv = ADAM_B2 * v + (1.0 - ADAM_B2) * (g * g)
    return -ADAM_LR * ((nm / c1) / (jnp.sqrt(nv / c2) + ADAM_EPS) + ADAM_WD * w), nm, nv


def adamw(w, g, m, v, name):
    r, c = w.shape
    tm = ROW_TILE if r % ROW_TILE == 0 else r

    def body(w_ref, g_ref, m_ref, v_ref, d_ref, nm_ref, nv_ref):
        d_ref[...], nm_ref[...], nv_ref[...] = _adam_update(w_ref[...], g_ref[...], m_ref[...], v_ref[...])

    blk = pl.BlockSpec((tm, c), lambda i: (i, 0))
    return _call(body, name=name, grid=(r // tm,), in_specs=[blk] * 4, out_specs=[blk] * 3,
                 out_shape=[_sds((r, c), F32)] * 3, args=[w, g, m, v], sem=("parallel",))


def adamw_layers(w, m, v, g0, g1, name):
    _, r, c = w.shape
    tm = ROW_TILE if r % ROW_TILE == 0 else r

    def body(w_ref, m_ref, v_ref, g0_ref, g1_ref, d_ref, nm_ref, nv_ref, g_ref):
        g = jnp.where(pl.program_id(0) == 0, g0_ref[...], g1_ref[...])
        g_ref[...] = g
        d_ref[...], nm_ref[...], nv_ref[...] = _adam_update(w_ref[...], g, m_ref[...], v_ref[...])

    blk = pl.BlockSpec((None, tm, c), lambda l, i: (l, i, 0))
    return _call(body, name=name, grid=(2, r // tm),
                 in_specs=[blk, blk, blk, pl.BlockSpec((tm, c), lambda l, i: (i * (1 - l), 0)),
                           pl.BlockSpec((tm, c), lambda l, i: (i * l, 0))],
                 out_specs=[blk] * 4, out_shape=[_sds(w.shape, F32)] * 4, args=[w, m, v, g0, g1], sem=("parallel", "parallel"))


def _vec(v):
    return v.reshape(1, -1)


def _hosted(hosts, name):
    make = hosts.get(name)
    return make() if make is not None else ()


def layer_fwd(x0, mem, w, sm, l, hosts):
    s = x0.shape[0]
    tm = min(s, ROW_TILE)
    nm = lambda base: f"{base}{l}"
    h1 = rms_fwd(x0, _vec(sm["norm_mix_pre"]), nm("rms_mix_pre"))
    row = lambda width: pl.BlockSpec((tm, width), lambda i, j, k: (i, 0))
    ag, flp, qkv = matmul(
        h1, w("w_in"), mode="nn", dims=(s, IN_PAD, D_MODEL), tiles=(tm, IN_PAD, D_MODEL), name=nm("mix_in"),
        out_shapes=[_sds((s, GATE_COL), F32), _sds((s, 128), F32), _sds((s, 3 * FOX_WIDTH), BF16)],
        out_specs=[row(GATE_COL), row(128), row(3 * FOX_WIDTH)],
        epi=lambda acc: (acc[:, :GATE_COL], acc[:, GATE_COL:QKV_COL], acc[:, QKV_COL:]))
    u1, cat = conv_fwd(ag, w("conv_w"), _vec(sm["conv_b"]), _vec(sm["conv_ln_g"]), _vec(sm["conv_ln_b"]), nm("conv_fwd"),
                       comms=_hosted(hosts, nm("conv_fwd")))
    fl_t = flp[:, :FOX_HEADS].T
    cum = fox_gate_fwd(fl_t, sm["b_forget"].reshape(FOX_HEADS, 1), nm("fox_gate_fwd"))
    cum4 = cum.reshape(4, 2, s)
    cat, lse = fox_fwd(qkv, cum4, cat, nm("fox_fwd"), comms=_hosted(hosts, nm("fox_fwd")))
    y1, = matmul(cat, w("w_out"), mode="nn", dims=(s, D_MODEL, D_MODEL), tiles=(tm, D_MODEL, D_MODEL), name=nm("mix_out"),
                 out_shapes=[_sds((s, D_MODEL), F32)], comms=_hosted(hosts, nm("mix_out")))
    x1 = resid_post(x0, y1, _vec(sm["norm_mix_post"]), nm("post_mix"))

    h2 = rms_fwd(x1, _vec(sm["norm_mem_pre"]), nm("rms_mem_pre"))
    mn = rms_fwd(mem, _vec(sm["norm_memkv"]), nm("rms_memkv"))
    qm, = matmul(h2, w("w_mq"), mode="nn", dims=(s, MEM_INNER, D_MODEL), tiles=(tm, MEM_INNER, D_MODEL), name=nm("mem_q"),
                 out_shapes=[_sds((s, MEM_INNER), BF16)])
    km, = matmul(mn, w("w_mk"), mode="nn", dims=(N_MEM, MEM_INNER, D_MODEL), tiles=(N_MEM, MEM_INNER, D_MODEL), name=nm("mem_k"),
                 out_shapes=[_sds((N_MEM, MEM_INNER), BF16)])
    vm, = matmul(mn, w("w_mv"), mode="nn", dims=(N_MEM, MEM_INNER, D_MODEL), tiles=(N_MEM, MEM_INNER, D_MODEL), name=nm("mem_v"),
                 out_shapes=[_sds((N_MEM, MEM_INNER), BF16)])
    om = mem_attn_fwd(qm, km, vm, nm("mem_attn_fwd"))
    y2, = matmul(om, w("w_mo"), mode="nn", dims=(s, D_MODEL, MEM_INNER), tiles=(tm, D_MODEL, MEM_INNER), name=nm("mem_o"),
                 out_shapes=[_sds((s, D_MODEL), F32)])
    x2 = resid_post(x1, y2, _vec(sm["norm_mem_post"]), nm("post_mem"))

    h3 = rms_fwd(x2, _vec(sm["norm_mlp_pre"]), nm("rms_mlp_pre"))

    def relu2(acc):
        r = jnp.maximum(acc, 0.0)
        return (r * r,)

    act, = matmul(h3, w("w_up"), mode="nn", dims=(s, D_FF, D_MODEL), tiles=(tm, 1024, D_MODEL), name=nm("mlp_up"),
                  out_shapes=[_sds((s, D_FF), BF16)], epi=relu2,
                  b_spec=pl.BlockSpec((None, D_MODEL, 1024), lambda i, j, k: (j, 0, 0)), comms=_hosted(hosts, nm("mlp_up")))
    y3, = matmul(act, w("w_down"), mode="nn", dims=(s, D_MODEL, D_FF), tiles=(tm, D_MODEL, 1024), name=nm("mlp_down"),
                 out_shapes=[_sds((s, D_MODEL), F32)], comms=_hosted(hosts, nm("mlp_down")))
    x3 = resid_post(x2, y3, _vec(sm["norm_mlp_post"]), nm("post_mlp"))
    saved = dict(x0=x0, h1=h1, ag=ag, qkv=qkv, fl_t=fl_t, u1=u1, cum4=cum4, lse=lse, cat=cat, y1=y1, x1=x1,
                 h2=h2, mn=mn, qm=qm, km=km, vm=vm, om=om, y2=y2, x2=x2, h3=h3, act=act, y3=y3)
    return x3, saved


def layer_bwd(dx3, mem, w, sm, sv, l, hosts, gw):
    s = dx3.shape[0]
    tm = min(s, ROW_TILE)
    nm = lambda base: f"{base}{l}"
    gs = {}
    shards = lambda g: g.reshape(N_CHIPS, g.shape[0] // N_CHIPS, g.shape[1])
    col_shards = lambda g: jnp.moveaxis(g.reshape(g.shape[0], N_CHIPS, g.shape[1] // N_CHIPS), 1, 0)
    dy3, gs["norm_mlp_post"] = rms_bwd(sv["y3"], _vec(sm["norm_mlp_post"]), dx3, nm("bwd_post_mlp"), out_dtype=BF16)
    g, = matmul(sv["act"], dy3, mode="tn", dims=(D_FF, D_MODEL, s), tiles=(1024, D_MODEL, tm), name=nm("dw_down"),
                out_shapes=[_sds((D_FF, D_MODEL), F32)], comms=_hosted(hosts, nm("dw_down")))
    gw[("w_down", l)] = shards(g)

    def dup_epi(acc, act_ref):
        return (acc * (2.0 * jnp.sqrt(act_ref[...].astype(F32))),)

    dup, = matmul(dy3, w("w_down"), mode="nt", dims=(s, D_FF, D_MODEL), tiles=(tm, 1024, D_MODEL), name=nm("d_act"),
                  out_shapes=[_sds((s, D_FF), BF16)], epi=dup_epi, extras=(sv["act"],),
                  extra_specs=(pl.BlockSpec((tm, 1024), lambda i, j, k: (i, j)),), comms=_hosted(hosts, nm("d_act")))
    gw[("w_up", l)], = matmul(sv["h3"], dup, mode="tn", dims=(D_MODEL, D_FF, s), tiles=(512, 1024, tm), name=nm("dw_up"),
                              out_shapes=[_sds((N_CHIPS, D_MODEL, 1024), F32)],
                              out_specs=[pl.BlockSpec((None, 512, 1024), lambda i, j, k: (j, i, 0))])
    dh3, = matmul(dup, w("w_up"), mode="nt", dims=(s, D_MODEL, D_FF), tiles=(tm, D_MODEL, 1024), name=nm("d_h3"),
                  out_shapes=[_sds((s, D_MODEL), F32)],
                  b_spec=pl.BlockSpec((None, D_MODEL, 1024), lambda i, j, k: (k, 0, 0)), comms=_hosted(hosts, nm("d_h3")))
    dx2, gs["norm_mlp_pre"] = rms_bwd(sv["x2"], _vec(sm["norm_mlp_pre"]), dh3, nm("bwd_pre_mlp"), add=dx3)

    dy2, gs["norm_mem_post"] = rms_bwd(sv["y2"], _vec(sm["norm_mem_post"]), dx2, nm("bwd_post_mem"), out_dtype=BF16)
    g, = matmul(sv["om"], dy2, mode="tn", dims=(MEM_INNER, D_MODEL, s), tiles=(MEM_INNER, D_MODEL, tm), name=nm("dw_mo"),
                out_shapes=[_sds((MEM_INNER, D_MODEL), F32)])
    gw[("w_mo", l)] = col_shards(g)
    dom, = matmul(dy2, w("w_mo"), mode="nt", dims=(s, MEM_INNER, D_MODEL), tiles=(tm, MEM_INNER, D_MODEL), name=nm("d_om"),
                  out_shapes=[_sds((s, MEM_INNER), BF16)])
    dqm, dkm, dvm = mem_attn_bwd(sv["qm"], sv["km"], sv["vm"], dom, nm("mem_attn_bwd"))
    g, = matmul(sv["h2"], dqm, mode="tn", dims=(D_MODEL, MEM_INNER, s), tiles=(D_MODEL, MEM_INNER, tm), name=nm("dw_mq"),
                out_shapes=[_sds((D_MODEL, MEM_INNER), F32)])
    gw[("w_mq", l)] = shards(g)
    dkm_b, dvm_b = dkm.astype(BF16), dvm.astype(BF16)
    g, = matmul(sv["mn"], dkm_b, mode="tn", dims=(D_MODEL, MEM_INNER, N_MEM), tiles=(D_MODEL, MEM_INNER, N_MEM),
                name=nm("dw_mk"), out_shapes=[_sds((D_MODEL, MEM_INNER), F32)])
    gw[("w_mk", l)] = shards(g)
    g, = matmul(sv["mn"], dvm_b, mode="tn", dims=(D_MODEL, MEM_INNER, N_MEM), tiles=(D_MODEL, MEM_INNER, N_MEM),
                name=nm("dw_mv"), out_shapes=[_sds((D_MODEL, MEM_INNER), F32)])
    gw[("w_mv", l)] = shards(g)
    dmn_k, = matmul(dkm_b, w("w_mk"), mode="nt", dims=(N_MEM, D_MODEL, MEM_INNER), tiles=(N_MEM, D_MODEL, MEM_INNER),
                    name=nm("d_mn_k"), out_shapes=[_sds((N_MEM, D_MODEL), F32)])
    dmn, = matmul(dvm_b, w("w_mv"), mode="nt", dims=(N_MEM, D_MODEL, MEM_INNER), tiles=(N_MEM, D_MODEL, MEM_INNER),
                  name=nm("d_mn_v"), out_shapes=[_sds((N_MEM, D_MODEL), F32)],
                  epi=lambda acc, other: (acc + other[...],), extras=(dmn_k,),
                  extra_specs=(pl.BlockSpec((N_MEM, D_MODEL), lambda i, j, k: (0, 0)),))
    _, gs["norm_memkv"] = rms_bwd(mem, _vec(sm["norm_memkv"]), dmn, nm("bwd_memkv"))
    dh2, = matmul(dqm, w("w_mq"), mode="nt", dims=(s, D_MODEL, MEM_INNER), tiles=(tm, D_MODEL, MEM_INNER), name=nm("d_h2"),
                  out_shapes=[_sds((s, D_MODEL), F32)])
    dx1, gs["norm_mem_pre"] = rms_bwd(sv["x1"], _vec(sm["norm_mem_pre"]), dh2, nm("bwd_pre_mem"), add=dx2)

    dy1, gs["norm_mix_post"] = rms_bwd(sv["y1"], _vec(sm["norm_mix_post"]), dx1, nm("bwd_post_mix"), out_dtype=BF16)
    g, = matmul(sv["cat"], dy1, mode="tn", dims=(D_MODEL, D_MODEL, s), tiles=(D_MODEL, D_MODEL, tm), name=nm("dw_out"),
                out_shapes=[_sds((D_MODEL, D_MODEL), F32)])
    gw[("w_out", l)] = shards(g)
    du, datt = matmul(dy1, w("w_out"), mode="nt", dims=(s, D_MODEL, D_MODEL), tiles=(tm, D_MODEL, D_MODEL), name=nm("d_cat"),
                      out_shapes=[_sds((s, CONV_CH), F32), _sds((s, FOX_WIDTH), BF16)],
                      out_specs=[pl.BlockSpec((tm, CONV_CH), lambda i, j, k: (i, 0)), pl.BlockSpec((tm, FOX_WIDTH), lambda i, j, k: (i, 0))],
                      epi=lambda acc: (acc[:, :CONV_CH], acc[:, CONV_CH:]))
    dz, csm = conv_bwd(du, sv["u1"], sv["ag"], w("conv_w"), _vec(sm["conv_ln_g"]), _vec(sm["conv_ln_b"]), nm("conv_bwd"),
                       comms=_hosted(hosts, nm("conv_bwd")))
    gs["conv_b"], gs["conv_ln_g"], gs["conv_ln_b"] = csm[32:33], csm[33:34], csm[34:35]
    dz, dcol4, drow4 = fox_bwd(sv["qkv"], sv["cum4"], sv["cat"], datt, sv["lse"], dz, nm("fox_bwd"),
                               comms=_hosted(hosts, nm("fox_bwd")))
    dfl_t, dbf = fox_gate_bwd(dcol4.reshape(FOX_HEADS, s), drow4.reshape(FOX_HEADS, s), sv["fl_t"],
                              sm["b_forget"].reshape(FOX_HEADS, 1), nm("fox_gate_bwd"))
    gs["b_forget"] = dbf.reshape(1, FOX_HEADS)
    dflp = jnp.pad(dfl_t.T.astype(BF16), ((0, 0), (0, 128 - FOX_HEADS)))
    dz = lax.dynamic_update_slice(dz, dflp, (0, GATE_COL))
    g, = matmul(sv["h1"], dz, mode="tn", dims=(D_MODEL, IN_PAD, s), tiles=(512, IN_PAD, tm), name=nm("dw_in"),
                out_shapes=[_sds((D_MODEL, IN_PAD), F32)])
    gw[("w_in", l)] = col_shards(_in_cols_back(g))
    dh1, = matmul(dz, w("w_in"), mode="nt", dims=(s, D_MODEL, IN_PAD), tiles=(tm, D_MODEL, IN_PAD), name=nm("d_h1"),
                  out_shapes=[_sds((s, D_MODEL), F32)], comms=_hosted(hosts, nm("d_h1")))
    dx0, gs["norm_mix_pre"] = rms_bwd(sv["x0"], _vec(sm["norm_mix_pre"]), dh1, nm("bwd_pre_mix"), add=dx1,
                                      comms=_hosted(hosts, nm("bwd_pre_mix")))
    return dx0, gs, csm[:CONV_WIDTH]


def _in_cols(w):
    c2, fw = 2 * CONV_CH, FOX_WIDTH
    parts = [w[:, :c2], w[:, c2 + 3 * fw:], jnp.zeros((w.shape[0], 128 - FOX_HEADS), w.dtype)]
    for p in range(4):
        parts += [w[:, c2 + part * fw + 128 * p:c2 + part * fw + 128 * (p + 1)] for part in range(3)]
    return jnp.concatenate(parts, axis=1)


def _in_cols_back(g):
    c2 = 2 * CONV_CH
    qkv = [[g[:, QKV_COL + 384 * p + 128 * part:QKV_COL + 384 * p + 128 * (part + 1)] for p in range(4)] for part in range(3)]
    return jnp.concatenate([g[:, :c2]] + [blk for part in qkv for blk in part] + [g[:, c2:c2 + FOX_HEADS]], axis=1)


def pack_small(get, conv_w):
    rows = []
    for l in range(2):
        for nm in NORMS:
            rows.append(get(nm, l).reshape(-1))
        rows.append(jnp.concatenate([get("conv_b", l).reshape(-1), get("conv_ln_g", l).reshape(-1)]))
        rows.append(jnp.concatenate([get("conv_ln_b", l).reshape(-1), get("b_forget", l).reshape(-1),
                                     jnp.zeros((D_MODEL - CONV_CH - FOX_HEADS,), F32)]))
    head = jnp.stack(rows + [jnp.zeros((D_MODEL,), F32)] * (24 - len(rows)))
    taps = jnp.concatenate([conv_w(0), conv_w(1)], axis=1)
    return jnp.concatenate([head, taps, jnp.zeros((1, D_MODEL), F32)], axis=0)


def unpack_small(packed):
    out = {}
    for idx, nm in enumerate(NORMS):
        out[nm] = jnp.stack([packed[9 * l + idx] for l in range(2)])
    out["conv_b"] = jnp.stack([packed[9 * l + 7, :CONV_CH] for l in range(2)])
    out["conv_ln_g"] = jnp.stack([packed[9 * l + 7, CONV_CH:] for l in range(2)])
    out["conv_ln_b"] = jnp.stack([packed[9 * l + 8, :CONV_CH] for l in range(2)])
    out["b_forget"] = jnp.stack([packed[9 * l + 8, CONV_CH:CONV_CH + FOX_HEADS] for l in range(2)])
    out["conv_w"] = jnp.stack([packed[24:24 + CONV_WIDTH, CONV_CH * l:CONV_CH * (l + 1)] for l in range(2)])
    return out


def kernel(x, mem, norm_mix_pre, norm_mix_post, w_in, b_forget, conv_w, conv_b, conv_ln_g, conv_ln_b, w_out, norm_mem_pre, norm_mem_post, norm_memkv, w_mq, w_mk, w_mv, w_mo, norm_mlp_pre, norm_mlp_post, w_up, w_down, loss_target, m_norm_mix_pre, m_norm_mix_post, m_w_in, m_b_forget, m_conv_w, m_conv_b, m_conv_ln_g, m_conv_ln_b, m_w_out, m_norm_mem_pre, m_norm_mem_post, m_norm_memkv, m_w_mq, m_w_mk, m_w_mv, m_w_mo, m_norm_mlp_pre, m_norm_mlp_post, m_w_up, m_w_down, v_norm_mix_pre, v_norm_mix_post, v_w_in, v_b_forget, v_conv_w, v_conv_b, v_conv_ln_g, v_conv_ln_b, v_w_out, v_norm_mem_pre, v_norm_mem_post, v_norm_memkv, v_w_mq, v_w_mk, v_w_mv, v_w_mo, v_norm_mlp_pre, v_norm_mlp_post, v_w_up, v_w_down):
    args = dict(locals())
    wts = {n: args[n] for n in WEIGHTS}
    mom = {n: args["m_" + n] for n in WEIGHTS}
    var = {n: args["v_" + n] for n in WEIGHTS}
    own = 2 * lax.axis_index("x") + lax.axis_index("y")
    place = jnp.stack([lax.axis_index("c"), own]).astype(jnp.int32)
    items = lambda group, l: [(n, l) for n in group]

    shard = {(n, l): wts[n][l].astype(BF16) for n in BIG for l in range(2)}
    shard[("conv_w", 0)] = wts["conv_w"]
    gath, full = {}, {}

    def gather_stage(keys):
        def done(res):
            for key, buf in zip(keys, res):
                gath[key] = lax.dynamic_update_slice(buf, shard[key][None], (own,) + (0,) * shard[key].ndim)
        return gather_ici([shard[k] for k in keys], done)

    def forward_stage(keys):
        def done(res):
            gath.update(zip(keys, res))
        return gather_forward([gath[k] for k in keys], done)

    def weight(l):
        def get(n):
            if (n, l) not in full:
                if n == "conv_w":
                    g = gath[("conv_w", 0)][:, l]
                    full[(n, l)] = jnp.pad(jnp.moveaxis(g, 0, 1).reshape(CONV_WIDTH, CONV_CH), ((0, CONV_HALO - CONV_WIDTH), (0, 0)))
                elif n == "w_in":
                    g = gath[(n, l)]
                    full[(n, l)] = _in_cols(jnp.moveaxis(g, 0, 1).reshape(D_MODEL, IN_COLS))
                elif n == "w_mo":
                    g = gath[(n, l)]
                    full[(n, l)] = jnp.moveaxis(g, 0, 1).reshape(MEM_INNER, D_MODEL)
                elif n == "w_up":
                    full[(n, l)] = gath[(n, l)]
                else:
                    g = gath[(n, l)]
                    full[(n, l)] = g.reshape(N_CHIPS * g.shape[1], g.shape[2])
            return full[(n, l)]
        return get

    first = items(GROUP_A, 0) + [("conv_w", 0)]
    run_comm([gather_stage(first)], "gather_first")
    run_comm([forward_stage(first)], "forward_first")
    rest0 = items(GROUP_B, 0) + items(GROUP_C, 0) + items(GROUP_A, 1)
    mid1 = items(GROUP_B, 1) + [("w_up", 1)]
    fwd_hosts = [
        {"fox_fwd0": lambda: [gather_stage(rest0)],
         "mix_out0": lambda: [forward_stage(rest0)],
         "mlp_up0": lambda: [gather_stage(mid1)],
         "mlp_down0": lambda: [gather_stage([("w_down", 1)]), forward_stage(mid1)]},
        {"conv_fwd1": lambda: [forward_stage([("w_down", 1)])]},
    ]

    gw, swapped, own_sum, wire, recvd, fin = {}, {}, {}, {}, {}, {}

    def swap_stage(keys):
        return swap_halves([gw[k] for k in keys], lambda res: swapped.update(zip(keys, res)))

    def scatter_stage(keys):
        for n, l in keys:
            own_sum[(n, l)], wire[(n, l)] = pair_sum(gw[(n, l)], swapped[(n, l)], place, f"pair_sum_{n}{l}")
        return scatter_partials([wire[k] for k in keys], lambda res: recvd.update(zip(keys, res)))

    def share_stage(keys):
        for n, l in keys:
            fin[(n, l)] = chip_sum(own_sum[(n, l)], recvd[(n, l)], place, f"chip_sum_{n}{l}")
        return share_halves([fin[k] for k in keys], lambda res: fin.update(zip(keys, res)))

    a0, b0, c0 = items(GROUP_A, 0), items(GROUP_B, 0), items(GROUP_C, 0)
    a1, b1, c1 = items(GROUP_A, 1), items(GROUP_B, 1), items(GROUP_C, 1)
    bwd_hosts = [
        {"dw_down0": lambda: [scatter_stage(a1)],
         "d_act0": lambda: [share_stage(a1)],
         "d_h30": lambda: [swap_stage(c0)],
         "conv_bwd0": lambda: [scatter_stage(c0), swap_stage(b0)],
         "fox_bwd0": lambda: [share_stage(c0), scatter_stage(b0)],
         "d_h10": lambda: [share_stage(b0), swap_stage(a0)],
         "bwd_pre_mix0": lambda: [scatter_stage(a0)]},
        {"d_h31": lambda: [swap_stage(c1)],
         "conv_bwd1": lambda: [scatter_stage(c1), swap_stage(b1)],
         "fox_bwd1": lambda: [share_stage(c1), scatter_stage(b1)],
         "d_h11": lambda: [share_stage(b1), swap_stage(a1)]},
    ]

    sml = [{n: wts[n][l] for n in SMALL} for l in range(2)]
    mem0 = mem[0]
    saved, h = [], x[0]
    for l in range(2):
        h, sv = layer_fwd(h, mem0, weight(l), sml[l], l, fwd_hosts[l])
        saved.append(sv)
    sq, dx = loss_head(h, loss_target[0], "loss_head")
    loss = lax.psum(0.5 * jnp.sum(sq) / D_MODEL, ("x", "y", "c"))
    gss, gconv = [None, None], [None, None]
    for l in (1, 0):
        dx, gss[l], gconv[l] = layer_bwd(dx, mem0, weight(l), sml[l], saved[l], l, bwd_hosts[l], gw)
    run_comm([share_stage(a0)], "share_last")

    g_small = unpack_small(allreduce_small(pack_small(lambda nm, l: gss[l][nm], lambda l: gconv[l])))

    grads, delta, new_m, new_v = {}, {}, {}, {}
    for n in BIG:
        delta[n], new_m[n], new_v[n], grads[n] = adamw_layers(wts[n], mom[n], var[n], fin[(n, 0)], fin[(n, 1)], "adamw_" + n)
    taps = lambda a: a.reshape(2 * CONV_WIDTH, CONV_CH // N_CHIPS)
    grads["conv_w"] = lax.dynamic_slice_in_dim(g_small["conv_w"], own * (CONV_CH // N_CHIPS), CONV_CH // N_CHIPS, axis=2)
    d_, m_, v_ = adamw(taps(wts["conv_w"]), taps(grads["conv_w"]), taps(mom["conv_w"]), taps(var["conv_w"]), "adamw_conv_w")
    delta["conv_w"], new_m["conv_w"], new_v["conv_w"] = (a.reshape(wts["conv_w"].shape) for a in (d_, m_, v_))
    zero_taps = lambda l: jnp.zeros((CONV_WIDTH, CONV_CH), F32)
    pk = lambda src: pack_small(lambda nm, l: src[nm][l], zero_taps)
    for n in SMALL:
        grads[n] = g_small[n]
    d_, m_, v_ = adamw(pk(wts), pk(grads), pk(mom), pk(var), "adamw_small")
    for res, packed in ((delta, d_), (new_m, m_), (new_v, v_)):
        small = unpack_small(packed)
        res.update({n: small[n] for n in SMALL})

    return (loss, dx[None], *[grads[n] for n in WEIGHTS], *[delta[n] for n in WEIGHTS],
            *[new_m[n] for n in WEIGHTS], *[new_v[n] for n in WEIGHTS])
```

```python
import jax
import jax.numpy as jnp
from jax import lax
from jax.experimental import pallas as pl
from jax.experimental.pallas import tpu as pltpu

F32, BF16 = jnp.float32, jnp.bfloat16
D_MODEL = 1024
CONV_CH = 512
CONV_WIDTH = 31
CONV_HALO = 32
FOX_WIDTH = 512
FOX_HEADS = 8
N_MEM = 256
MEM_HEADS = 4
MEM_HEAD_DIM = 128
MEM_INNER = 512
D_FF = 4096
IN_COLS = 2568
IN_PAD = 2688
GATE_COL = 2 * CONV_CH
QKV_COL = GATE_COL + 128
EPS = 1e-6
NEG_INF = -1e30
FOX_SCALE = 0.125
MEM_SCALE = MEM_HEAD_DIM ** -0.5
ADAM_LR, ADAM_B1, ADAM_B2, ADAM_EPS, ADAM_WD, ADAM_STEP = 0.001, 0.9, 0.999, 1e-08, 0.01, 10
VMEM_LIMIT = 56 * 1024 * 1024
ROW_TILE = 512
MESH = pl.DeviceIdType.MESH
N_CHIPS = 4

GROUP_A = ("w_in", "w_out")
GROUP_B = ("w_mq", "w_mk", "w_mv", "w_mo")
GROUP_C = ("w_up", "w_down")
BIG = GROUP_A + GROUP_B + GROUP_C
NORMS = ("norm_mix_pre", "norm_mix_post", "norm_mem_pre", "norm_mem_post", "norm_memkv", "norm_mlp_pre", "norm_mlp_post")
SMALL = NORMS + ("conv_b", "conv_ln_g", "conv_ln_b", "b_forget")
WEIGHTS = ("norm_mix_pre", "norm_mix_post", "w_in", "b_forget", "conv_w", "conv_b", "conv_ln_g", "conv_ln_b", "w_out",
           "norm_mem_pre", "norm_mem_post", "norm_memkv", "w_mq", "w_mk", "w_mv", "w_mo", "norm_mlp_pre", "norm_mlp_post",
           "w_up", "w_down")

ANY = pl.BlockSpec(memory_space=pl.ANY)


def _sds(shape, dtype):
    return jax.ShapeDtypeStruct(shape, dtype)


class Comm:
    def __init__(self, ins, out_shapes, n_sems, start, finish, aliases=(), on_done=None):
        self.ins, self.out_shapes, self.n_sems = list(ins), list(out_shapes), n_sems
        self.start, self.finish, self.aliases, self.on_done = start, finish, tuple(aliases), on_done


def _call(body, *, name, grid, in_specs, out_specs, out_shape, args, scratch=(), sem=(), comms=(), aliases=None):
    comms = list(comms)
    n_in, n_out, n_scr = len(in_specs), len(out_specs), len(scratch)
    io_alias = dict(aliases or {})
    c_args, c_shapes, c_scratch = [], [], []
    for cm in comms:
        for i, o in cm.aliases:
            io_alias[n_in + len(c_args) + i] = n_out + len(c_shapes) + o
        c_args += cm.ins
        c_shapes += cm.out_shapes
        c_scratch += [pltpu.SemaphoreType.DMA((cm.n_sems,)), pltpu.SemaphoreType.DMA((cm.n_sems,))]
    rank = len(grid)

    def hosted(*refs):
        ins, c_in = refs[:n_in], refs[n_in:n_in + len(c_args)]
        outs = refs[n_in + len(c_args):n_in + len(c_args) + n_out]
        c_out = refs[n_in + len(c_args) + n_out:n_in + len(c_args) + n_out + len(c_shapes)]
        scr = refs[len(refs) - n_scr - 2 * len(comms):len(refs) - 2 * len(comms)]
        sems = refs[len(refs) - 2 * len(comms):]
        first = last = None
        for d in range(rank):
            f, e = pl.program_id(d) == 0, pl.program_id(d) == grid[d] - 1
            first = f if first is None else jnp.logical_and(first, f)
            last = e if last is None else jnp.logical_and(last, e)

        def each(which):
            a = b = 0
            for k, cm in enumerate(comms):
                getattr(cm, which)(c_in[a:a + len(cm.ins)], c_out[b:b + len(cm.out_shapes)], sems[2 * k], sems[2 * k + 1])
                a, b = a + len(cm.ins), b + len(cm.out_shapes)

        if rank:
            pl.when(first)(lambda: each("start"))
        else:
            each("start")
        body(*ins, *outs, *scr)
        if rank:
            pl.when(last)(lambda: each("finish"))
        else:
            each("finish")

    if comms:
        sem = ("arbitrary",) * rank
    params = pltpu.CompilerParams(dimension_semantics=sem, vmem_limit_bytes=VMEM_LIMIT) if rank else \
        pltpu.CompilerParams(vmem_limit_bytes=VMEM_LIMIT)
    kw = dict(grid=grid) if rank else {}
    res = pl.pallas_call(
        hosted if comms else body, name=name, in_specs=list(in_specs) + [ANY] * len(c_args),
        out_specs=list(out_specs) + [ANY] * len(c_shapes), out_shape=list(out_shape) + c_shapes,
        scratch_shapes=list(scratch) + c_scratch, input_output_aliases=io_alias, compiler_params=params, **kw)(*args, *c_args)
    base, rest = list(res[:n_out]), list(res[n_out:])
    for cm in comms:
        got, rest = rest[:len(cm.out_shapes)], rest[len(cm.out_shapes):]
        if cm.on_done is not None:
            cm.on_done(got)
    return base


def run_comm(comms, name):
    _call(lambda: None, name=name, grid=(), in_specs=[], out_specs=[], out_shape=[], args=[], comms=comms)


def _place():
    x, y, c = lax.axis_index("x"), lax.axis_index("y"), lax.axis_index("c")
    chips = [(1 - x, y), (x, 1 - y), (1 - x, 1 - y)]
    return x, y, c, chips


def _half(ref, h, lead=()):
    n = ref.shape[len(lead)] // 2
    return ref.at[(*lead, pl.ds(h * n, n))]


def _remote(src, dst, send_sems, recv_sems, k, to):
    return pltpu.make_async_remote_copy(src_ref=src, dst_ref=dst, send_sem=send_sems.at[k], recv_sem=recv_sems.at[k],
                                        device_id=to, device_id_type=MESH)


def gather_ici(shards, on_done):
    n = len(shards)

    def start(ins, outs, ss, rs):
        x, y, c, chips = _place()
        own = 2 * x + y
        for t in range(n):
            for k, chip in enumerate(chips):
                _remote(_half(ins[t], c), _half(outs[t], c, (own,)), ss, rs, 3 * t + k, (*chip, c)).start()

    def finish(ins, outs, ss, rs):
        x, y, c, chips = _place()
        for t in range(n):
            for k, (px, py) in enumerate(chips):
                cp = _remote(_half(ins[t], c), _half(outs[t], c, (2 * px + py,)), ss, rs, 3 * t + k, (px, py, c))
                cp.wait_recv()
                cp.wait_send()

    return Comm(shards, [_sds((N_CHIPS,) + a.shape, a.dtype) for a in shards], 3 * n, start, finish, on_done=on_done)


def gather_forward(bufs, on_done):
    n = len(bufs)

    def start(ins, outs, ss, rs):
        x, y, c, chips = _place()
        for t in range(n):
            for k, (px, py) in enumerate(chips):
                blk = _half(outs[t], c, (2 * px + py,))
                _remote(blk, blk, ss, rs, 3 * t + k, (x, y, 1 - c)).start()

    def finish(ins, outs, ss, rs):
        x, y, c, chips = _place()
        for t in range(n):
            for k, (px, py) in enumerate(chips):
                cp = _remote(_half(outs[t], c, (2 * px + py,)), _half(outs[t], 1 - c, (2 * px + py,)), ss, rs, 3 * t + k, (x, y, 1 - c))
                cp.wait_recv()
                cp.wait_send()

    return Comm(bufs, [_sds(b.shape, b.dtype) for b in bufs], 3 * n, start, finish,
                aliases=[(t, t) for t in range(n)], on_done=on_done)


def swap_halves(grads, on_done):
    n = len(grads)

    def copies(ins, outs, ss, rs):
        x, y, c, _ = _place()
        out = []
        for t in range(n):
            for k in range(N_CHIPS):
                out.append(_remote(_half(ins[t], 1 - c, (k,)), outs[t].at[k], ss, rs, N_CHIPS * t + k, (x, y, 1 - c)))
        return out

    def start(ins, outs, ss, rs):
        for cp in copies(ins, outs, ss, rs):
            cp.start()

    def finish(ins, outs, ss, rs):
        for cp in copies(ins, outs, ss, rs):
            cp.wait()

    return Comm(grads, [_sds((N_CHIPS, g.shape[1] // 2, g.shape[2]), g.dtype) for g in grads], N_CHIPS * n, start, finish,
                on_done=on_done)


def scatter_partials(parts, on_done):
    n = len(parts)

    def start(ins, outs, ss, rs):
        x, y, c, chips = _place()
        own = 2 * x + y
        for t in range(n):
            for k, (px, py) in enumerate(chips):
                _remote(ins[t].at[2 * px + py], outs[t].at[own], ss, rs, 3 * t + k, (px, py, c)).start()

    def finish(ins, outs, ss, rs):
        x, y, c, chips = _place()
        own = 2 * x + y
        for t in range(n):
            for k, (px, py) in enumerate(chips):
                cp = _remote(ins[t].at[own], outs[t].at[2 * px + py], ss, rs, 3 * t + k, (px, py, c))
                cp.wait_recv()
                cp.wait_send()

    return Comm(parts, [_sds(p.shape, p.dtype) for p in parts], 3 * n, start, finish, on_done=on_done)


def share_halves(fins, on_done):
    n = len(fins)

    def copies(outs, ss, rs, c, to):
        return [_remote(_half(outs[t], c), _half(outs[t], c), ss, rs, t, to) for t in range(n)]

    def start(ins, outs, ss, rs):
        x, y, c, _ = _place()
        for cp in copies(outs, ss, rs, c, (x, y, 1 - c)):
            cp.start()

    def finish(ins, outs, ss, rs):
        x, y, c, _ = _place()
        for t in range(n):
            cp = _remote(_half(outs[t], c), _half(outs[t], 1 - c), ss, rs, t, (x, y, 1 - c))
            cp.wait_recv()
            cp.wait_send()

    return Comm(fins, [_sds(f.shape, f.dtype) for f in fins], n, start, finish, aliases=[(t, t) for t in range(n)], on_done=on_done)


def pair_sum(g, other, place, name):
    _, r, c = g.shape
    h = r // 2
    tm = ROW_TILE if h % ROW_TILE == 0 else h
    nb = h // tm

    def body(pl_ref, g_ref, o_ref, own_ref, wire_ref):
        k = pl.program_id(1)
        val = g_ref[...] + o_ref[...]
        wire_ref[...] = val.astype(BF16)

        @pl.when(k == pl_ref[1])
        def _():
            own_ref[...] = val

    return pl.pallas_call(
        body, name=name,
        grid_spec=pltpu.PrefetchScalarGridSpec(
            num_scalar_prefetch=1, grid=(nb, N_CHIPS),
            in_specs=[pl.BlockSpec((None, tm, c), lambda i, k, p: (k, p[0] * nb + i, 0)),
                      pl.BlockSpec((None, tm, c), lambda i, k, p: (k, i, 0))],
            out_specs=[pl.BlockSpec((tm, c), lambda i, k, p: (i, 0)),
                       pl.BlockSpec((None, tm, c), lambda i, k, p: (k, i, 0))]),
        out_shape=[_sds((h, c), F32), _sds((N_CHIPS, h, c), BF16)],
        compiler_params=pltpu.CompilerParams(dimension_semantics=("parallel", "arbitrary"), vmem_limit_bytes=VMEM_LIMIT))(place, g, other)


def chip_sum(own, recv, place, name):
    h, c = own.shape
    tm = ROW_TILE if h % ROW_TILE == 0 else h
    nb = h // tm

    def body(pl_ref, own_ref, a_ref, b_ref, c_ref, out_ref):
        out_ref[...] = ((own_ref[...] + a_ref[...].astype(F32)) + b_ref[...].astype(F32)) + c_ref[...].astype(F32)

    slot = lambda d: pl.BlockSpec((None, tm, c), lambda i, p: ((p[1] + d) % N_CHIPS, i, 0))
    return pl.pallas_call(
        body, name=name,
        grid_spec=pltpu.PrefetchScalarGridSpec(
            num_scalar_prefetch=1, grid=(nb,),
            in_specs=[pl.BlockSpec((tm, c), lambda i, p: (i, 0)), slot(1), slot(2), slot(3)],
            out_specs=pl.BlockSpec((tm, c), lambda i, p: (p[0] * nb + i, 0))),
        out_shape=_sds((2 * h, c), F32),
        compiler_params=pltpu.CompilerParams(dimension_semantics=("parallel",), vmem_limit_bytes=VMEM_LIMIT))(place, own, recv, recv, recv)


def allreduce_small(packed):
    rows, cols = packed.shape

    def body(in_ref, out_ref, buf_ref, send_sems, recv_sems):
        x, y, c, _ = _place()
        me = 4 * x + 2 * y + c
        buf_ref[me] = in_ref[...]
        peers = [(x ^ (k >> 2), y ^ ((k >> 1) & 1), c ^ (k & 1)) for k in range(1, 8)]
        cps = [_remote(in_ref, buf_ref.at[me], send_sems, recv_sems, k, peer) for k, peer in enumerate(peers)]
        for cp in cps:
            cp.start()
        for k, (px, py, pc) in enumerate(peers):
            _remote(in_ref, buf_ref.at[4 * px + 2 * py + pc], send_sems, recv_sems, k, (px, py, pc)).wait_recv()
        for cp in cps:
            cp.wait_send()
        total = buf_ref[0]
        for d in range(1, 8):
            total = total + buf_ref[d]
        out_ref[...] = total

    vm = pl.BlockSpec(memory_space=pltpu.VMEM)
    return pl.pallas_call(
        body, name="allreduce_small", out_shape=_sds((rows, cols), F32), in_specs=[vm], out_specs=vm,
        scratch_shapes=[pltpu.VMEM((8, rows, cols), F32), pltpu.SemaphoreType.DMA((7,)), pltpu.SemaphoreType.DMA((7,))])(packed)


def rms_fwd(x, g, name):
    m, d = x.shape
    tm = min(m, ROW_TILE)

    def body(x_ref, g_ref, h_ref):
        xv = x_ref[...]
        r = lax.rsqrt(jnp.mean(xv * xv, axis=-1, keepdims=True) + EPS)
        h_ref[...] = ((xv * r) * g_ref[...]).astype(BF16)

    return _call(body, name=name, grid=(m // tm,),
                 in_specs=[pl.BlockSpec((tm, d), lambda i: (i, 0)), pl.BlockSpec((1, d), lambda i: (0, 0))],
                 out_specs=[pl.BlockSpec((tm, d), lambda i: (i, 0))], out_shape=[_sds((m, d), BF16)],
                 args=[x, g], sem=("parallel",))[0]


def resid_post(x, y, g, name):
    m, d = x.shape
    tm = min(m, ROW_TILE)

    def body(x_ref, y_ref, g_ref, o_ref):
        yv = y_ref[...]
        r = lax.rsqrt(jnp.mean(yv * yv, axis=-1, keepdims=True) + EPS)
        o_ref[...] = x_ref[...] + (yv * r) * g_ref[...]

    row = pl.BlockSpec((tm, d), lambda i: (i, 0))
    return _call(body, name=name, grid=(m // tm,), in_specs=[row, row, pl.BlockSpec((1, d), lambda i: (0, 0))],
                 out_specs=[row], out_shape=[_sds((m, d), F32)], args=[x, y, g], sem=("parallel",))[0]


def rms_bwd(y, g, dout, name, add=None, out_dtype=F32, comms=()):
    m, d = y.shape
    tm = min(m, ROW_TILE)
    has_add = add is not None

    def body(*refs):
        y_ref, g_ref, d_ref = refs[:3]
        dy_ref, dg_ref = refs[-2:]
        i = pl.program_id(0)
        yv = y_ref[...]
        dv = d_ref[...].astype(F32)
        r = lax.rsqrt(jnp.mean(yv * yv, axis=-1, keepdims=True) + EPS)
        gy = g_ref[...] * dv
        t = jnp.mean(yv * gy, axis=-1, keepdims=True) * (r * r)
        dy = r * (gy - yv * t)
        if has_add:
            dy = dy + refs[3][...]
        dy_ref[...] = dy.astype(dy_ref.dtype)
        part = jnp.sum(dv * (yv * r), axis=0, keepdims=True)

        @pl.when(i == 0)
        def _():
            dg_ref[...] = part

        @pl.when(i > 0)
        def _():
            dg_ref[...] += part

    row = pl.BlockSpec((tm, d), lambda i: (i, 0))
    vec = pl.BlockSpec((1, d), lambda i: (0, 0))
    return _call(body, name=name, grid=(m // tm,), in_specs=[row, vec, row] + ([row] if has_add else []),
                 out_specs=[row, vec], out_shape=[_sds((m, d), out_dtype), _sds((1, d), F32)],
                 args=[y, g, dout] + ([add] if has_add else []), sem=("arbitrary",), comms=comms)


def loss_head(xf, tgt, name):
    m, d = xf.shape
    tm = min(m, ROW_TILE)

    def body(x_ref, t_ref, s_ref, dx_ref):
        i = pl.program_id(0)
        err = x_ref[...] - t_ref[...]
        dx_ref[...] = err * (1.0 / d)
        part = jnp.sum(err * err, axis=0, keepdims=True)

        @pl.when(i == 0)
        def _():
            s_ref[...] = part

        @pl.when(i > 0)
        def _():
            s_ref[...] += part

    row = pl.BlockSpec((tm, d), lambda i: (i, 0))
    vec = pl.BlockSpec((1, d), lambda i: (0, 0))
    return _call(body, name=name, grid=(m // tm,), in_specs=[row, row], out_specs=[vec, row],
                 out_shape=[_sds((1, d), F32), _sds((m, d), F32)], args=[xf, tgt], sem=("arbitrary",))


_DOT_DIMS = {"nn": (((1,), (0,)), ((), ())), "nt": (((1,), (1,)), ((), ())), "tn": (((0,), (0,)), ((), ()))}


def matmul(a, b, *, mode, dims, tiles, name, out_shapes, out_specs=None, epi=None, extras=(), extra_specs=(),
           a_spec=None, b_spec=None, comms=(), aliases=None, store=None):
    m, n, k = dims
    tm, tn, tk = tiles
    nk = k // tk
    assert m % tm == 0 and n % tn == 0 and k % tk == 0
    n_ex, n_out = len(extras), len(out_shapes)
    if a_spec is None:
        a_spec = pl.BlockSpec((tk, tm), lambda i, j, kk: (kk, i)) if mode == "tn" else pl.BlockSpec((tm, tk), lambda i, j, kk: (i, kk))
    if b_spec is None:
        b_spec = pl.BlockSpec((tn, tk), lambda i, j, kk: (j, kk)) if mode == "nt" else pl.BlockSpec((tk, tn), lambda i, j, kk: (kk, j))
    if out_specs is None:
        out_specs = [pl.BlockSpec((tm, tn), lambda i, j, kk: (i, j)) for _ in out_shapes]
    if epi is None:
        epi = lambda acc: (acc,)

    def body(*refs):
        a_ref, b_ref = refs[0], refs[1]
        ex = refs[2:2 + n_ex]
        outs = refs[2 + n_ex:2 + n_ex + n_out]
        part = lax.dot_general(a_ref[...], b_ref[...], _DOT_DIMS[mode], preferred_element_type=F32)

        def finish(acc):
            for o_ref, val in zip(outs, epi(acc, *ex)):
                if store is None:
                    o_ref[...] = val.astype(o_ref.dtype)
                else:
                    store(o_ref, val)

        if nk == 1:
            finish(part)
        else:
            acc_ref = refs[-1]
            kk = pl.program_id(2)

            @pl.when(kk == 0)
            def _():
                acc_ref[...] = part

            @pl.when(kk > 0)
            def _():
                acc_ref[...] += part

            @pl.when(kk == nk - 1)
            def _():
                finish(acc_ref[...])

    return _call(body, name=name, grid=(m // tm, n // tn, nk), in_specs=[a_spec, b_spec] + list(extra_specs),
                 out_specs=list(out_specs), out_shape=list(out_shapes), args=[a, b, *extras],
                 scratch=[pltpu.VMEM((tm, tn), F32)] if nk > 1 else [], sem=("parallel", "parallel", "arbitrary"),
                 comms=comms, aliases=aliases)


def _sigmoid(v):
    return 1.0 / (1.0 + jnp.exp(-v))


def conv_fwd(ag, cw, cb, lg, lb, name, comms=()):
    s = ag.shape[0]
    tm = min(s, ROW_TILE)
    c = CONV_CH
    hb = tm // CONV_HALO

    def body(ag_ref, halo_ref, w_ref, cb_ref, lg_ref, lb_ref, u1_ref, u_ref, ext_ref):
        i = pl.program_id(0)
        u0 = ag_ref[:, :c] * _sigmoid(ag_ref[:, c:])
        h0 = halo_ref[:, :c] * _sigmoid(halo_ref[:, c:])
        ext_ref[0:CONV_HALO, :] = jnp.where(i == 0, 0.0, h0)
        ext_ref[CONV_HALO:, :] = u0
        acc = jnp.zeros((tm, c), F32) + cb_ref[...]
        for k in range(CONV_WIDTH):
            off = CONV_HALO - (CONV_WIDTH - 1) + k
            acc = acc + w_ref[k:k + 1, :] * ext_ref[off:off + tm, :]
        u1_ref[...] = acc
        mu = jnp.mean(acc, axis=-1, keepdims=True)
        xc = acc - mu
        rstd = lax.rsqrt(jnp.mean(xc * xc, axis=-1, keepdims=True) + EPS)
        u2 = (xc * rstd) * lg_ref[...] + lb_ref[...]
        u_ref[...] = (u2 * _sigmoid(u2)).astype(BF16)

    vec = pl.BlockSpec((1, c), lambda i: (0, 0))
    return _call(body, name=name, grid=(s // tm,),
                 in_specs=[pl.BlockSpec((tm, 2 * c), lambda i: (i, 0)),
                           pl.BlockSpec((CONV_HALO, 2 * c), lambda i: (jnp.maximum(i * hb - 1, 0), 0)),
                           pl.BlockSpec((CONV_HALO, c), lambda i: (0, 0)), vec, vec, vec],
                 out_specs=[pl.BlockSpec((tm, c), lambda i: (i, 0)), pl.BlockSpec((tm, c), lambda i: (i, 0))],
                 out_shape=[_sds((s, c), F32), _sds((s, 2 * c), BF16)],
                 args=[ag, ag, cw, cb, lg, lb], scratch=[pltpu.VMEM((tm + CONV_HALO, c), F32)], sem=("parallel",), comms=comms)


def conv_bwd(du, u1, ag, cw, lg, lb, name, comms=()):
    s = du.shape[0]
    tm = min(s, ROW_TILE)
    c = CONV_CH
    hb = tm // CONV_HALO
    nt = s // tm
    last_halo = s // CONV_HALO - 1

    def ln_silu_bwd(du_v, u1_v, lg_v, lb_v):
        mu = jnp.mean(u1_v, axis=-1, keepdims=True)
        xc = u1_v - mu
        rstd = lax.rsqrt(jnp.mean(xc * xc, axis=-1, keepdims=True) + EPS)
        xh = xc * rstd
        u2 = xh * lg_v + lb_v
        sg = _sigmoid(u2)
        du2 = du_v * (sg * (1.0 + u2 * (1.0 - sg)))
        dxh = du2 * lg_v
        du1 = rstd * (dxh - jnp.mean(dxh, axis=-1, keepdims=True) - xh * jnp.mean(dxh * xh, axis=-1, keepdims=True))
        return du1, du2, xh

    def body(du_ref, dun_ref, u1_ref, u1n_ref, ag_ref, agp_ref, w_ref, lg_ref, lb_ref, dag_ref, sm_ref, ext_ref, dext_ref):
        i = pl.program_id(0)
        lg_v, lb_v = lg_ref[...], lb_ref[...]
        du1, du2, xh = ln_silu_bwd(du_ref[...], u1_ref[...], lg_v, lb_v)
        du1n, _, _ = ln_silu_bwd(dun_ref[...], u1n_ref[...], lg_v, lb_v)
        dext_ref[0:tm, :] = du1
        dext_ref[tm:, :] = jnp.where(i == nt - 1, 0.0, du1n)
        a, g = ag_ref[:, :c], ag_ref[:, c:]
        sg = _sigmoid(g)
        ext_ref[0:CONV_HALO, :] = jnp.where(i == 0, 0.0, agp_ref[:, :c] * _sigmoid(agp_ref[:, c:]))
        ext_ref[CONV_HALO:, :] = a * sg

        @pl.when(i == 0)
        def _():
            sm_ref[...] = jnp.zeros_like(sm_ref)

        du0 = jnp.zeros((tm, c), F32)
        for k in range(CONV_WIDTH):
            back = CONV_WIDTH - 1 - k
            du0 = du0 + w_ref[k:k + 1, :] * dext_ref[back:back + tm, :]
            off = CONV_HALO - (CONV_WIDTH - 1) + k
            sm_ref[k:k + 1, :] += jnp.sum(du1 * ext_ref[off:off + tm, :], axis=0, keepdims=True)
        sm_ref[32:33, :] += jnp.sum(du1, axis=0, keepdims=True)
        sm_ref[33:34, :] += jnp.sum(du2 * xh, axis=0, keepdims=True)
        sm_ref[34:35, :] += jnp.sum(du2, axis=0, keepdims=True)
        dag_ref[:, :c] = (du0 * sg).astype(BF16)
        dag_ref[:, c:] = (du0 * a * (sg * (1.0 - sg))).astype(BF16)

    vec = pl.BlockSpec((1, c), lambda i: (0, 0))
    tile = pl.BlockSpec((tm, c), lambda i: (i, 0))
    nxt = pl.BlockSpec((CONV_HALO, c), lambda i: (jnp.minimum((i + 1) * hb, last_halo), 0))
    return _call(body, name=name, grid=(nt,),
                 in_specs=[tile, nxt, tile, nxt,
                           pl.BlockSpec((tm, 2 * c), lambda i: (i, 0)),
                           pl.BlockSpec((CONV_HALO, 2 * c), lambda i: (jnp.maximum(i * hb - 1, 0), 0)),
                           pl.BlockSpec((CONV_HALO, c), lambda i: (0, 0)), vec, vec],
                 out_specs=[pl.BlockSpec((tm, 2 * c), lambda i: (i, 0)), pl.BlockSpec((40, c), lambda i: (0, 0))],
                 out_shape=[_sds((s, IN_PAD), BF16), _sds((40, c), F32)],
                 args=[du, du, u1, u1, ag, ag, cw, lg, lb],
                 scratch=[pltpu.VMEM((tm + CONV_HALO, c), F32), pltpu.VMEM((tm + CONV_HALO, c), F32)], sem=("arbitrary",), comms=comms)


CUM_BLOCK = 256


def _tri(n, upper):
    r = lax.broadcasted_iota(jnp.int32, (n, n), 0)
    cidx = lax.broadcasted_iota(jnp.int32, (n, n), 1)
    return jnp.where((r <= cidx) if upper else (r >= cidx), 1.0, 0.0).astype(F32)


def fox_gate_fwd(fl_t, bf, name):
    h, s = fl_t.shape
    nb = s // CUM_BLOCK

    def body(fl_ref, bf_ref, cum_ref):
        tri = _tri(CUM_BLOCK, True)
        carry = jnp.zeros((h, 1), F32)
        for b in range(nb):
            v = fl_ref[:, b * CUM_BLOCK:(b + 1) * CUM_BLOCK] + bf_ref[...]
            logf = jnp.minimum(v, 0.0) - jnp.log(1.0 + jnp.exp(-jnp.abs(v)))
            cs = jnp.dot(logf, tri, precision=lax.Precision.HIGHEST, preferred_element_type=F32) + carry
            cum_ref[:, b * CUM_BLOCK:(b + 1) * CUM_BLOCK] = cs
            carry = carry + jnp.sum(logf, axis=-1, keepdims=True)

    return pl.pallas_call(body, name=name, out_shape=_sds((h, s), F32),
                          compiler_params=pltpu.CompilerParams(vmem_limit_bytes=VMEM_LIMIT))(fl_t, bf)


def fox_gate_bwd(dcol_t, drow_t, fl_t, bf, name):
    h, s = fl_t.shape
    nb = s // CUM_BLOCK

    def body(dcol_ref, drow_ref, fl_ref, bf_ref, dfl_ref, dbf_ref):
        tri = _tri(CUM_BLOCK, False)
        carry = jnp.zeros((h, 1), F32)
        dbf = jnp.zeros((h, 1), F32)
        for b in reversed(range(nb)):
            sl = slice(b * CUM_BLOCK, (b + 1) * CUM_BLOCK)
            dcb = dcol_ref[:, sl] + drow_ref[:, sl]
            dlogf = jnp.dot(dcb, tri, precision=lax.Precision.HIGHEST, preferred_element_type=F32) + carry
            carry = carry + jnp.sum(dcb, axis=-1, keepdims=True)
            dfl = dlogf * _sigmoid(-(fl_ref[:, sl] + bf_ref[...]))
            dfl_ref[:, sl] = dfl
            dbf = dbf + jnp.sum(dfl, axis=-1, keepdims=True)
        dbf_ref[...] = dbf

    return pl.pallas_call(body, name=name, out_shape=[_sds((h, s), F32), _sds((h, 1), F32)],
                          compiler_params=pltpu.CompilerParams(vmem_limit_bytes=VMEM_LIMIT))(dcol_t, drow_t, fl_t, bf)


FOX_TILE = 512


def _causal_mask(tq, tk, q0, k0):
    row = lax.broadcasted_iota(jnp.int32, (tq, tk), 0) + q0
    col = lax.broadcasted_iota(jnp.int32, (tq, tk), 1) + k0
    return row >= col


def _fox_bias(c_ref, hh, q0, k0, t):
    c_q = jnp.max(c_ref[hh:hh + 1, pl.ds(q0, 128)], axis=-1, keepdims=True)
    return c_q - c_ref[hh:hh + 1, pl.ds(k0, t)]


def fox_fwd(qkv, cum4, cat, name, comms=()):
    s = qkv.shape[0]
    t = min(s, FOX_TILE)
    nq = s // t
    dn = _DOT_DIMS["nt"]

    def body(q_ref, k_ref, v_ref, c_ref, cat_ref, o_ref, lse_ref):
        i = pl.program_id(1)
        q0 = pl.multiple_of(i * t, t)
        lane = lax.broadcasted_iota(jnp.int32, (t, 128), 1)
        qv = q_ref[...]
        zero = jnp.zeros_like(qv)
        q_h = (jnp.where(lane < 64, qv, zero), jnp.where(lane >= 64, qv, zero))

        def step(j, carry, masked):
            k0 = pl.multiple_of(j * t, t)
            kj = k_ref[pl.ds(k0, t), :]
            vj = v_ref[pl.ds(k0, t), :]
            out = []
            for hh in range(2):
                m_prev, l_prev, acc_prev = carry[hh]
                bias = _fox_bias(c_ref, hh, q0, k0, t)
                sc = lax.dot_general(q_h[hh], kj, dn, preferred_element_type=F32) * FOX_SCALE + bias
                if masked:
                    sc = jnp.where(_causal_mask(t, t, q0, k0), sc, NEG_INF)
                m_new = jnp.maximum(m_prev, jnp.max(sc, axis=-1, keepdims=True))
                alpha = jnp.exp(m_prev - m_new)
                p = jnp.exp(sc - m_new)
                l_new = alpha * l_prev + jnp.sum(p, axis=-1, keepdims=True)
                acc_new = alpha * acc_prev + jnp.dot(p.astype(BF16), vj, preferred_element_type=F32)
                out.append((m_new, l_new, acc_new))
            return tuple(out)

        init = tuple((jnp.full((t, 1), NEG_INF, F32), jnp.zeros((t, 1), F32), jnp.zeros((t, 128), F32)) for _ in range(2))
        carry = lax.fori_loop(0, i, lambda j, cr: step(j, cr, False), init)
        carry = step(i, carry, True)
        (m_a, l_a, acc_a), (m_b, l_b, acc_b) = carry
        o_ref[...] = jnp.where(lane < 64, acc_a / l_a, acc_b / l_b).astype(BF16)
        lse_ref[0] = jnp.broadcast_to(m_a + jnp.log(l_a), (t, 128))
        lse_ref[1] = jnp.broadcast_to(m_b + jnp.log(l_b), (t, 128))

    return _call(body, name=name, grid=(4, nq),
                 in_specs=[pl.BlockSpec((t, 128), lambda p, i: (i, 3 * p)),
                           pl.BlockSpec((s, 128), lambda p, i: (0, 3 * p + 1)),
                           pl.BlockSpec((s, 128), lambda p, i: (0, 3 * p + 2)),
                           pl.BlockSpec((None, 2, s), lambda p, i: (p, 0, 0)), ANY],
                 out_specs=[pl.BlockSpec((t, 128), lambda p, i: (i, 4 + p)),
                            pl.BlockSpec((2, t, 128), lambda p, i: (p, i, 0))],
                 out_shape=[_sds((s, 2 * FOX_WIDTH), BF16), _sds((FOX_HEADS, s, 128), F32)],
                 args=[qkv, qkv, qkv, cum4, cat], sem=("parallel", "arbitrary"), comms=comms, aliases={4: 0})


def fox_bwd(qkv, cum4, cat, datt, lse, dz, name, comms=()):
    s = qkv.shape[0]
    t = min(s, FOX_TILE)
    nk = s // t
    nt_dims, tn_dims = _DOT_DIMS["nt"], _DOT_DIMS["tn"]

    def body(q_ref, k_ref, v_ref, c_ref, o_ref, do_ref, lse_ref, dz_in, dz_ref, dc_ref, dr_ref, dq_acc, dk_acc, dv_acc, dr_acc):
        j = pl.program_id(1)
        k0 = pl.multiple_of(j * t, t)
        lane = lax.broadcasted_iota(jnp.int32, (t, 128), 1)
        lo = lane < 64
        kj, vj = k_ref[...], v_ref[...]
        zero = jnp.zeros_like(kj)
        k_h = (jnp.where(lo, kj, zero), jnp.where(lo, zero, kj))

        @pl.when(j == 0)
        def _():
            dq_acc[...] = jnp.zeros_like(dq_acc)
            dr_acc[...] = jnp.zeros_like(dr_acc)

        dk_acc[...] = jnp.zeros_like(dk_acc)
        dv_acc[...] = jnp.zeros_like(dv_acc)

        def step(i, dc, masked):
            q0 = pl.multiple_of(i * t, t)
            qi = q_ref[pl.ds(q0, t), :]
            doi = do_ref[pl.ds(q0, t), :]
            prod = doi.astype(F32) * o_ref[pl.ds(q0, t), :].astype(F32)
            zq = jnp.zeros_like(qi)
            dq_new = jnp.zeros((t, 128), F32)
            dc_out = []
            for hh in range(2):
                sel = lo if hh == 0 else jnp.logical_not(lo)
                q_m = jnp.where(sel, qi, zq)
                do_m = jnp.where(sel, doi, zq)
                delta = jnp.sum(jnp.where(sel, prod, 0.0), axis=-1, keepdims=True)
                bias = _fox_bias(c_ref, hh, q0, k0, t)
                sc = lax.dot_general(q_m, kj, nt_dims, preferred_element_type=F32) * FOX_SCALE + bias
                if masked:
                    sc = jnp.where(_causal_mask(t, t, q0, k0), sc, NEG_INF)
                lse_t = jnp.tile(lse_ref[hh, pl.ds(q0, t), :], (1, t // 128))
                p = jnp.exp(sc - lse_t)
                dv_acc[hh] += lax.dot_general(p.astype(BF16), doi, tn_dims, preferred_element_type=F32)
                dp = lax.dot_general(do_m, vj, nt_dims, preferred_element_type=F32)
                ds = p * (dp - delta)
                dc_out.append(dc[hh] - jnp.sum(ds, axis=0, keepdims=True))
                dr_acc[hh, pl.ds(q0, t), :] += jnp.sum(ds, axis=-1, keepdims=True)
                ds_b = ds.astype(BF16)
                dq_new = dq_new + jnp.dot(ds_b, k_h[hh], preferred_element_type=F32)
                dk_acc[hh] += lax.dot_general(ds_b, qi, tn_dims, preferred_element_type=F32)
            dq_acc[pl.ds(q0, t), :] += dq_new
            return tuple(dc_out)

        dc = step(j, (jnp.zeros((1, t), F32), jnp.zeros((1, t), F32)), True)
        dc = lax.fori_loop(j + 1, nk, lambda i, cr: step(i, cr, False), dc)
        dz_ref[pl.ds(k0, t), 128:256] = (jnp.where(lo, dk_acc[0], dk_acc[1]) * FOX_SCALE).astype(BF16)
        dz_ref[pl.ds(k0, t), 256:384] = jnp.where(lo, dv_acc[0], dv_acc[1]).astype(BF16)
        dc_ref[0:1, :] = dc[0]
        dc_ref[1:2, :] = dc[1]

        @pl.when(j == nk - 1)
        def _():
            dz_ref[:, 0:128] = (dq_acc[...] * FOX_SCALE).astype(BF16)
            eye = lax.broadcasted_iota(jnp.int32, (t, t), 0) == lax.broadcasted_iota(jnp.int32, (t, t), 1)
            for hh in range(2):
                for b in range(nk):
                    col = dr_acc[hh, b * t:(b + 1) * t, :]
                    dr_ref[hh:hh + 1, b * t:(b + 1) * t] = jnp.sum(jnp.where(eye, col, 0.0), axis=0, keepdims=True)

    return _call(body, name=name, grid=(4, nk),
                 in_specs=[pl.BlockSpec((s, 128), lambda p, j: (0, 3 * p)),
                           pl.BlockSpec((t, 128), lambda p, j: (j, 3 * p + 1)),
                           pl.BlockSpec((t, 128), lambda p, j: (j, 3 * p + 2)),
                           pl.BlockSpec((None, 2, s), lambda p, j: (p, 0, 0)),
                           pl.BlockSpec((s, 128), lambda p, j: (0, 4 + p)),
                           pl.BlockSpec((s, 128), lambda p, j: (0, p)),
                           pl.BlockSpec((2, s, 128), lambda p, j: (p, 0, 0)), ANY],
                 out_specs=[pl.BlockSpec((s, 384), lambda p, j: (0, 3 + p)),
                            pl.BlockSpec((None, 2, t), lambda p, j: (p, 0, j)),
                            pl.BlockSpec((None, 2, s), lambda p, j: (p, 0, 0))],
                 out_shape=[_sds((s, IN_PAD), BF16), _sds((4, 2, s), F32), _sds((4, 2, s), F32)],
                 args=[qkv, qkv, qkv, cum4, cat, datt, lse, dz],
                 scratch=[pltpu.VMEM((s, 128), F32), pltpu.VMEM((2, t, 128), F32), pltpu.VMEM((2, t, 128), F32),
                          pltpu.VMEM((2, s, 1), F32)],
                 sem=("parallel", "arbitrary"), comms=comms, aliases={7: 0})


def _mem_probs(q_h, k_h):
    sc = lax.dot_general(q_h, k_h, _DOT_DIMS["nt"], preferred_element_type=F32) * MEM_SCALE
    e = jnp.exp(sc - jnp.max(sc, axis=-1, keepdims=True))
    return e / jnp.sum(e, axis=-1, keepdims=True)


def mem_attn_fwd(qm, km, vm, name):
    s = qm.shape[0]
    tm = min(s, ROW_TILE)

    def body(q_ref, k_ref, v_ref, o_ref):
        for h in range(MEM_HEADS):
            sl = slice(h * MEM_HEAD_DIM, (h + 1) * MEM_HEAD_DIM)
            p = _mem_probs(q_ref[:, sl], k_ref[:, sl])
            o_ref[:, sl] = jnp.dot(p.astype(BF16), v_ref[:, sl], preferred_element_type=F32).astype(BF16)

    kv = pl.BlockSpec((N_MEM, MEM_INNER), lambda i: (0, 0))
    row = pl.BlockSpec((tm, MEM_INNER), lambda i: (i, 0))
    return _call(body, name=name, grid=(s // tm,), in_specs=[row, kv, kv], out_specs=[row],
                 out_shape=[_sds((s, MEM_INNER), BF16)], args=[qm, km, vm], sem=("parallel",))[0]


def mem_attn_bwd(qm, km, vm, dom, name):
    s = qm.shape[0]
    tm = min(s, ROW_TILE)
    tn_dims = _DOT_DIMS["tn"]

    def body(q_ref, k_ref, v_ref, do_ref, dq_ref, dk_ref, dv_ref):
        i = pl.program_id(0)

        @pl.when(i == 0)
        def _():
            dk_ref[...] = jnp.zeros_like(dk_ref)
            dv_ref[...] = jnp.zeros_like(dv_ref)

        for h in range(MEM_HEADS):
            sl = slice(h * MEM_HEAD_DIM, (h + 1) * MEM_HEAD_DIM)
            q_h, k_h, do_h = q_ref[:, sl], k_ref[:, sl], do_ref[:, sl]
            p = _mem_probs(q_h, k_h)
            dp = lax.dot_general(do_h, v_ref[:, sl], _DOT_DIMS["nt"], preferred_element_type=F32)
            ds = p * (dp - jnp.sum(p * dp, axis=-1, keepdims=True))
            ds_b = (ds * MEM_SCALE).astype(BF16)
            dq_ref[:, sl] = jnp.dot(ds_b, k_h, preferred_element_type=F32).astype(BF16)
            dk_ref[:, sl] += lax.dot_general(ds_b, q_h, tn_dims, preferred_element_type=F32)
            dv_ref[:, sl] += lax.dot_general(p.astype(BF16), do_h, tn_dims, preferred_element_type=F32)

    kv = pl.BlockSpec((N_MEM, MEM_INNER), lambda i: (0, 0))
    row = pl.BlockSpec((tm, MEM_INNER), lambda i: (i, 0))
    return _call(body, name=name, grid=(s // tm,), in_specs=[row, kv, kv, row], out_specs=[row, kv, kv],
                 out_shape=[_sds((s, MEM_INNER), BF16), _sds((N_MEM, MEM_INNER), F32), _sds((N_MEM, MEM_INNER), F32)],
                 args=[qm, km, vm, dom], sem=("arbitrary",))


def _adam_update(w, g, m, v):
    c1 = 1.0 - ADAM_B1 ** ADAM_STEP
    c2 = 1.0 - ADAM_B2 ** ADAM_STEP
    nm = ADAM_B1 * m + (1.0 - ADAM_B1) * g
    nv = ADAM_B2 * v + (1.0 - ADAM_B2) * (g * g)
    return -ADAM_LR * ((nm / c1) / (jnp.sqrt(nv / c2) + ADAM_EPS) + ADAM_WD * w), nm, nv


def adamw(w, g, m, v, name, comms=()):
    r, c = w.shape
    tm = ROW_TILE if r % ROW_TILE == 0 else r

    def body(w_ref, g_ref, m_ref, v_ref, d_ref, nm_ref, nv_ref):
        d_ref[...], nm_ref[...], nv_ref[...] = _adam_update(w_ref[...], g_ref[...], m_ref[...], v_ref[...])

    blk = pl.BlockSpec((tm, c), lambda i: (i, 0))
    return _call(body, name=name, grid=(r // tm,), in_specs=[blk] * 4, out_specs=[blk] * 3,
                 out_shape=[_sds((r, c), F32)] * 3, args=[w, g, m, v], sem=("parallel",), comms=comms)


def adamw_layers(w, m, v, g0, g1, name, comms=()):
    _, r, c = w.shape
    tm = ROW_TILE if r % ROW_TILE == 0 else r

    def body(w_ref, m_ref, v_ref, g0_ref, g1_ref, d_ref, nm_ref, nv_ref, g_ref):
        g = jnp.where(pl.program_id(0) == 0, g0_ref[...], g1_ref[...])
        g_ref[...] = g
        d_ref[...], nm_ref[...], nv_ref[...] = _adam_update(w_ref[...], g, m_ref[...], v_ref[...])

    blk = pl.BlockSpec((None, tm, c), lambda l, i: (l, i, 0))
    return _call(body, name=name, grid=(2, r // tm),
                 in_specs=[blk, blk, blk, pl.BlockSpec((tm, c), lambda l, i: (i * (1 - l), 0)),
                           pl.BlockSpec((tm, c), lambda l, i: (i * l, 0))],
                 out_specs=[blk] * 4, out_shape=[_sds(w.shape, F32)] * 4, args=[w, m, v, g0, g1], sem=("parallel", "parallel"),
                 comms=comms)


def _vec(v):
    return v.reshape(1, -1)


def _hosted(hosts, name):
    make = hosts.get(name)
    return make() if make is not None else ()


def layer_fwd(x0, mem, w, sm, l, hosts):
    s = x0.shape[0]
    tm = min(s, ROW_TILE)
    nm = lambda base: f"{base}{l}"
    h1 = rms_fwd(x0, _vec(sm["norm_mix_pre"]), nm("rms_mix_pre"))
    row = lambda width: pl.BlockSpec((tm, width), lambda i, j, k: (i, 0))
    ag, flp, qkv = matmul(
        h1, w("w_in"), mode="nn", dims=(s, IN_PAD, D_MODEL), tiles=(tm, IN_PAD, D_MODEL), name=nm("mix_in"),
        out_shapes=[_sds((s, GATE_COL), F32), _sds((s, 128), F32), _sds((s, 3 * FOX_WIDTH), BF16)],
        out_specs=[row(GATE_COL), row(128), row(3 * FOX_WIDTH)],
        epi=lambda acc: (acc[:, :GATE_COL], acc[:, GATE_COL:QKV_COL], acc[:, QKV_COL:]))
    u1, cat = conv_fwd(ag, w("conv_w"), _vec(sm["conv_b"]), _vec(sm["conv_ln_g"]), _vec(sm["conv_ln_b"]), nm("conv_fwd"),
                       comms=_hosted(hosts, nm("conv_fwd")))
    fl_t = flp[:, :FOX_HEADS].T
    cum = fox_gate_fwd(fl_t, sm["b_forget"].reshape(FOX_HEADS, 1), nm("fox_gate_fwd"))
    cum4 = cum.reshape(4, 2, s)
    cat, lse = fox_fwd(qkv, cum4, cat, nm("fox_fwd"), comms=_hosted(hosts, nm("fox_fwd")))
    y1, = matmul(cat, w("w_out"), mode="nn", dims=(s, D_MODEL, D_MODEL), tiles=(tm, D_MODEL, D_MODEL), name=nm("mix_out"),
                 out_shapes=[_sds((s, D_MODEL), F32)], comms=_hosted(hosts, nm("mix_out")))
    x1 = resid_post(x0, y1, _vec(sm["norm_mix_post"]), nm("post_mix"))

    h2 = rms_fwd(x1, _vec(sm["norm_mem_pre"]), nm("rms_mem_pre"))
    mn = rms_fwd(mem, _vec(sm["norm_memkv"]), nm("rms_memkv"))
    qm, = matmul(h2, w("w_mq"), mode="nn", dims=(s, MEM_INNER, D_MODEL), tiles=(tm, MEM_INNER, D_MODEL), name=nm("mem_q"),
                 out_shapes=[_sds((s, MEM_INNER), BF16)], comms=_hosted(hosts, nm("mem_q")))
    km, = matmul(mn, w("w_mk"), mode="nn", dims=(N_MEM, MEM_INNER, D_MODEL), tiles=(N_MEM, MEM_INNER, D_MODEL), name=nm("mem_k"),
                 out_shapes=[_sds((N_MEM, MEM_INNER), BF16)])
    vm, = matmul(mn, w("w_mv"), mode="nn", dims=(N_MEM, MEM_INNER, D_MODEL), tiles=(N_MEM, MEM_INNER, D_MODEL), name=nm("mem_v"),
                 out_shapes=[_sds((N_MEM, MEM_INNER), BF16)])
    om = mem_attn_fwd(qm, km, vm, nm("mem_attn_fwd"))
    y2, = matmul(om, w("w_mo"), mode="nn", dims=(s, D_MODEL, MEM_INNER), tiles=(tm, D_MODEL, MEM_INNER), name=nm("mem_o"),
                 out_shapes=[_sds((s, D_MODEL), F32)], comms=_hosted(hosts, nm("mem_o")))
    x2 = resid_post(x1, y2, _vec(sm["norm_mem_post"]), nm("post_mem"))

    h3 = rms_fwd(x2, _vec(sm["norm_mlp_pre"]), nm("rms_mlp_pre"))

    def relu2(acc):
        r = jnp.maximum(acc, 0.0)
        return (r * r,)

    act, = matmul(h3, w("w_up"), mode="nn", dims=(s, D_FF, D_MODEL), tiles=(tm, 1024, D_MODEL), name=nm("mlp_up"),
                  out_shapes=[_sds((s, D_FF), BF16)], epi=relu2,
                  b_spec=pl.BlockSpec((None, D_MODEL, 1024), lambda i, j, k: (j, 0, 0)), comms=_hosted(hosts, nm("mlp_up")))
    y3, = matmul(act, w("w_down"), mode="nn", dims=(s, D_MODEL, D_FF), tiles=(tm, D_MODEL, 1024), name=nm("mlp_down"),
                 out_shapes=[_sds((s, D_MODEL), F32)], comms=_hosted(hosts, nm("mlp_down")))
    x3 = resid_post(x2, y3, _vec(sm["norm_mlp_post"]), nm("post_mlp"))
    saved = dict(x0=x0, h1=h1, ag=ag, qkv=qkv, fl_t=fl_t, u1=u1, cum4=cum4, lse=lse, cat=cat, y1=y1, x1=x1,
                 h2=h2, mn=mn, qm=qm, km=km, vm=vm, om=om, y2=y2, x2=x2, h3=h3, act=act, y3=y3)
    return x3, saved


def layer_bwd(dx3, mem, w, sm, sv, l, hosts, gw):
    s = dx3.shape[0]
    tm = min(s, ROW_TILE)
    nm = lambda base: f"{base}{l}"
    gs = {}
    shards = lambda g: g.reshape(N_CHIPS, g.shape[0] // N_CHIPS, g.shape[1])
    col_shards = lambda g: jnp.moveaxis(g.reshape(g.shape[0], N_CHIPS, g.shape[1] // N_CHIPS), 1, 0)
    dy3, gs["norm_mlp_post"] = rms_bwd(sv["y3"], _vec(sm["norm_mlp_post"]), dx3, nm("bwd_post_mlp"), out_dtype=BF16)
    g, = matmul(sv["act"], dy3, mode="tn", dims=(D_FF, D_MODEL, s), tiles=(1024, D_MODEL, tm), name=nm("dw_down"),
                out_shapes=[_sds((D_FF, D_MODEL), F32)], comms=_hosted(hosts, nm("dw_down")))
    gw[("w_down", l)] = shards(g)

    def dup_epi(acc, act_ref):
        return (acc * (2.0 * jnp.sqrt(act_ref[...].astype(F32))),)

    dup, = matmul(dy3, w("w_down"), mode="nt", dims=(s, D_FF, D_MODEL), tiles=(tm, 1024, D_MODEL), name=nm("d_act"),
                  out_shapes=[_sds((s, D_FF), BF16)], epi=dup_epi, extras=(sv["act"],),
                  extra_specs=(pl.BlockSpec((tm, 1024), lambda i, j, k: (i, j)),), comms=_hosted(hosts, nm("d_act")))
    def store_shards(o_ref, val):
        for k in range(N_CHIPS):
            o_ref[k] = val[:, 1024 * k:1024 * (k + 1)]

    gw[("w_up", l)], = matmul(sv["h3"], dup, mode="tn", dims=(D_MODEL, D_FF, s), tiles=(512, D_FF, tm), name=nm("dw_up"),
                              out_shapes=[_sds((N_CHIPS, D_MODEL, 1024), F32)],
                              out_specs=[pl.BlockSpec((N_CHIPS, 512, 1024), lambda i, j, k: (0, i, 0))], store=store_shards)
    dh3, = matmul(dup, w("w_up"), mode="nt", dims=(s, D_MODEL, D_FF), tiles=(tm, D_MODEL, 1024), name=nm("d_h3"),
                  out_shapes=[_sds((s, D_MODEL), F32)],
                  b_spec=pl.BlockSpec((None, D_MODEL, 1024), lambda i, j, k: (k, 0, 0)), comms=_hosted(hosts, nm("d_h3")))
    dx2, gs["norm_mlp_pre"] = rms_bwd(sv["x2"], _vec(sm["norm_mlp_pre"]), dh3, nm("bwd_pre_mlp"), add=dx3)

    dy2, gs["norm_mem_post"] = rms_bwd(sv["y2"], _vec(sm["norm_mem_post"]), dx2, nm("bwd_post_mem"), out_dtype=BF16)
    g, = matmul(sv["om"], dy2, mode="tn", dims=(MEM_INNER, D_MODEL, s), tiles=(MEM_INNER, D_MODEL, tm), name=nm("dw_mo"),
                out_shapes=[_sds((MEM_INNER, D_MODEL), F32)])
    gw[("w_mo", l)] = col_shards(g)
    dom, = matmul(dy2, w("w_mo"), mode="nt", dims=(s, MEM_INNER, D_MODEL), tiles=(tm, MEM_INNER, D_MODEL), name=nm("d_om"),
                  out_shapes=[_sds((s, MEM_INNER), BF16)])
    dqm, dkm, dvm = mem_attn_bwd(sv["qm"], sv["km"], sv["vm"], dom, nm("mem_attn_bwd"))
    g, = matmul(sv["h2"], dqm, mode="tn", dims=(D_MODEL, MEM_INNER, s), tiles=(D_MODEL, MEM_INNER, tm), name=nm("dw_mq"),
                out_shapes=[_sds((D_MODEL, MEM_INNER), F32)])
    gw[("w_mq", l)] = shards(g)
    dkm_b, dvm_b = dkm.astype(BF16), dvm.astype(BF16)
    g, = matmul(sv["mn"], dkm_b, mode="tn", dims=(D_MODEL, MEM_INNER, N_MEM), tiles=(D_MODEL, MEM_INNER, N_MEM),
                name=nm("dw_mk"), out_shapes=[_sds((D_MODEL, MEM_INNER), F32)])
    gw[("w_mk", l)] = shards(g)
    g, = matmul(sv["mn"], dvm_b, mode="tn", dims=(D_MODEL, MEM_INNER, N_MEM), tiles=(D_MODEL, MEM_INNER, N_MEM),
                name=nm("dw_mv"), out_shapes=[_sds((D_MODEL, MEM_INNER), F32)])
    gw[("w_mv", l)] = shards(g)
    dmn_k, = matmul(dkm_b, w("w_mk"), mode="nt", dims=(N_MEM, D_MODEL, MEM_INNER), tiles=(N_MEM, D_MODEL, MEM_INNER),
                    name=nm("d_mn_k"), out_shapes=[_sds((N_MEM, D_MODEL), F32)])
    dmn, = matmul(dvm_b, w("w_mv"), mode="nt", dims=(N_MEM, D_MODEL, MEM_INNER), tiles=(N_MEM, D_MODEL, MEM_INNER),
                  name=nm("d_mn_v"), out_shapes=[_sds((N_MEM, D_MODEL), F32)],
                  epi=lambda acc, other: (acc + other[...],), extras=(dmn_k,),
                  extra_specs=(pl.BlockSpec((N_MEM, D_MODEL), lambda i, j, k: (0, 0)),))
    _, gs["norm_memkv"] = rms_bwd(mem, _vec(sm["norm_memkv"]), dmn, nm("bwd_memkv"))
    dh2, = matmul(dqm, w("w_mq"), mode="nt", dims=(s, D_MODEL, MEM_INNER), tiles=(tm, D_MODEL, MEM_INNER), name=nm("d_h2"),
                  out_shapes=[_sds((s, D_MODEL), F32)])
    dx1, gs["norm_mem_pre"] = rms_bwd(sv["x1"], _vec(sm["norm_mem_pre"]), dh2, nm("bwd_pre_mem"), add=dx2)

    dy1, gs["norm_mix_post"] = rms_bwd(sv["y1"], _vec(sm["norm_mix_post"]), dx1, nm("bwd_post_mix"), out_dtype=BF16)
    g, = matmul(sv["cat"], dy1, mode="tn", dims=(D_MODEL, D_MODEL, s), tiles=(D_MODEL, D_MODEL, tm), name=nm("dw_out"),
                out_shapes=[_sds((D_MODEL, D_MODEL), F32)])
    gw[("w_out", l)] = shards(g)
    du, datt = matmul(dy1, w("w_out"), mode="nt", dims=(s, D_MODEL, D_MODEL), tiles=(tm, D_MODEL, D_MODEL), name=nm("d_cat"),
                      out_shapes=[_sds((s, CONV_CH), F32), _sds((s, FOX_WIDTH), BF16)],
                      out_specs=[pl.BlockSpec((tm, CONV_CH), lambda i, j, k: (i, 0)), pl.BlockSpec((tm, FOX_WIDTH), lambda i, j, k: (i, 0))],
                      epi=lambda acc: (acc[:, :CONV_CH], acc[:, CONV_CH:]))
    dz, csm = conv_bwd(du, sv["u1"], sv["ag"], w("conv_w"), _vec(sm["conv_ln_g"]), _vec(sm["conv_ln_b"]), nm("conv_bwd"),
                       comms=_hosted(hosts, nm("conv_bwd")))
    gs["conv_b"], gs["conv_ln_g"], gs["conv_ln_b"] = csm[32:33], csm[33:34], csm[34:35]
    dz, dcol4, drow4 = fox_bwd(sv["qkv"], sv["cum4"], sv["cat"], datt, sv["lse"], dz, nm("fox_bwd"),
                               comms=_hosted(hosts, nm("fox_bwd")))
    dfl_t, dbf = fox_gate_bwd(dcol4.reshape(FOX_HEADS, s), drow4.reshape(FOX_HEADS, s), sv["fl_t"],
                              sm["b_forget"].reshape(FOX_HEADS, 1), nm("fox_gate_bwd"))
    gs["b_forget"] = dbf.reshape(1, FOX_HEADS)
    dflp = jnp.pad(dfl_t.T.astype(BF16), ((0, 0), (0, 128 - FOX_HEADS)))
    dz = lax.dynamic_update_slice(dz, dflp, (0, GATE_COL))
    g, = matmul(sv["h1"], dz, mode="tn", dims=(D_MODEL, IN_PAD, s), tiles=(512, IN_PAD, tm), name=nm("dw_in"),
                out_shapes=[_sds((D_MODEL, IN_PAD), F32)])
    gw[("w_in", l)] = col_shards(_in_cols_back(g))
    dh1, = matmul(dz, w("w_in"), mode="nt", dims=(s, D_MODEL, IN_PAD), tiles=(tm, D_MODEL, IN_PAD), name=nm("d_h1"),
                  out_shapes=[_sds((s, D_MODEL), F32)], comms=_hosted(hosts, nm("d_h1")))
    dx0, gs["norm_mix_pre"] = rms_bwd(sv["x0"], _vec(sm["norm_mix_pre"]), dh1, nm("bwd_pre_mix"), add=dx1,
                                      comms=_hosted(hosts, nm("bwd_pre_mix")))
    return dx0, gs, csm[:CONV_WIDTH]


def _in_cols(w):
    c2, fw = 2 * CONV_CH, FOX_WIDTH
    parts = [w[:, :c2], w[:, c2 + 3 * fw:], jnp.zeros((w.shape[0], 128 - FOX_HEADS), w.dtype)]
    for p in range(4):
        parts += [w[:, c2 + part * fw + 128 * p:c2 + part * fw + 128 * (p + 1)] for part in range(3)]
    return jnp.concatenate(parts, axis=1)


def _in_cols_back(g):
    c2 = 2 * CONV_CH
    qkv = [[g[:, QKV_COL + 384 * p + 128 * part:QKV_COL + 384 * p + 128 * (part + 1)] for p in range(4)] for part in range(3)]
    return jnp.concatenate([g[:, :c2]] + [blk for part in qkv for blk in part] + [g[:, c2:c2 + FOX_HEADS]], axis=1)


def pack_small_grads(gss, gconv):
    both = lambda f: jnp.concatenate([f(0), f(1)], axis=0)
    rows = [both(lambda l, nm=nm: gss[l][nm]) for nm in NORMS]
    rows.append(both(lambda l: jnp.concatenate([gss[l]["conv_b"], gss[l]["conv_ln_g"]], axis=1)))
    rows.append(both(lambda l: jnp.concatenate([gss[l]["conv_ln_b"], gss[l]["b_forget"],
                                                jnp.zeros((1, D_MODEL - CONV_CH - FOX_HEADS), F32)], axis=1)))
    rows += [jnp.zeros((6, D_MODEL), F32), jnp.concatenate([gconv[0], gconv[1]], axis=1), jnp.zeros((1, D_MODEL), F32)]
    return jnp.concatenate(rows, axis=0)


def unpack_small_grads(p):
    out = {nm: p[2 * idx:2 * idx + 2] for idx, nm in enumerate(NORMS)}
    out["conv_b"], out["conv_ln_g"] = p[14:16, :CONV_CH], p[14:16, CONV_CH:]
    out["conv_ln_b"], out["b_forget"] = p[16:18, :CONV_CH], p[16:18, CONV_CH:CONV_CH + FOX_HEADS]
    out["conv_w"] = jnp.stack([p[24:24 + CONV_WIDTH, :CONV_CH], p[24:24 + CONV_WIDTH, CONV_CH:]])
    return out


def kernel(x, mem, norm_mix_pre, norm_mix_post, w_in, b_forget, conv_w, conv_b, conv_ln_g, conv_ln_b, w_out, norm_mem_pre, norm_mem_post, norm_memkv, w_mq, w_mk, w_mv, w_mo, norm_mlp_pre, norm_mlp_post, w_up, w_down, loss_target, m_norm_mix_pre, m_norm_mix_post, m_w_in, m_b_forget, m_conv_w, m_conv_b, m_conv_ln_g, m_conv_ln_b, m_w_out, m_norm_mem_pre, m_norm_mem_post, m_norm_memkv, m_w_mq, m_w_mk, m_w_mv, m_w_mo, m_norm_mlp_pre, m_norm_mlp_post, m_w_up, m_w_down, v_norm_mix_pre, v_norm_mix_post, v_w_in, v_b_forget, v_conv_w, v_conv_b, v_conv_ln_g, v_conv_ln_b, v_w_out, v_norm_mem_pre, v_norm_mem_post, v_norm_memkv, v_w_mq, v_w_mk, v_w_mv, v_w_mo, v_norm_mlp_pre, v_norm_mlp_post, v_w_up, v_w_down):
    args = dict(locals())
    wts = {n: args[n] for n in WEIGHTS}
    mom = {n: args["m_" + n] for n in WEIGHTS}
    var = {n: args["v_" + n] for n in WEIGHTS}
    own = 2 * lax.axis_index("x") + lax.axis_index("y")
    place = jnp.stack([lax.axis_index("c"), own]).astype(jnp.int32)
    items = lambda group, l: [(n, l) for n in group]

    shard = {(n, l): wts[n][l].astype(BF16) for n in BIG for l in range(2)}
    shard[("conv_w", 0)] = wts["conv_w"]
    gath, full = {}, {}

    def gather_stage(keys):
        def done(res):
            for key, buf in zip(keys, res):
                gath[key] = lax.dynamic_update_slice(buf, shard[key][None], (own,) + (0,) * shard[key].ndim)
        return gather_ici([shard[k] for k in keys], done)

    def forward_stage(keys):
        def done(res):
            gath.update(zip(keys, res))
        return gather_forward([gath[k] for k in keys], done)

    def weight(l):
        def get(n):
            if (n, l) not in full:
                if n == "conv_w":
                    g = gath[("conv_w", 0)][:, l]
                    full[(n, l)] = jnp.pad(jnp.moveaxis(g, 0, 1).reshape(CONV_WIDTH, CONV_CH), ((0, CONV_HALO - CONV_WIDTH), (0, 0)))
                elif n == "w_in":
                    g = gath[(n, l)]
                    full[(n, l)] = _in_cols(jnp.moveaxis(g, 0, 1).reshape(D_MODEL, IN_COLS))
                elif n == "w_mo":
                    g = gath[(n, l)]
                    full[(n, l)] = jnp.moveaxis(g, 0, 1).reshape(MEM_INNER, D_MODEL)
                elif n == "w_up":
                    full[(n, l)] = gath[(n, l)]
                else:
                    g = gath[(n, l)]
                    full[(n, l)] = g.reshape(N_CHIPS * g.shape[1], g.shape[2])
            return full[(n, l)]
        return get

    first = [("w_in", 0), ("conv_w", 0)]
    run_comm([gather_stage(first)], "gather_first")
    run_comm([forward_stage(first)], "forward_first")
    mem0w, mlp0w = items(GROUP_B, 0), items(GROUP_C, 0)
    ab1 = items(GROUP_A, 1) + items(GROUP_B, 1)
    fwd_hosts = [
        {"conv_fwd0": lambda: [gather_stage([("w_out", 0)])],
         "fox_fwd0": lambda: [gather_stage(mem0w + mlp0w + ab1), forward_stage([("w_out", 0)])],
         "mix_out0": lambda: [forward_stage(mem0w)],
         "mem_q0": lambda: [forward_stage(mlp0w)],
         "mem_o0": lambda: [forward_stage(ab1)],
         "mlp_up0": lambda: [gather_stage([("w_up", 1)])],
         "mlp_down0": lambda: [gather_stage([("w_down", 1)]), forward_stage([("w_up", 1)])]},
        {"conv_fwd1": lambda: [forward_stage([("w_down", 1)])]},
    ]

    gw, swapped, own_sum, wire, recvd, fin = {}, {}, {}, {}, {}, {}

    def swap_stage(keys):
        return swap_halves([gw[k] for k in keys], lambda res: swapped.update(zip(keys, res)))

    def scatter_stage(keys):
        for n, l in keys:
            own_sum[(n, l)], wire[(n, l)] = pair_sum(gw[(n, l)], swapped[(n, l)], place, f"pair_sum_{n}{l}")
        return scatter_partials([wire[k] for k in keys], lambda res: recvd.update(zip(keys, res)))

    def share_stage(keys):
        for n, l in keys:
            fin[(n, l)] = chip_sum(own_sum[(n, l)], recvd[(n, l)], place, f"chip_sum_{n}{l}")
        return share_halves([fin[k] for k in keys], lambda res: fin.update(zip(keys, res)))

    a0, b0, c0 = [("w_in", 0)], items(GROUP_B, 0) + [("w_out", 0)], items(GROUP_C, 0)
    a1, b1, c1 = [("w_in", 1)], items(GROUP_B, 1) + [("w_out", 1)], items(GROUP_C, 1)
    bwd_hosts = [
        {"dw_down0": lambda: [scatter_stage(a1)],
         "d_act0": lambda: [share_stage(a1)],
         "d_h30": lambda: [swap_stage(c0)],
         "conv_bwd0": lambda: [scatter_stage(c0), swap_stage(b0)],
         "fox_bwd0": lambda: [share_stage(c0), scatter_stage(b0)],
         "d_h10": lambda: [share_stage(b0), swap_stage(a0)],
         "bwd_pre_mix0": lambda: [scatter_stage(a0)]},
        {"d_h31": lambda: [swap_stage(c1)],
         "conv_bwd1": lambda: [scatter_stage(c1), swap_stage(b1)],
         "fox_bwd1": lambda: [share_stage(c1), scatter_stage(b1)],
         "d_h11": lambda: [share_stage(b1), swap_stage(a1)]},
    ]

    sml = [{n: wts[n][l] for n in SMALL} for l in range(2)]
    mem0 = mem[0]
    saved, h = [], x[0]
    for l in range(2):
        h, sv = layer_fwd(h, mem0, weight(l), sml[l], l, fwd_hosts[l])
        saved.append(sv)
    sq, dx = loss_head(h, loss_target[0], "loss_head")
    loss = lax.psum(0.5 * jnp.sum(sq) / D_MODEL, ("x", "y", "c"))
    gss, gconv = [None, None], [None, None]
    for l in (1, 0):
        dx, gss[l], gconv[l] = layer_bwd(dx, mem0, weight(l), sml[l], saved[l], l, bwd_hosts[l], gw)

    grads, delta, new_m, new_v = {}, {}, {}, {}
    for n in ("w_up", "w_down", "w_out") + GROUP_B + ("w_in",):
        comms = [share_stage(a0)] if n == "w_up" else ()
        delta[n], new_m[n], new_v[n], grads[n] = adamw_layers(wts[n], mom[n], var[n], fin[(n, 0)], fin[(n, 1)], "adamw_" + n,
                                                              comms=comms)

    g_small = unpack_small_grads(allreduce_small(pack_small_grads(gss, gconv)))
    taps = lambda a: a.reshape(2 * CONV_WIDTH, CONV_CH // N_CHIPS)
    grads["conv_w"] = lax.dynamic_slice_in_dim(g_small["conv_w"], own * (CONV_CH // N_CHIPS), CONV_CH // N_CHIPS, axis=2)
    d_, m_, v_ = adamw(taps(wts["conv_w"]), taps(grads["conv_w"]), taps(mom["conv_w"]), taps(var["conv_w"]), "adamw_conv_w")
    delta["conv_w"], new_m["conv_w"], new_v["conv_w"] = (a.reshape(wts["conv_w"].shape) for a in (d_, m_, v_))
    for n in SMALL:
        grads[n] = g_small[n]
        delta[n], new_m[n], new_v[n] = adamw(wts[n], grads[n], mom[n], var[n], "adamw_" + n)

    return (loss, dx[None], *[grads[n] for n in WEIGHTS], *[delta[n] for n in WEIGHTS],
            *[new_m[n] for n in WEIGHTS], *[new_v[n] for n in WEIGHTS])
```

```python
import jax
import jax.numpy as jnp
from jax import lax
from jax.experimental import pallas as pl
from jax.experimental.pallas import tpu as pltpu

F32, BF16 = jnp.float32, jnp.bfloat16
D_MODEL = 1024
CONV_CH = 512
CONV_WIDTH = 31
CONV_HALO = 32
FOX_WIDTH = 512
FOX_HEADS = 8
N_MEM = 256
MEM_HEADS = 4
MEM_HEAD_DIM = 128
MEM_INNER = 512
D_FF = 4096
IN_COLS = 2568
IN_PAD = 2688
GATE_COL = 2 * CONV_CH
QKV_COL = GATE_COL + 128
EPS = 1e-6
NEG_INF = -1e30
FOX_SCALE = 0.125
MEM_SCALE = MEM_HEAD_DIM ** -0.5
ADAM_LR, ADAM_B1, ADAM_B2, ADAM_EPS, ADAM_WD, ADAM_STEP = 0.001, 0.9, 0.999, 1e-08, 0.01, 10
VMEM_LIMIT = 56 * 1024 * 1024
ROW_TILE = 512
MESH = pl.DeviceIdType.MESH
N_CHIPS = 4

GROUP_A = ("w_in", "w_out")
GROUP_B = ("w_mq", "w_mk", "w_mv", "w_mo")
GROUP_C = ("w_up", "w_down")
BIG = GROUP_A + GROUP_B + GROUP_C
NORMS = ("norm_mix_pre", "norm_mix_post", "norm_mem_pre", "norm_mem_post", "norm_memkv", "norm_mlp_pre", "norm_mlp_post")
SMALL = NORMS + ("conv_b", "conv_ln_g", "conv_ln_b", "b_forget")
WEIGHTS = ("norm_mix_pre", "norm_mix_post", "w_in", "b_forget", "conv_w", "conv_b", "conv_ln_g", "conv_ln_b", "w_out",
           "norm_mem_pre", "norm_mem_post", "norm_memkv", "w_mq", "w_mk", "w_mv", "w_mo", "norm_mlp_pre", "norm_mlp_post",
           "w_up", "w_down")

ANY = pl.BlockSpec(memory_space=pl.ANY)


def _sds(shape, dtype):
    return jax.ShapeDtypeStruct(shape, dtype)


class Comm:
    def __init__(self, ins, out_shapes, n_sems, start, finish, aliases=(), on_done=None):
        self.ins, self.out_shapes, self.n_sems = list(ins), list(out_shapes), n_sems
        self.start, self.finish, self.aliases, self.on_done = start, finish, tuple(aliases), on_done


def _call(body, *, name, grid, in_specs, out_specs, out_shape, args, scratch=(), sem=(), comms=(), aliases=None):
    comms = list(comms)
    n_in, n_out, n_scr = len(in_specs), len(out_specs), len(scratch)
    io_alias = dict(aliases or {})
    c_args, c_shapes, c_scratch = [], [], []
    for cm in comms:
        for i, o in cm.aliases:
            io_alias[n_in + len(c_args) + i] = n_out + len(c_shapes) + o
        c_args += cm.ins
        c_shapes += cm.out_shapes
        c_scratch += [pltpu.SemaphoreType.DMA((cm.n_sems,)), pltpu.SemaphoreType.DMA((cm.n_sems,))]
    rank = len(grid)

    def hosted(*refs):
        ins, c_in = refs[:n_in], refs[n_in:n_in + len(c_args)]
        outs = refs[n_in + len(c_args):n_in + len(c_args) + n_out]
        c_out = refs[n_in + len(c_args) + n_out:n_in + len(c_args) + n_out + len(c_shapes)]
        scr = refs[len(refs) - n_scr - 2 * len(comms):len(refs) - 2 * len(comms)]
        sems = refs[len(refs) - 2 * len(comms):]
        first = last = None
        for d in range(rank):
            f, e = pl.program_id(d) == 0, pl.program_id(d) == grid[d] - 1
            first = f if first is None else jnp.logical_and(first, f)
            last = e if last is None else jnp.logical_and(last, e)

        def each(which):
            a = b = 0
            for k, cm in enumerate(comms):
                getattr(cm, which)(c_in[a:a + len(cm.ins)], c_out[b:b + len(cm.out_shapes)], sems[2 * k], sems[2 * k + 1])
                a, b = a + len(cm.ins), b + len(cm.out_shapes)

        if rank:
            pl.when(first)(lambda: each("start"))
        else:
            each("start")
        body(*ins, *outs, *scr)
        if rank:
            pl.when(last)(lambda: each("finish"))
        else:
            each("finish")

    if comms:
        sem = ("arbitrary",) * rank
    params = pltpu.CompilerParams(dimension_semantics=sem, vmem_limit_bytes=VMEM_LIMIT) if rank else \
        pltpu.CompilerParams(vmem_limit_bytes=VMEM_LIMIT)
    kw = dict(grid=grid) if rank else {}
    res = pl.pallas_call(
        hosted if comms else body, name=name, in_specs=list(in_specs) + [ANY] * len(c_args),
        out_specs=list(out_specs) + [ANY] * len(c_shapes), out_shape=list(out_shape) + c_shapes,
        scratch_shapes=list(scratch) + c_scratch, input_output_aliases=io_alias, compiler_params=params, **kw)(*args, *c_args)
    base, rest = list(res[:n_out]), list(res[n_out:])
    for cm in comms:
        got, rest = rest[:len(cm.out_shapes)], rest[len(cm.out_shapes):]
        if cm.on_done is not None:
            cm.on_done(got)
    return base


def run_comm(comms, name):
    _call(lambda: None, name=name, grid=(), in_specs=[], out_specs=[], out_shape=[], args=[], comms=comms)


def _place():
    x, y, c = lax.axis_index("x"), lax.axis_index("y"), lax.axis_index("c")
    chips = [(1 - x, y), (x, 1 - y), (1 - x, 1 - y)]
    return x, y, c, chips


def _half(ref, h, lead=()):
    n = ref.shape[len(lead)] // 2
    return ref.at[(*lead, pl.ds(h * n, n))]


def _remote(src, dst, send_sems, recv_sems, k, to):
    return pltpu.make_async_remote_copy(src_ref=src, dst_ref=dst, send_sem=send_sems.at[k], recv_sem=recv_sems.at[k],
                                        device_id=to, device_id_type=MESH)


def gather_ici(shards, on_done):
    n = len(shards)

    def start(ins, outs, ss, rs):
        x, y, c, chips = _place()
        own = 2 * x + y
        for t in range(n):
            for k, chip in enumerate(chips):
                _remote(_half(ins[t], c), _half(outs[t], c, (own,)), ss, rs, 3 * t + k, (*chip, c)).start()

    def finish(ins, outs, ss, rs):
        x, y, c, chips = _place()
        for t in range(n):
            for k, (px, py) in enumerate(chips):
                cp = _remote(_half(ins[t], c), _half(outs[t], c, (2 * px + py,)), ss, rs, 3 * t + k, (px, py, c))
                cp.wait_recv()
                cp.wait_send()

    return Comm(shards, [_sds((N_CHIPS,) + a.shape, a.dtype) for a in shards], 3 * n, start, finish, on_done=on_done)


def gather_forward(bufs, on_done):
    n = len(bufs)

    def start(ins, outs, ss, rs):
        x, y, c, chips = _place()
        for t in range(n):
            for k, (px, py) in enumerate(chips):
                blk = _half(outs[t], c, (2 * px + py,))
                _remote(blk, blk, ss, rs, 3 * t + k, (x, y, 1 - c)).start()

    def finish(ins, outs, ss, rs):
        x, y, c, chips = _place()
        for t in range(n):
            for k, (px, py) in enumerate(chips):
                cp = _remote(_half(outs[t], c, (2 * px + py,)), _half(outs[t], 1 - c, (2 * px + py,)), ss, rs, 3 * t + k, (x, y, 1 - c))
                cp.wait_recv()
                cp.wait_send()

    return Comm(bufs, [_sds(b.shape, b.dtype) for b in bufs], 3 * n, start, finish,
                aliases=[(t, t) for t in range(n)], on_done=on_done)


def swap_halves(grads, on_done):
    n = len(grads)

    def copies(ins, outs, ss, rs):
        x, y, c, _ = _place()
        out = []
        for t in range(n):
            for k in range(N_CHIPS):
                out.append(_remote(_half(ins[t], 1 - c, (k,)), outs[t].at[k], ss, rs, N_CHIPS * t + k, (x, y, 1 - c)))
        return out

    def start(ins, outs, ss, rs):
        for cp in copies(ins, outs, ss, rs):
            cp.start()

    def finish(ins, outs, ss, rs):
        for cp in copies(ins, outs, ss, rs):
            cp.wait()

    return Comm(grads, [_sds((N_CHIPS, g.shape[1] // 2, g.shape[2]), g.dtype) for g in grads], N_CHIPS * n, start, finish,
                on_done=on_done)


def scatter_partials(parts, on_done):
    n = len(parts)

    def start(ins, outs, ss, rs):
        x, y, c, chips = _place()
        own = 2 * x + y
        for t in range(n):
            for k, (px, py) in enumerate(chips):
                _remote(ins[t].at[2 * px + py], outs[t].at[own], ss, rs, 3 * t + k, (px, py, c)).start()

    def finish(ins, outs, ss, rs):
        x, y, c, chips = _place()
        own = 2 * x + y
        for t in range(n):
            for k, (px, py) in enumerate(chips):
                cp = _remote(ins[t].at[own], outs[t].at[2 * px + py], ss, rs, 3 * t + k, (px, py, c))
                cp.wait_recv()
                cp.wait_send()

    return Comm(parts, [_sds(p.shape, p.dtype) for p in parts], 3 * n, start, finish, on_done=on_done)


def share_halves(fins, on_done):
    n = len(fins)

    def copies(outs, ss, rs, c, to):
        return [_remote(_half(outs[t], c), _half(outs[t], c), ss, rs, t, to) for t in range(n)]

    def start(ins, outs, ss, rs):
        x, y, c, _ = _place()
        for cp in copies(outs, ss, rs, c, (x, y, 1 - c)):
            cp.start()

    def finish(ins, outs, ss, rs):
        x, y, c, _ = _place()
        for t in range(n):
            cp = _remote(_half(outs[t], c), _half(outs[t], 1 - c), ss, rs, t, (x, y, 1 - c))
            cp.wait_recv()
            cp.wait_send()

    return Comm(fins, [_sds(f.shape, f.dtype) for f in fins], n, start, finish, aliases=[(t, t) for t in range(n)], on_done=on_done)


def pair_sum(g, other, place, name):
    _, r, c = g.shape
    h = r // 2
    tm = ROW_TILE if h % ROW_TILE == 0 else h
    nb = h // tm

    def body(pl_ref, g_ref, o_ref, own_ref, wire_ref):
        k = pl.program_id(1)
        val = g_ref[...] + o_ref[...]
        wire_ref[...] = val.astype(BF16)

        @pl.when(k == pl_ref[1])
        def _():
            own_ref[...] = val

    return pl.pallas_call(
        body, name=name,
        grid_spec=pltpu.PrefetchScalarGridSpec(
            num_scalar_prefetch=1, grid=(nb, N_CHIPS),
            in_specs=[pl.BlockSpec((None, tm, c), lambda i, k, p: (k, p[0] * nb + i, 0)),
                      pl.BlockSpec((None, tm, c), lambda i, k, p: (k, i, 0))],
            out_specs=[pl.BlockSpec((tm, c), lambda i, k, p: (i, 0)),
                       pl.BlockSpec((None, tm, c), lambda i, k, p: (k, i, 0))]),
        out_shape=[_sds((h, c), F32), _sds((N_CHIPS, h, c), BF16)],
        compiler_params=pltpu.CompilerParams(dimension_semantics=("parallel", "arbitrary"), vmem_limit_bytes=VMEM_LIMIT))(place, g, other)


def chip_sum(own, recv, place, name):
    h, c = own.shape
    tm = ROW_TILE if h % ROW_TILE == 0 else h
    nb = h // tm

    def body(pl_ref, own_ref, a_ref, b_ref, c_ref, out_ref):
        out_ref[...] = ((own_ref[...] + a_ref[...].astype(F32)) + b_ref[...].astype(F32)) + c_ref[...].astype(F32)

    slot = lambda d: pl.BlockSpec((None, tm, c), lambda i, p: ((p[1] + d) % N_CHIPS, i, 0))
    return pl.pallas_call(
        body, name=name,
        grid_spec=pltpu.PrefetchScalarGridSpec(
            num_scalar_prefetch=1, grid=(nb,),
            in_specs=[pl.BlockSpec((tm, c), lambda i, p: (i, 0)), slot(1), slot(2), slot(3)],
            out_specs=pl.BlockSpec((tm, c), lambda i, p: (p[0] * nb + i, 0))),
        out_shape=_sds((2 * h, c), F32),
        compiler_params=pltpu.CompilerParams(dimension_semantics=("parallel",), vmem_limit_bytes=VMEM_LIMIT))(place, own, recv, recv, recv)


def allreduce_small(packed, comms=()):
    rows, cols = packed.shape

    def body(in_ref, out_ref, buf_ref, send_sems, recv_sems):
        x, y, c, _ = _place()
        me = 4 * x + 2 * y + c
        buf_ref[me] = in_ref[...]
        peers = [(x ^ (k >> 2), y ^ ((k >> 1) & 1), c ^ (k & 1)) for k in range(1, 8)]
        cps = [_remote(in_ref, buf_ref.at[me], send_sems, recv_sems, k, peer) for k, peer in enumerate(peers)]
        for cp in cps:
            cp.start()
        for k, (px, py, pc) in enumerate(peers):
            _remote(in_ref, buf_ref.at[4 * px + 2 * py + pc], send_sems, recv_sems, k, (px, py, pc)).wait_recv()
        for cp in cps:
            cp.wait_send()
        total = buf_ref[0]
        for d in range(1, 8):
            total = total + buf_ref[d]
        out_ref[...] = total

    vm = pl.BlockSpec(memory_space=pltpu.VMEM)
    return _call(body, name="allreduce_small", grid=(), in_specs=[vm], out_specs=[vm], out_shape=[_sds((rows, cols), F32)],
                 args=[packed], comms=comms,
                 scratch=[pltpu.VMEM((8, rows, cols), F32), pltpu.SemaphoreType.DMA((7,)), pltpu.SemaphoreType.DMA((7,))])[0]


def rms_fwd(x, g, name):
    m, d = x.shape
    tm = min(m, ROW_TILE)

    def body(x_ref, g_ref, h_ref):
        xv = x_ref[...]
        r = lax.rsqrt(jnp.mean(xv * xv, axis=-1, keepdims=True) + EPS)
        h_ref[...] = ((xv * r) * g_ref[...]).astype(BF16)

    return _call(body, name=name, grid=(m // tm,),
                 in_specs=[pl.BlockSpec((tm, d), lambda i: (i, 0)), pl.BlockSpec((1, d), lambda i: (0, 0))],
                 out_specs=[pl.BlockSpec((tm, d), lambda i: (i, 0))], out_shape=[_sds((m, d), BF16)],
                 args=[x, g], sem=("parallel",))[0]


def resid_post(x, y, g, name):
    m, d = x.shape
    tm = min(m, ROW_TILE)

    def body(x_ref, y_ref, g_ref, o_ref):
        yv = y_ref[...]
        r = lax.rsqrt(jnp.mean(yv * yv, axis=-1, keepdims=True) + EPS)
        o_ref[...] = x_ref[...] + (yv * r) * g_ref[...]

    row = pl.BlockSpec((tm, d), lambda i: (i, 0))
    return _call(body, name=name, grid=(m // tm,), in_specs=[row, row, pl.BlockSpec((1, d), lambda i: (0, 0))],
                 out_specs=[row], out_shape=[_sds((m, d), F32)], args=[x, y, g], sem=("parallel",))[0]


def rms_bwd(y, g, dout, name, add=None, out_dtype=F32, comms=()):
    m, d = y.shape
    tm = min(m, ROW_TILE)
    has_add = add is not None

    def body(*refs):
        y_ref, g_ref, d_ref = refs[:3]
        dy_ref, dg_ref = refs[-2:]
        i = pl.program_id(0)
        yv = y_ref[...]
        dv = d_ref[...].astype(F32)
        r = lax.rsqrt(jnp.mean(yv * yv, axis=-1, keepdims=True) + EPS)
        gy = g_ref[...] * dv
        t = jnp.mean(yv * gy, axis=-1, keepdims=True) * (r * r)
        dy = r * (gy - yv * t)
        if has_add:
            dy = dy + refs[3][...]
        dy_ref[...] = dy.astype(dy_ref.dtype)
        part = jnp.sum(dv * (yv * r), axis=0, keepdims=True)

        @pl.when(i == 0)
        def _():
            dg_ref[...] = part

        @pl.when(i > 0)
        def _():
            dg_ref[...] += part

    row = pl.BlockSpec((tm, d), lambda i: (i, 0))
    vec = pl.BlockSpec((1, d), lambda i: (0, 0))
    return _call(body, name=name, grid=(m // tm,), in_specs=[row, vec, row] + ([row] if has_add else []),
                 out_specs=[row, vec], out_shape=[_sds((m, d), out_dtype), _sds((1, d), F32)],
                 args=[y, g, dout] + ([add] if has_add else []), sem=("arbitrary",), comms=comms)


def loss_head(xf, tgt, name):
    m, d = xf.shape
    tm = min(m, ROW_TILE)

    def body(x_ref, t_ref, s_ref, dx_ref):
        i = pl.program_id(0)
        err = x_ref[...] - t_ref[...]
        dx_ref[...] = err * (1.0 / d)
        part = jnp.sum(err * err, axis=0, keepdims=True)

        @pl.when(i == 0)
        def _():
            s_ref[...] = part

        @pl.when(i > 0)
        def _():
            s_ref[...] += part

    row = pl.BlockSpec((tm, d), lambda i: (i, 0))
    vec = pl.BlockSpec((1, d), lambda i: (0, 0))
    return _call(body, name=name, grid=(m // tm,), in_specs=[row, row], out_specs=[vec, row],
                 out_shape=[_sds((1, d), F32), _sds((m, d), F32)], args=[xf, tgt], sem=("arbitrary",))


_DOT_DIMS = {"nn": (((1,), (0,)), ((), ())), "nt": (((1,), (1,)), ((), ())), "tn": (((0,), (0,)), ((), ()))}


def matmul(a, b, *, mode, dims, tiles, name, out_shapes, out_specs=None, epi=None, extras=(), extra_specs=(),
           a_spec=None, b_spec=None, comms=(), aliases=None, store=None):
    m, n, k = dims
    tm, tn, tk = tiles
    nk = k // tk
    assert m % tm == 0 and n % tn == 0 and k % tk == 0
    n_ex, n_out = len(extras), len(out_shapes)
    if a_spec is None:
        a_spec = pl.BlockSpec((tk, tm), lambda i, j, kk: (kk, i)) if mode == "tn" else pl.BlockSpec((tm, tk), lambda i, j, kk: (i, kk))
    if b_spec is None:
        b_spec = pl.BlockSpec((tn, tk), lambda i, j, kk: (j, kk)) if mode == "nt" else pl.BlockSpec((tk, tn), lambda i, j, kk: (kk, j))
    if out_specs is None:
        out_specs = [pl.BlockSpec((tm, tn), lambda i, j, kk: (i, j)) for _ in out_shapes]
    if epi is None:
        epi = lambda acc: (acc,)

    def body(*refs):
        a_ref, b_ref = refs[0], refs[1]
        ex = refs[2:2 + n_ex]
        outs = refs[2 + n_ex:2 + n_ex + n_out]
        part = lax.dot_general(a_ref[...], b_ref[...], _DOT_DIMS[mode], preferred_element_type=F32)

        def finish(acc):
            for o_ref, val in zip(outs, epi(acc, *ex)):
                if store is None:
                    o_ref[...] = val.astype(o_ref.dtype)
                else:
                    store(o_ref, val)

        if nk == 1:
            finish(part)
        else:
            acc_ref = refs[-1]
            kk = pl.program_id(2)

            @pl.when(kk == 0)
            def _():
                acc_ref[...] = part

            @pl.when(kk > 0)
            def _():
                acc_ref[...] += part

            @pl.when(kk == nk - 1)
            def _():
                finish(acc_ref[...])

    return _call(body, name=name, grid=(m // tm, n // tn, nk), in_specs=[a_spec, b_spec] + list(extra_specs),
                 out_specs=list(out_specs), out_shape=list(out_shapes), args=[a, b, *extras],
                 scratch=[pltpu.VMEM((tm, tn), F32)] if nk > 1 else [], sem=("parallel", "parallel", "arbitrary"),
                 comms=comms, aliases=aliases)


def _sigmoid(v):
    return 1.0 / (1.0 + jnp.exp(-v))


def _shifted_copies(src_ref, dst_ref, rows):
    n = rows + CONV_HALO - 8
    for b in range(1, 8):
        dst_ref[b - 1, 0:n, :] = src_ref[b:b + n, :]


def _tap(src_ref, sh_ref, off, rows):
    lo = (off // 8) * 8
    return src_ref[lo:lo + rows, :] if off % 8 == 0 else sh_ref[off % 8 - 1, lo:lo + rows, :]


def conv_fwd(ag, cw, cb, lg, lb, name, comms=()):
    s = ag.shape[0]
    tm = min(s, ROW_TILE)
    c = CONV_CH
    hb = tm // CONV_HALO

    def body(ag_ref, halo_ref, w_ref, cb_ref, lg_ref, lb_ref, u1_ref, u_ref, ext_ref, sh_ref):
        i = pl.program_id(0)
        u0 = ag_ref[:, :c] * _sigmoid(ag_ref[:, c:])
        h0 = halo_ref[:, :c] * _sigmoid(halo_ref[:, c:])
        ext_ref[0:CONV_HALO, :] = jnp.where(i == 0, 0.0, h0)
        ext_ref[CONV_HALO:, :] = u0
        _shifted_copies(ext_ref, sh_ref, tm)
        acc = jnp.zeros((tm, c), F32) + cb_ref[...]
        for k in range(CONV_WIDTH):
            off = CONV_HALO - (CONV_WIDTH - 1) + k
            acc = acc + w_ref[k:k + 1, :] * _tap(ext_ref, sh_ref, off, tm)
        u1_ref[...] = acc
        mu = jnp.mean(acc, axis=-1, keepdims=True)
        xc = acc - mu
        rstd = lax.rsqrt(jnp.mean(xc * xc, axis=-1, keepdims=True) + EPS)
        u2 = (xc * rstd) * lg_ref[...] + lb_ref[...]
        u_ref[...] = (u2 * _sigmoid(u2)).astype(BF16)

    vec = pl.BlockSpec((1, c), lambda i: (0, 0))
    return _call(body, name=name, grid=(s // tm,),
                 in_specs=[pl.BlockSpec((tm, 2 * c), lambda i: (i, 0)),
                           pl.BlockSpec((CONV_HALO, 2 * c), lambda i: (jnp.maximum(i * hb - 1, 0), 0)),
                           pl.BlockSpec((CONV_HALO, c), lambda i: (0, 0)), vec, vec, vec],
                 out_specs=[pl.BlockSpec((tm, c), lambda i: (i, 0)), pl.BlockSpec((tm, c), lambda i: (i, 0))],
                 out_shape=[_sds((s, c), F32), _sds((s, 2 * c), BF16)],
                 args=[ag, ag, cw, cb, lg, lb],
                 scratch=[pltpu.VMEM((tm + CONV_HALO, c), F32), pltpu.VMEM((7, tm + CONV_HALO, c), F32)],
                 sem=("parallel",), comms=comms)


def conv_bwd(du, u1, ag, cw, lg, lb, name, comms=()):
    s = du.shape[0]
    tm = min(s, ROW_TILE)
    c = CONV_CH
    hb = tm // CONV_HALO
    nt = s // tm
    last_halo = s // CONV_HALO - 1

    def ln_silu_bwd(du_v, u1_v, lg_v, lb_v):
        mu = jnp.mean(u1_v, axis=-1, keepdims=True)
        xc = u1_v - mu
        rstd = lax.rsqrt(jnp.mean(xc * xc, axis=-1, keepdims=True) + EPS)
        xh = xc * rstd
        u2 = xh * lg_v + lb_v
        sg = _sigmoid(u2)
        du2 = du_v * (sg * (1.0 + u2 * (1.0 - sg)))
        dxh = du2 * lg_v
        du1 = rstd * (dxh - jnp.mean(dxh, axis=-1, keepdims=True) - xh * jnp.mean(dxh * xh, axis=-1, keepdims=True))
        return du1, du2, xh

    def body(du_ref, dun_ref, u1_ref, u1n_ref, ag_ref, agp_ref, w_ref, lg_ref, lb_ref, dag_ref, sm_ref, ext_ref, dext_ref,
             sh_ref, dsh_ref):
        i = pl.program_id(0)
        lg_v, lb_v = lg_ref[...], lb_ref[...]
        du1, du2, xh = ln_silu_bwd(du_ref[...], u1_ref[...], lg_v, lb_v)
        du1n, _, _ = ln_silu_bwd(dun_ref[...], u1n_ref[...], lg_v, lb_v)
        dext_ref[0:tm, :] = du1
        dext_ref[tm:, :] = jnp.where(i == nt - 1, 0.0, du1n)
        a, g = ag_ref[:, :c], ag_ref[:, c:]
        sg = _sigmoid(g)
        ext_ref[0:CONV_HALO, :] = jnp.where(i == 0, 0.0, agp_ref[:, :c] * _sigmoid(agp_ref[:, c:]))
        ext_ref[CONV_HALO:, :] = a * sg
        _shifted_copies(ext_ref, sh_ref, tm)
        _shifted_copies(dext_ref, dsh_ref, tm)

        @pl.when(i == 0)
        def _():
            sm_ref[...] = jnp.zeros_like(sm_ref)

        du0 = jnp.zeros((tm, c), F32)
        for k in range(CONV_WIDTH):
            back = CONV_WIDTH - 1 - k
            du0 = du0 + w_ref[k:k + 1, :] * _tap(dext_ref, dsh_ref, back, tm)
            off = CONV_HALO - (CONV_WIDTH - 1) + k
            sm_ref[k:k + 1, :] += jnp.sum(du1 * _tap(ext_ref, sh_ref, off, tm), axis=0, keepdims=True)
        sm_ref[32:33, :] += jnp.sum(du1, axis=0, keepdims=True)
        sm_ref[33:34, :] += jnp.sum(du2 * xh, axis=0, keepdims=True)
        sm_ref[34:35, :] += jnp.sum(du2, axis=0, keepdims=True)
        dag_ref[:, :c] = (du0 * sg).astype(BF16)
        dag_ref[:, c:] = (du0 * a * (sg * (1.0 - sg))).astype(BF16)

    vec = pl.BlockSpec((1, c), lambda i: (0, 0))
    tile = pl.BlockSpec((tm, c), lambda i: (i, 0))
    nxt = pl.BlockSpec((CONV_HALO, c), lambda i: (jnp.minimum((i + 1) * hb, last_halo), 0))
    return _call(body, name=name, grid=(nt,),
                 in_specs=[tile, nxt, tile, nxt,
                           pl.BlockSpec((tm, 2 * c), lambda i: (i, 0)),
                           pl.BlockSpec((CONV_HALO, 2 * c), lambda i: (jnp.maximum(i * hb - 1, 0), 0)),
                           pl.BlockSpec((CONV_HALO, c), lambda i: (0, 0)), vec, vec],
                 out_specs=[pl.BlockSpec((tm, 2 * c), lambda i: (i, 0)), pl.BlockSpec((40, c), lambda i: (0, 0))],
                 out_shape=[_sds((s, IN_PAD), BF16), _sds((40, c), F32)],
                 args=[du, du, u1, u1, ag, ag, cw, lg, lb],
                 scratch=[pltpu.VMEM((tm + CONV_HALO, c), F32), pltpu.VMEM((tm + CONV_HALO, c), F32),
                          pltpu.VMEM((7, tm + CONV_HALO, c), F32), pltpu.VMEM((7, tm + CONV_HALO, c), F32)],
                 sem=("arbitrary",), comms=comms)


CUM_BLOCK = 256


def _tri(n, upper):
    r = lax.broadcasted_iota(jnp.int32, (n, n), 0)
    cidx = lax.broadcasted_iota(jnp.int32, (n, n), 1)
    return jnp.where((r <= cidx) if upper else (r >= cidx), 1.0, 0.0).astype(F32)


def fox_gate_fwd(fl_t, bf, name):
    h, s = fl_t.shape
    nb = s // CUM_BLOCK

    def body(fl_ref, bf_ref, cum_ref):
        tri = _tri(CUM_BLOCK, True)
        carry = jnp.zeros((h, 1), F32)
        for b in range(nb):
            v = fl_ref[:, b * CUM_BLOCK:(b + 1) * CUM_BLOCK] + bf_ref[...]
            logf = jnp.minimum(v, 0.0) - jnp.log(1.0 + jnp.exp(-jnp.abs(v)))
            cs = jnp.dot(logf, tri, precision=lax.Precision.HIGHEST, preferred_element_type=F32) + carry
            cum_ref[:, b * CUM_BLOCK:(b + 1) * CUM_BLOCK] = cs
            carry = carry + jnp.sum(logf, axis=-1, keepdims=True)

    return pl.pallas_call(body, name=name, out_shape=_sds((h, s), F32),
                          compiler_params=pltpu.CompilerParams(vmem_limit_bytes=VMEM_LIMIT))(fl_t, bf)


def fox_gate_bwd(dcol_t, drow_t, fl_t, bf, name):
    h, s = fl_t.shape
    nb = s // CUM_BLOCK

    def body(dcol_ref, drow_ref, fl_ref, bf_ref, dfl_ref, dbf_ref):
        tri = _tri(CUM_BLOCK, False)
        carry = jnp.zeros((h, 1), F32)
        dbf = jnp.zeros((h, 1), F32)
        for b in reversed(range(nb)):
            sl = slice(b * CUM_BLOCK, (b + 1) * CUM_BLOCK)
            dcb = dcol_ref[:, sl] + drow_ref[:, sl]
            dlogf = jnp.dot(dcb, tri, precision=lax.Precision.HIGHEST, preferred_element_type=F32) + carry
            carry = carry + jnp.sum(dcb, axis=-1, keepdims=True)
            dfl = dlogf * _sigmoid(-(fl_ref[:, sl] + bf_ref[...]))
            dfl_ref[:, sl] = dfl
            dbf = dbf + jnp.sum(dfl, axis=-1, keepdims=True)
        dbf_ref[...] = dbf

    return pl.pallas_call(body, name=name, out_shape=[_sds((h, s), F32), _sds((h, 1), F32)],
                          compiler_params=pltpu.CompilerParams(vmem_limit_bytes=VMEM_LIMIT))(dcol_t, drow_t, fl_t, bf)


FOX_TILE = 512


def _causal_mask(tq, tk, q0, k0):
    row = lax.broadcasted_iota(jnp.int32, (tq, tk), 0) + q0
    col = lax.broadcasted_iota(jnp.int32, (tq, tk), 1) + k0
    return row >= col


def _fox_bias(c_ref, hh, q0, k0, t):
    c_q = jnp.max(c_ref[hh:hh + 1, pl.ds(q0, 128)], axis=-1, keepdims=True)
    return c_q - c_ref[hh:hh + 1, pl.ds(k0, t)]


def fox_fwd(qkv, cum4, cat, name, comms=()):
    s = qkv.shape[0]
    t = min(s, FOX_TILE)
    nq = s // t
    dn = _DOT_DIMS["nt"]

    def body(q_ref, k_ref, v_ref, c_ref, cat_ref, o_ref, lse_ref):
        i = pl.program_id(1)
        q0 = pl.multiple_of(i * t, t)
        lane = lax.broadcasted_iota(jnp.int32, (t, 128), 1)
        qv = q_ref[...] * FOX_SCALE
        zero = jnp.zeros_like(qv)
        q_h = (jnp.where(lane < 64, qv, zero), jnp.where(lane >= 64, qv, zero))

        def step(j, carry, masked):
            k0 = pl.multiple_of(j * t, t)
            kj = k_ref[pl.ds(k0, t), :]
            vj = v_ref[pl.ds(k0, t), :]
            out = []
            for hh in range(2):
                m_prev, l_prev, acc_prev = carry[hh]
                bias = _fox_bias(c_ref, hh, q0, k0, t)
                sc = lax.dot_general(q_h[hh], kj, dn, preferred_element_type=F32) + bias
                if masked:
                    sc = jnp.where(_causal_mask(t, t, q0, k0), sc, NEG_INF)
                m_new = jnp.maximum(m_prev, jnp.max(sc, axis=-1, keepdims=True))
                alpha = jnp.exp(m_prev - m_new)
                p = jnp.exp(sc - m_new)
                l_new = alpha * l_prev + jnp.sum(p, axis=-1, keepdims=True)
                acc_new = alpha * acc_prev + jnp.dot(p.astype(BF16), vj, preferred_element_type=F32)
                out.append((m_new, l_new, acc_new))
            return tuple(out)

        init = tuple((jnp.full((t, 1), NEG_INF, F32), jnp.zeros((t, 1), F32), jnp.zeros((t, 128), F32)) for _ in range(2))
        carry = lax.fori_loop(0, i, lambda j, cr: step(j, cr, False), init)
        carry = step(i, carry, True)
        (m_a, l_a, acc_a), (m_b, l_b, acc_b) = carry
        o_ref[...] = jnp.where(lane < 64, acc_a / l_a, acc_b / l_b).astype(BF16)
        lse_ref[0] = jnp.broadcast_to(m_a + jnp.log(l_a), (t, 128))
        lse_ref[1] = jnp.broadcast_to(m_b + jnp.log(l_b), (t, 128))

    return _call(body, name=name, grid=(4, nq),
                 in_specs=[pl.BlockSpec((t, 128), lambda p, i: (i, 3 * p)),
                           pl.BlockSpec((s, 128), lambda p, i: (0, 3 * p + 1)),
                           pl.BlockSpec((s, 128), lambda p, i: (0, 3 * p + 2)),
                           pl.BlockSpec((None, 2, s), lambda p, i: (p, 0, 0)), ANY],
                 out_specs=[pl.BlockSpec((t, 128), lambda p, i: (i, 4 + p)),
                            pl.BlockSpec((2, t, 128), lambda p, i: (p, i, 0))],
                 out_shape=[_sds((s, 2 * FOX_WIDTH), BF16), _sds((FOX_HEADS, s, 128), F32)],
                 args=[qkv, qkv, qkv, cum4, cat], sem=("parallel", "arbitrary"), comms=comms, aliases={4: 0})


def fox_bwd(qkv, cum4, cat, datt, lse, dz, name, comms=()):
    s = qkv.shape[0]
    t = min(s, FOX_TILE)
    nk = s // t
    nt_dims, tn_dims = _DOT_DIMS["nt"], _DOT_DIMS["tn"]

    def body(q_ref, k_ref, v_ref, c_ref, o_ref, do_ref, lse_ref, dz_in, dz_ref, dc_ref, dr_ref, dq_acc, dk_acc, dv_acc, dr_acc):
        j = pl.program_id(1)
        k0 = pl.multiple_of(j * t, t)
        lane = lax.broadcasted_iota(jnp.int32, (t, 128), 1)
        lo = lane < 64
        kj, vj = k_ref[...], v_ref[...]
        zero = jnp.zeros_like(kj)
        k_h = (jnp.where(lo, kj, zero), jnp.where(lo, zero, kj))

        @pl.when(j == 0)
        def _():
            dq_acc[...] = jnp.zeros_like(dq_acc)
            dr_acc[...] = jnp.zeros_like(dr_acc)

        dk_acc[...] = jnp.zeros_like(dk_acc)
        dv_acc[...] = jnp.zeros_like(dv_acc)

        def step(i, dc, masked):
            q0 = pl.multiple_of(i * t, t)
            qi = q_ref[pl.ds(q0, t), :]
            doi = do_ref[pl.ds(q0, t), :]
            prod = doi.astype(F32) * o_ref[pl.ds(q0, t), :].astype(F32)
            zq = jnp.zeros_like(qi)
            dq_new = jnp.zeros((t, 128), F32)
            dc_out = []
            for hh in range(2):
                sel = lo if hh == 0 else jnp.logical_not(lo)
                q_m = jnp.where(sel, qi * FOX_SCALE, zq)
                do_m = jnp.where(sel, doi, zq)
                delta = jnp.sum(jnp.where(sel, prod, 0.0), axis=-1, keepdims=True)
                bias = _fox_bias(c_ref, hh, q0, k0, t)
                sc = lax.dot_general(q_m, kj, nt_dims, preferred_element_type=F32) + bias
                if masked:
                    sc = jnp.where(_causal_mask(t, t, q0, k0), sc, NEG_INF)
                lse_t = jnp.tile(lse_ref[hh, pl.ds(q0, t), :], (1, t // 128))
                p = jnp.exp(sc - lse_t)
                dv_acc[hh] += lax.dot_general(p.astype(BF16), doi, tn_dims, preferred_element_type=F32)
                dp = lax.dot_general(do_m, vj, nt_dims, preferred_element_type=F32)
                ds = p * (dp - delta)
                dc_out.append(dc[hh] - jnp.sum(ds, axis=0, keepdims=True))
                dr_acc[hh, pl.ds(q0, t), :] += jnp.sum(ds, axis=-1, keepdims=True)
                ds_b = ds.astype(BF16)
                dq_new = dq_new + jnp.dot(ds_b, k_h[hh], preferred_element_type=F32)
                dk_acc[hh] += lax.dot_general(ds_b, qi, tn_dims, preferred_element_type=F32)
            dq_acc[pl.ds(q0, t), :] += dq_new
            return tuple(dc_out)

        dc = step(j, (jnp.zeros((1, t), F32), jnp.zeros((1, t), F32)), True)
        dc = lax.fori_loop(j + 1, nk, lambda i, cr: step(i, cr, False), dc)
        dz_ref[pl.ds(k0, t), 128:256] = (jnp.where(lo, dk_acc[0], dk_acc[1]) * FOX_SCALE).astype(BF16)
        dz_ref[pl.ds(k0, t), 256:384] = jnp.where(lo, dv_acc[0], dv_acc[1]).astype(BF16)
        dc_ref[0:1, :] = dc[0]
        dc_ref[1:2, :] = dc[1]

        @pl.when(j == nk - 1)
        def _():
            dz_ref[:, 0:128] = (dq_acc[...] * FOX_SCALE).astype(BF16)
            eye = lax.broadcasted_iota(jnp.int32, (t, t), 0) == lax.broadcasted_iota(jnp.int32, (t, t), 1)
            for hh in range(2):
                for b in range(nk):
                    col = dr_acc[hh, b * t:(b + 1) * t, :]
                    dr_ref[hh:hh + 1, b * t:(b + 1) * t] = jnp.sum(jnp.where(eye, col, 0.0), axis=0, keepdims=True)

    return _call(body, name=name, grid=(4, nk),
                 in_specs=[pl.BlockSpec((s, 128), lambda p, j: (0, 3 * p)),
                           pl.BlockSpec((t, 128), lambda p, j: (j, 3 * p + 1)),
                           pl.BlockSpec((t, 128), lambda p, j: (j, 3 * p + 2)),
                           pl.BlockSpec((None, 2, s), lambda p, j: (p, 0, 0)),
                           pl.BlockSpec((s, 128), lambda p, j: (0, 4 + p)),
                           pl.BlockSpec((s, 128), lambda p, j: (0, p)),
                           pl.BlockSpec((2, s, 128), lambda p, j: (p, 0, 0)), ANY],
                 out_specs=[pl.BlockSpec((s, 384), lambda p, j: (0, 3 + p)),
                            pl.BlockSpec((None, 2, t), lambda p, j: (p, 0, j)),
                            pl.BlockSpec((None, 2, s), lambda p, j: (p, 0, 0))],
                 out_shape=[_sds((s, IN_PAD), BF16), _sds((4, 2, s), F32), _sds((4, 2, s), F32)],
                 args=[qkv, qkv, qkv, cum4, cat, datt, lse, dz],
                 scratch=[pltpu.VMEM((s, 128), F32), pltpu.VMEM((2, t, 128), F32), pltpu.VMEM((2, t, 128), F32),
                          pltpu.VMEM((2, s, 1), F32)],
                 sem=("parallel", "arbitrary"), comms=comms, aliases={7: 0})


def _mem_probs(q_h, k_h):
    sc = lax.dot_general(q_h, k_h, _DOT_DIMS["nt"], preferred_element_type=F32) * MEM_SCALE
    e = jnp.exp(sc - jnp.max(sc, axis=-1, keepdims=True))
    return e / jnp.sum(e, axis=-1, keepdims=True)


def mem_attn_fwd(qm, km, vm, name):
    s = qm.shape[0]
    tm = min(s, ROW_TILE)

    def body(q_ref, k_ref, v_ref, o_ref):
        for h in range(MEM_HEADS):
            sl = slice(h * MEM_HEAD_DIM, (h + 1) * MEM_HEAD_DIM)
            p = _mem_probs(q_ref[:, sl], k_ref[:, sl])
            o_ref[:, sl] = jnp.dot(p.astype(BF16), v_ref[:, sl], preferred_element_type=F32).astype(BF16)

    kv = pl.BlockSpec((N_MEM, MEM_INNER), lambda i: (0, 0))
    row = pl.BlockSpec((tm, MEM_INNER), lambda i: (i, 0))
    return _call(body, name=name, grid=(s // tm,), in_specs=[row, kv, kv], out_specs=[row],
                 out_shape=[_sds((s, MEM_INNER), BF16)], args=[qm, km, vm], sem=("parallel",))[0]


def mem_attn_bwd(qm, km, vm, dom, name):
    s = qm.shape[0]
    tm = min(s, ROW_TILE)
    tn_dims = _DOT_DIMS["tn"]

    def body(q_ref, k_ref, v_ref, do_ref, dq_ref, dk_ref, dv_ref):
        i = pl.program_id(0)

        @pl.when(i == 0)
        def _():
            dk_ref[...] = jnp.zeros_like(dk_ref)
            dv_ref[...] = jnp.zeros_like(dv_ref)

        for h in range(MEM_HEADS):
            sl = slice(h * MEM_HEAD_DIM, (h + 1) * MEM_HEAD_DIM)
            q_h, k_h, do_h = q_ref[:, sl], k_ref[:, sl], do_ref[:, sl]
            p = _mem_probs(q_h, k_h)
            dp = lax.dot_general(do_h, v_ref[:, sl], _DOT_DIMS["nt"], preferred_element_type=F32)
            ds = p * (dp - jnp.sum(p * dp, axis=-1, keepdims=True))
            ds_b = (ds * MEM_SCALE).astype(BF16)
            dq_ref[:, sl] = jnp.dot(ds_b, k_h, preferred_element_type=F32).astype(BF16)
            dk_ref[:, sl] += lax.dot_general(ds_b, q_h, tn_dims, preferred_element_type=F32)
            dv_ref[:, sl] += lax.dot_general(p.astype(BF16), do_h, tn_dims, preferred_element_type=F32)

    kv = pl.BlockSpec((N_MEM, MEM_INNER), lambda i: (0, 0))
    row = pl.BlockSpec((tm, MEM_INNER), lambda i: (i, 0))
    return _call(body, name=name, grid=(s // tm,), in_specs=[row, kv, kv, row], out_specs=[row, kv, kv],
                 out_shape=[_sds((s, MEM_INNER), BF16), _sds((N_MEM, MEM_INNER), F32), _sds((N_MEM, MEM_INNER), F32)],
                 args=[qm, km, vm, dom], sem=("arbitrary",))


def _adam_update(w, g, m, v):
    c1 = 1.0 - ADAM_B1 ** ADAM_STEP
    c2 = 1.0 - ADAM_B2 ** ADAM_STEP
    nm = ADAM_B1 * m + (1.0 - ADAM_B1) * g
    nv = ADAM_B2 * v + (1.0 - ADAM_B2) * (g * g)
    return -ADAM_LR * ((nm / c1) / (jnp.sqrt(nv / c2) + ADAM_EPS) + ADAM_WD * w), nm, nv


def adamw(w, g, m, v, name, comms=()):
    r, c = w.shape
    tm = ROW_TILE if r % ROW_TILE == 0 else r

    def body(w_ref, g_ref, m_ref, v_ref, d_ref, nm_ref, nv_ref):
        d_ref[...], nm_ref[...], nv_ref[...] = _adam_update(w_ref[...], g_ref[...], m_ref[...], v_ref[...])

    blk = pl.BlockSpec((tm, c), lambda i: (i, 0))
    return _call(body, name=name, grid=(r // tm,), in_specs=[blk] * 4, out_specs=[blk] * 3,
                 out_shape=[_sds((r, c), F32)] * 3, args=[w, g, m, v], sem=("parallel",), comms=comms)


def adamw_layers(w, m, v, g0, g1, name, comms=()):
    _, r, c = w.shape
    tm = ROW_TILE if r % ROW_TILE == 0 else r

    def body(w_ref, m_ref, v_ref, g0_ref, g1_ref, d_ref, nm_ref, nv_ref, g_ref):
        g = jnp.where(pl.program_id(0) == 0, g0_ref[...], g1_ref[...])
        g_ref[...] = g
        d_ref[...], nm_ref[...], nv_ref[...] = _adam_update(w_ref[...], g, m_ref[...], v_ref[...])

    blk = pl.BlockSpec((None, tm, c), lambda l, i: (l, i, 0))
    return _call(body, name=name, grid=(2, r // tm),
                 in_specs=[blk, blk, blk, pl.BlockSpec((tm, c), lambda l, i: (i * (1 - l), 0)),
                           pl.BlockSpec((tm, c), lambda l, i: (i * l, 0))],
                 out_specs=[blk] * 4, out_shape=[_sds(w.shape, F32)] * 4, args=[w, m, v, g0, g1], sem=("parallel", "parallel"),
                 comms=comms)


def _vec(v):
    return v.reshape(1, -1)


def _hosted(hosts, name):
    make = hosts.get(name)
    return make() if make is not None else ()


def layer_fwd(x0, mem, w, sm, l, hosts):
    s = x0.shape[0]
    tm = min(s, ROW_TILE)
    nm = lambda base: f"{base}{l}"
    h1 = rms_fwd(x0, _vec(sm["norm_mix_pre"]), nm("rms_mix_pre"))
    row = lambda width: pl.BlockSpec((tm, width), lambda i, j, k: (i, 0))
    ag, flp, qkv = matmul(
        h1, w("w_in"), mode="nn", dims=(s, IN_PAD, D_MODEL), tiles=(tm, IN_PAD, D_MODEL), name=nm("mix_in"),
        out_shapes=[_sds((s, GATE_COL), F32), _sds((s, 128), F32), _sds((s, 3 * FOX_WIDTH), BF16)],
        out_specs=[row(GATE_COL), row(128), row(3 * FOX_WIDTH)],
        epi=lambda acc: (acc[:, :GATE_COL], acc[:, GATE_COL:QKV_COL], acc[:, QKV_COL:]))
    u1, cat = conv_fwd(ag, w("conv_w"), _vec(sm["conv_b"]), _vec(sm["conv_ln_g"]), _vec(sm["conv_ln_b"]), nm("conv_fwd"),
                       comms=_hosted(hosts, nm("conv_fwd")))
    fl_t = flp[:, :FOX_HEADS].T
    cum = fox_gate_fwd(fl_t, sm["b_forget"].reshape(FOX_HEADS, 1), nm("fox_gate_fwd"))
    cum4 = cum.reshape(4, 2, s)
    cat, lse = fox_fwd(qkv, cum4, cat, nm("fox_fwd"), comms=_hosted(hosts, nm("fox_fwd")))
    y1, = matmul(cat, w("w_out"), mode="nn", dims=(s, D_MODEL, D_MODEL), tiles=(tm, D_MODEL, D_MODEL), name=nm("mix_out"),
                 out_shapes=[_sds((s, D_MODEL), F32)], comms=_hosted(hosts, nm("mix_out")))
    x1 = resid_post(x0, y1, _vec(sm["norm_mix_post"]), nm("post_mix"))

    h2 = rms_fwd(x1, _vec(sm["norm_mem_pre"]), nm("rms_mem_pre"))
    mn = rms_fwd(mem, _vec(sm["norm_memkv"]), nm("rms_memkv"))
    qm, = matmul(h2, w("w_mq"), mode="nn", dims=(s, MEM_INNER, D_MODEL), tiles=(tm, MEM_INNER, D_MODEL), name=nm("mem_q"),
                 out_shapes=[_sds((s, MEM_INNER), BF16)], comms=_hosted(hosts, nm("mem_q")))
    km, = matmul(mn, w("w_mk"), mode="nn", dims=(N_MEM, MEM_INNER, D_MODEL), tiles=(N_MEM, MEM_INNER, D_MODEL), name=nm("mem_k"),
                 out_shapes=[_sds((N_MEM, MEM_INNER), BF16)])
    vm, = matmul(mn, w("w_mv"), mode="nn", dims=(N_MEM, MEM_INNER, D_MODEL), tiles=(N_MEM, MEM_INNER, D_MODEL), name=nm("mem_v"),
                 out_shapes=[_sds((N_MEM, MEM_INNER), BF16)])
    om = mem_attn_fwd(qm, km, vm, nm("mem_attn_fwd"))
    y2, = matmul(om, w("w_mo"), mode="nn", dims=(s, D_MODEL, MEM_INNER), tiles=(tm, D_MODEL, MEM_INNER), name=nm("mem_o"),
                 out_shapes=[_sds((s, D_MODEL), F32)], comms=_hosted(hosts, nm("mem_o")))
    x2 = resid_post(x1, y2, _vec(sm["norm_mem_post"]), nm("post_mem"))

    h3 = rms_fwd(x2, _vec(sm["norm_mlp_pre"]), nm("rms_mlp_pre"))

    def relu2(acc):
        r = jnp.maximum(acc, 0.0)
        return (r * r,)

    act, = matmul(h3, w("w_up"), mode="nn", dims=(s, D_FF, D_MODEL), tiles=(tm, 1024, D_MODEL), name=nm("mlp_up"),
                  out_shapes=[_sds((s, D_FF), BF16)], epi=relu2,
                  b_spec=pl.BlockSpec((None, D_MODEL, 1024), lambda i, j, k: (j, 0, 0)), comms=_hosted(hosts, nm("mlp_up")))
    y3, = matmul(act, w("w_down"), mode="nn", dims=(s, D_MODEL, D_FF), tiles=(tm, D_MODEL, 1024), name=nm("mlp_down"),
                 out_shapes=[_sds((s, D_MODEL), F32)], comms=_hosted(hosts, nm("mlp_down")))
    x3 = resid_post(x2, y3, _vec(sm["norm_mlp_post"]), nm("post_mlp"))
    saved = dict(x0=x0, h1=h1, ag=ag, qkv=qkv, fl_t=fl_t, u1=u1, cum4=cum4, lse=lse, cat=cat, y1=y1, x1=x1,
                 h2=h2, mn=mn, qm=qm, km=km, vm=vm, om=om, y2=y2, x2=x2, h3=h3, act=act, y3=y3)
    return x3, saved


def layer_bwd(dx3, mem, w, sm, sv, l, hosts, gw):
    s = dx3.shape[0]
    tm = min(s, ROW_TILE)
    nm = lambda base: f"{base}{l}"
    gs = {}
    shards = lambda g: g.reshape(N_CHIPS, g.shape[0] // N_CHIPS, g.shape[1])
    col_shards = lambda g: jnp.moveaxis(g.reshape(g.shape[0], N_CHIPS, g.shape[1] // N_CHIPS), 1, 0)
    dy3, gs["norm_mlp_post"] = rms_bwd(sv["y3"], _vec(sm["norm_mlp_post"]), dx3, nm("bwd_post_mlp"), out_dtype=BF16)
    g, = matmul(sv["act"], dy3, mode="tn", dims=(D_FF, D_MODEL, s), tiles=(1024, D_MODEL, tm), name=nm("dw_down"),
                out_shapes=[_sds((D_FF, D_MODEL), F32)], comms=_hosted(hosts, nm("dw_down")))
    gw[("w_down", l)] = shards(g)

    def dup_epi(acc, act_ref):
        return (acc * (2.0 * jnp.sqrt(act_ref[...].astype(F32))),)

    dup, = matmul(dy3, w("w_down"), mode="nt", dims=(s, D_FF, D_MODEL), tiles=(tm, 1024, D_MODEL), name=nm("d_act"),
                  out_shapes=[_sds((s, D_FF), BF16)], epi=dup_epi, extras=(sv["act"],),
                  extra_specs=(pl.BlockSpec((tm, 1024), lambda i, j, k: (i, j)),), comms=_hosted(hosts, nm("d_act")))
    def store_shards(o_ref, val):
        for k in range(N_CHIPS):
            o_ref[k] = val[:, 1024 * k:1024 * (k + 1)]

    gw[("w_up", l)], = matmul(sv["h3"], dup, mode="tn", dims=(D_MODEL, D_FF, s), tiles=(512, D_FF, tm), name=nm("dw_up"),
                              out_shapes=[_sds((N_CHIPS, D_MODEL, 1024), F32)],
                              out_specs=[pl.BlockSpec((N_CHIPS, 512, 1024), lambda i, j, k: (0, i, 0))], store=store_shards)
    dh3, = matmul(dup, w("w_up"), mode="nt", dims=(s, D_MODEL, D_FF), tiles=(tm, D_MODEL, 1024), name=nm("d_h3"),
                  out_shapes=[_sds((s, D_MODEL), F32)],
                  b_spec=pl.BlockSpec((None, D_MODEL, 1024), lambda i, j, k: (k, 0, 0)), comms=_hosted(hosts, nm("d_h3")))
    dx2, gs["norm_mlp_pre"] = rms_bwd(sv["x2"], _vec(sm["norm_mlp_pre"]), dh3, nm("bwd_pre_mlp"), add=dx3)

    dy2, gs["norm_mem_post"] = rms_bwd(sv["y2"], _vec(sm["norm_mem_post"]), dx2, nm("bwd_post_mem"), out_dtype=BF16)
    g, = matmul(sv["om"], dy2, mode="tn", dims=(MEM_INNER, D_MODEL, s), tiles=(MEM_INNER, D_MODEL, tm), name=nm("dw_mo"),
                out_shapes=[_sds((MEM_INNER, D_MODEL), F32)])
    gw[("w_mo", l)] = col_shards(g)
    dom, = matmul(dy2, w("w_mo"), mode="nt", dims=(s, MEM_INNER, D_MODEL), tiles=(tm, MEM_INNER, D_MODEL), name=nm("d_om"),
                  out_shapes=[_sds((s, MEM_INNER), BF16)])
    dqm, dkm, dvm = mem_attn_bwd(sv["qm"], sv["km"], sv["vm"], dom, nm("mem_attn_bwd"))
    g, = matmul(sv["h2"], dqm, mode="tn", dims=(D_MODEL, MEM_INNER, s), tiles=(D_MODEL, MEM_INNER, tm), name=nm("dw_mq"),
                out_shapes=[_sds((D_MODEL, MEM_INNER), F32)])
    gw[("w_mq", l)] = shards(g)
    dkm_b, dvm_b = dkm.astype(BF16), dvm.astype(BF16)
    g, = matmul(sv["mn"], dkm_b, mode="tn", dims=(D_MODEL, MEM_INNER, N_MEM), tiles=(D_MODEL, MEM_INNER, N_MEM),
                name=nm("dw_mk"), out_shapes=[_sds((D_MODEL, MEM_INNER), F32)])
    gw[("w_mk", l)] = shards(g)
    g, = matmul(sv["mn"], dvm_b, mode="tn", dims=(D_MODEL, MEM_INNER, N_MEM), tiles=(D_MODEL, MEM_INNER, N_MEM),
                name=nm("dw_mv"), out_shapes=[_sds((D_MODEL, MEM_INNER), F32)])
    gw[("w_mv", l)] = shards(g)
    dmn_k, = matmul(dkm_b, w("w_mk"), mode="nt", dims=(N_MEM, D_MODEL, MEM_INNER), tiles=(N_MEM, D_MODEL, MEM_INNER),
                    name=nm("d_mn_k"), out_shapes=[_sds((N_MEM, D_MODEL), F32)])
    dmn, = matmul(dvm_b, w("w_mv"), mode="nt", dims=(N_MEM, D_MODEL, MEM_INNER), tiles=(N_MEM, D_MODEL, MEM_INNER),
                  name=nm("d_mn_v"), out_shapes=[_sds((N_MEM, D_MODEL), F32)],
                  epi=lambda acc, other: (acc + other[...],), extras=(dmn_k,),
                  extra_specs=(pl.BlockSpec((N_MEM, D_MODEL), lambda i, j, k: (0, 0)),))
    _, gs["norm_memkv"] = rms_bwd(mem, _vec(sm["norm_memkv"]), dmn, nm("bwd_memkv"))
    dh2, = matmul(dqm, w("w_mq"), mode="nt", dims=(s, D_MODEL, MEM_INNER), tiles=(tm, D_MODEL, MEM_INNER), name=nm("d_h2"),
                  out_shapes=[_sds((s, D_MODEL), F32)])
    dx1, gs["norm_mem_pre"] = rms_bwd(sv["x1"], _vec(sm["norm_mem_pre"]), dh2, nm("bwd_pre_mem"), add=dx2)

    dy1, gs["norm_mix_post"] = rms_bwd(sv["y1"], _vec(sm["norm_mix_post"]), dx1, nm("bwd_post_mix"), out_dtype=BF16)
    g, = matmul(sv["cat"], dy1, mode="tn", dims=(D_MODEL, D_MODEL, s), tiles=(D_MODEL, D_MODEL, tm), name=nm("dw_out"),
                out_shapes=[_sds((D_MODEL, D_MODEL), F32)])
    gw[("w_out", l)] = shards(g)
    du, datt = matmul(dy1, w("w_out"), mode="nt", dims=(s, D_MODEL, D_MODEL), tiles=(tm, D_MODEL, D_MODEL), name=nm("d_cat"),
                      out_shapes=[_sds((s, CONV_CH), F32), _sds((s, FOX_WIDTH), BF16)],
                      out_specs=[pl.BlockSpec((tm, CONV_CH), lambda i, j, k: (i, 0)), pl.BlockSpec((tm, FOX_WIDTH), lambda i, j, k: (i, 0))],
                      epi=lambda acc: (acc[:, :CONV_CH], acc[:, CONV_CH:]))
    dz, csm = conv_bwd(du, sv["u1"], sv["ag"], w("conv_w"), _vec(sm["conv_ln_g"]), _vec(sm["conv_ln_b"]), nm("conv_bwd"),
                       comms=_hosted(hosts, nm("conv_bwd")))
    gs["conv_b"], gs["conv_ln_g"], gs["conv_ln_b"] = csm[32:33], csm[33:34], csm[34:35]
    dz, dcol4, drow4 = fox_bwd(sv["qkv"], sv["cum4"], sv["cat"], datt, sv["lse"], dz, nm("fox_bwd"),
                               comms=_hosted(hosts, nm("fox_bwd")))
    dfl_t, dbf = fox_gate_bwd(dcol4.reshape(FOX_HEADS, s), drow4.reshape(FOX_HEADS, s), sv["fl_t"],
                              sm["b_forget"].reshape(FOX_HEADS, 1), nm("fox_gate_bwd"))
    gs["b_forget"] = dbf.reshape(1, FOX_HEADS)
    dflp = jnp.pad(dfl_t.T.astype(BF16), ((0, 0), (0, 128 - FOX_HEADS)))
    dz = lax.dynamic_update_slice(dz, dflp, (0, GATE_COL))
    g, = matmul(sv["h1"], dz, mode="tn", dims=(D_MODEL, IN_PAD, s), tiles=(512, IN_PAD, tm), name=nm("dw_in"),
                out_shapes=[_sds((D_MODEL, IN_PAD), F32)])
    gw[("w_in", l)] = col_shards(_in_cols_back(g))
    dh1, = matmul(dz, w("w_in"), mode="nt", dims=(s, D_MODEL, IN_PAD), tiles=(tm, D_MODEL, IN_PAD), name=nm("d_h1"),
                  out_shapes=[_sds((s, D_MODEL), F32)], comms=_hosted(hosts, nm("d_h1")))
    dx0, gs["norm_mix_pre"] = rms_bwd(sv["x0"], _vec(sm["norm_mix_pre"]), dh1, nm("bwd_pre_mix"), add=dx1,
                                      comms=_hosted(hosts, nm("bwd_pre_mix")))
    return dx0, gs, csm[:CONV_WIDTH]


def _in_cols(w):
    c2, fw = 2 * CONV_CH, FOX_WIDTH
    parts = [w[:, :c2], w[:, c2 + 3 * fw:], jnp.zeros((w.shape[0], 128 - FOX_HEADS), w.dtype)]
    for p in range(4):
        parts += [w[:, c2 + part * fw + 128 * p:c2 + part * fw + 128 * (p + 1)] for part in range(3)]
    return jnp.concatenate(parts, axis=1)


def _in_cols_back(g):
    c2 = 2 * CONV_CH
    qkv = [[g[:, QKV_COL + 384 * p + 128 * part:QKV_COL + 384 * p + 128 * (part + 1)] for p in range(4)] for part in range(3)]
    return jnp.concatenate([g[:, :c2]] + [blk for part in qkv for blk in part] + [g[:, c2:c2 + FOX_HEADS]], axis=1)


def pack_small_grads(gss, gconv, loss):
    both = lambda f: jnp.concatenate([f(0), f(1)], axis=0)
    rows = [both(lambda l, nm=nm: gss[l][nm]) for nm in NORMS]
    rows.append(both(lambda l: jnp.concatenate([gss[l]["conv_b"], gss[l]["conv_ln_g"]], axis=1)))
    rows.append(both(lambda l: jnp.concatenate([gss[l]["conv_ln_b"], gss[l]["b_forget"],
                                                jnp.zeros((1, D_MODEL - CONV_CH - FOX_HEADS), F32)], axis=1)))
    loss_row = jnp.concatenate([loss.reshape(1, 1), jnp.zeros((1, D_MODEL - 1), F32)], axis=1)
    rows += [jnp.zeros((5, D_MODEL), F32), loss_row, jnp.concatenate([gconv[0], gconv[1]], axis=1), jnp.zeros((1, D_MODEL), F32)]
    return jnp.concatenate(rows, axis=0)


def unpack_small_grads(p):
    out = {nm: p[2 * idx:2 * idx + 2] for idx, nm in enumerate(NORMS)}
    out["loss"] = p[23, 0]
    out["conv_b"], out["conv_ln_g"] = p[14:16, :CONV_CH], p[14:16, CONV_CH:]
    out["conv_ln_b"], out["b_forget"] = p[16:18, :CONV_CH], p[16:18, CONV_CH:CONV_CH + FOX_HEADS]
    out["conv_w"] = jnp.stack([p[24:24 + CONV_WIDTH, :CONV_CH], p[24:24 + CONV_WIDTH, CONV_CH:]])
    return out


def kernel(x, mem, norm_mix_pre, norm_mix_post, w_in, b_forget, conv_w, conv_b, conv_ln_g, conv_ln_b, w_out, norm_mem_pre, norm_mem_post, norm_memkv, w_mq, w_mk, w_mv, w_mo, norm_mlp_pre, norm_mlp_post, w_up, w_down, loss_target, m_norm_mix_pre, m_norm_mix_post, m_w_in, m_b_forget, m_conv_w, m_conv_b, m_conv_ln_g, m_conv_ln_b, m_w_out, m_norm_mem_pre, m_norm_mem_post, m_norm_memkv, m_w_mq, m_w_mk, m_w_mv, m_w_mo, m_norm_mlp_pre, m_norm_mlp_post, m_w_up, m_w_down, v_norm_mix_pre, v_norm_mix_post, v_w_in, v_b_forget, v_conv_w, v_conv_b, v_conv_ln_g, v_conv_ln_b, v_w_out, v_norm_mem_pre, v_norm_mem_post, v_norm_memkv, v_w_mq, v_w_mk, v_w_mv, v_w_mo, v_norm_mlp_pre, v_norm_mlp_post, v_w_up, v_w_down):
    args = dict(locals())
    wts = {n: args[n] for n in WEIGHTS}
    mom = {n: args["m_" + n] for n in WEIGHTS}
    var = {n: args["v_" + n] for n in WEIGHTS}
    own = 2 * lax.axis_index("x") + lax.axis_index("y")
    place = jnp.stack([lax.axis_index("c"), own]).astype(jnp.int32)
    items = lambda group, l: [(n, l) for n in group]

    shard = {(n, l): wts[n][l].astype(BF16) for n in BIG for l in range(2)}
    shard[("conv_w", 0)] = wts["conv_w"]
    gath, full = {}, {}

    def gather_stage(keys):
        def done(res):
            for key, buf in zip(keys, res):
                gath[key] = lax.dynamic_update_slice(buf, shard[key][None], (own,) + (0,) * shard[key].ndim)
        return gather_ici([shard[k] for k in keys], done)

    def forward_stage(keys):
        def done(res):
            gath.update(zip(keys, res))
        return gather_forward([gath[k] for k in keys], done)

    def weight(l):
        def get(n):
            if (n, l) not in full:
                if n == "conv_w":
                    g = gath[("conv_w", 0)][:, l]
                    full[(n, l)] = jnp.pad(jnp.moveaxis(g, 0, 1).reshape(CONV_WIDTH, CONV_CH), ((0, CONV_HALO - CONV_WIDTH), (0, 0)))
                elif n == "w_in":
                    g = gath[(n, l)]
                    full[(n, l)] = _in_cols(jnp.moveaxis(g, 0, 1).reshape(D_MODEL, IN_COLS))
                elif n == "w_mo":
                    g = gath[(n, l)]
                    full[(n, l)] = jnp.moveaxis(g, 0, 1).reshape(MEM_INNER, D_MODEL)
                elif n == "w_up":
                    full[(n, l)] = gath[(n, l)]
                else:
                    g = gath[(n, l)]
                    full[(n, l)] = g.reshape(N_CHIPS * g.shape[1], g.shape[2])
            return full[(n, l)]
        return get

    first = [("w_in", 0), ("conv_w", 0)]
    run_comm([gather_stage(first)], "gather_first")
    run_comm([forward_stage(first)], "forward_first")
    mem0w, mlp0w = items(GROUP_B, 0), items(GROUP_C, 0)
    ab1 = items(GROUP_A, 1) + items(GROUP_B, 1)
    fwd_hosts = [
        {"conv_fwd0": lambda: [gather_stage([("w_out", 0)])],
         "fox_fwd0": lambda: [gather_stage(mem0w + mlp0w + ab1), forward_stage([("w_out", 0)])],
         "mix_out0": lambda: [forward_stage(mem0w)],
         "mem_q0": lambda: [forward_stage(mlp0w)],
         "mem_o0": lambda: [forward_stage(ab1)],
         "mlp_up0": lambda: [gather_stage([("w_up", 1)])],
         "mlp_down0": lambda: [gather_stage([("w_down", 1)]), forward_stage([("w_up", 1)])]},
        {"conv_fwd1": lambda: [forward_stage([("w_down", 1)])]},
    ]

    gw, swapped, own_sum, wire, recvd, fin = {}, {}, {}, {}, {}, {}

    def swap_stage(keys):
        return swap_halves([gw[k] for k in keys], lambda res: swapped.update(zip(keys, res)))

    def scatter_stage(keys):
        for n, l in keys:
            own_sum[(n, l)], wire[(n, l)] = pair_sum(gw[(n, l)], swapped[(n, l)], place, f"pair_sum_{n}{l}")
        return scatter_partials([wire[k] for k in keys], lambda res: recvd.update(zip(keys, res)))

    def share_stage(keys):
        for n, l in keys:
            fin[(n, l)] = chip_sum(own_sum[(n, l)], recvd[(n, l)], place, f"chip_sum_{n}{l}")
        return share_halves([fin[k] for k in keys], lambda res: fin.update(zip(keys, res)))

    a0, b0, c0 = [("w_in", 0)], items(GROUP_B, 0) + [("w_out", 0)], items(GROUP_C, 0)
    a1, b1, c1 = [("w_in", 1)], items(GROUP_B, 1) + [("w_out", 1)], items(GROUP_C, 1)
    bwd_hosts = [
        {"dw_down0": lambda: [scatter_stage(a1)],
         "d_act0": lambda: [share_stage(a1)],
         "d_h30": lambda: [swap_stage(c0)],
         "conv_bwd0": lambda: [scatter_stage(c0), swap_stage(b0)],
         "fox_bwd0": lambda: [share_stage(c0), scatter_stage(b0)],
         "d_h10": lambda: [share_stage(b0), swap_stage(a0)]},
        {"d_h31": lambda: [swap_stage(c1)],
         "conv_bwd1": lambda: [scatter_stage(c1), swap_stage(b1)],
         "fox_bwd1": lambda: [share_stage(c1), scatter_stage(b1)],
         "d_h11": lambda: [share_stage(b1), swap_stage(a1)]},
    ]

    sml = [{n: wts[n][l] for n in SMALL} for l in range(2)]
    mem0 = mem[0]
    saved, h = [], x[0]
    for l in range(2):
        h, sv = layer_fwd(h, mem0, weight(l), sml[l], l, fwd_hosts[l])
        saved.append(sv)
    sq, dx = loss_head(h, loss_target[0], "loss_head")
    gss, gconv = [None, None], [None, None]
    for l in (1, 0):
        dx, gss[l], gconv[l] = layer_bwd(dx, mem0, weight(l), sml[l], saved[l], l, bwd_hosts[l], gw)

    grads, delta, new_m, new_v = {}, {}, {}, {}
    packed = pack_small_grads(gss, gconv, 0.5 * jnp.sum(sq) / D_MODEL)
    g_small = unpack_small_grads(allreduce_small(packed, comms=[scatter_stage(a0)]))
    loss = g_small["loss"]
    taps = lambda a: a.reshape(2 * CONV_WIDTH, CONV_CH // N_CHIPS)
    grads["conv_w"] = lax.dynamic_slice_in_dim(g_small["conv_w"], own * (CONV_CH // N_CHIPS), CONV_CH // N_CHIPS, axis=2)
    d_, m_, v_ = adamw(taps(wts["conv_w"]), taps(grads["conv_w"]), taps(mom["conv_w"]), taps(var["conv_w"]), "adamw_conv_w",
                       comms=[share_stage(a0)])
    delta["conv_w"], new_m["conv_w"], new_v["conv_w"] = (a.reshape(wts["conv_w"].shape) for a in (d_, m_, v_))
    for n in SMALL:
        grads[n] = g_small[n]
        delta[n], new_m[n], new_v[n] = adamw(wts[n], grads[n], mom[n], var[n], "adamw_" + n)
    for n in BIG:
        delta[n], new_m[n], new_v[n], grads[n] = adamw_layers(wts[n], mom[n], var[n], fin[(n, 0)], fin[(n, 1)], "adamw_" + n)

    return (loss, dx[None], *[grads[n] for n in WEIGHTS], *[delta[n] for n in WEIGHTS],
            *[new_m[n] for n in WEIGHTS], *[new_v[n] for n in WEIGHTS])
```

```python
import jax
import jax.numpy as jnp
from jax import lax
from jax.experimental import pallas as pl
from jax.experimental.pallas import tpu as pltpu

F32, BF16 = jnp.float32, jnp.bfloat16
D_MODEL = 1024
CONV_CH = 512
CONV_WIDTH = 31
CONV_HALO = 32
FOX_WIDTH = 512
FOX_HEADS = 8
N_MEM = 256
MEM_HEADS = 4
MEM_HEAD_DIM = 128
MEM_INNER = 512
D_FF = 4096
IN_COLS = 2568
IN_PAD = 2688
GATE_COL = 2 * CONV_CH
QKV_COL = GATE_COL + 128
EPS = 1e-6
NEG_INF = -1e30
FOX_SCALE = 0.125
MEM_SCALE = MEM_HEAD_DIM ** -0.5
ADAM_LR, ADAM_B1, ADAM_B2, ADAM_EPS, ADAM_WD, ADAM_STEP = 0.001, 0.9, 0.999, 1e-08, 0.01, 10
VMEM_LIMIT = 56 * 1024 * 1024
ROW_TILE = 512
MESH = pl.DeviceIdType.MESH
N_CHIPS = 4

GROUP_A = ("w_in", "w_out")
GROUP_B = ("w_mq", "w_mk", "w_mv", "w_mo")
GROUP_C = ("w_up", "w_down")
BIG = GROUP_A + GROUP_B + GROUP_C
NORMS = ("norm_mix_pre", "norm_mix_post", "norm_mem_pre", "norm_mem_post", "norm_memkv", "norm_mlp_pre", "norm_mlp_post")
SMALL = NORMS + ("conv_b", "conv_ln_g", "conv_ln_b", "b_forget")
WEIGHTS = ("norm_mix_pre", "norm_mix_post", "w_in", "b_forget", "conv_w", "conv_b", "conv_ln_g", "conv_ln_b", "w_out",
           "norm_mem_pre", "norm_mem_post", "norm_memkv", "w_mq", "w_mk", "w_mv", "w_mo", "norm_mlp_pre", "norm_mlp_post",
           "w_up", "w_down")

ANY = pl.BlockSpec(memory_space=pl.ANY)


def _sds(shape, dtype):
    return jax.ShapeDtypeStruct(shape, dtype)


class Comm:
    def __init__(self, ins, out_shapes, n_sems, start, finish, aliases=(), on_done=None):
        self.ins, self.out_shapes, self.n_sems = list(ins), list(out_shapes), n_sems
        self.start, self.finish, self.aliases, self.on_done = start, finish, tuple(aliases), on_done


def _call(body, *, name, grid, in_specs, out_specs, out_shape, args, scratch=(), sem=(), comms=(), aliases=None):
    comms = list(comms)
    n_in, n_out, n_scr = len(in_specs), len(out_specs), len(scratch)
    io_alias = dict(aliases or {})
    c_args, c_shapes, c_scratch = [], [], []
    for cm in comms:
        for i, o in cm.aliases:
            io_alias[n_in + len(c_args) + i] = n_out + len(c_shapes) + o
        c_args += cm.ins
        c_shapes += cm.out_shapes
        c_scratch += [pltpu.SemaphoreType.DMA((cm.n_sems,)), pltpu.SemaphoreType.DMA((cm.n_sems,))]
    rank = len(grid)

    def hosted(*refs):
        ins, c_in = refs[:n_in], refs[n_in:n_in + len(c_args)]
        outs = refs[n_in + len(c_args):n_in + len(c_args) + n_out]
        c_out = refs[n_in + len(c_args) + n_out:n_in + len(c_args) + n_out + len(c_shapes)]
        scr = refs[len(refs) - n_scr - 2 * len(comms):len(refs) - 2 * len(comms)]
        sems = refs[len(refs) - 2 * len(comms):]
        first = last = None
        for d in range(rank):
            f, e = pl.program_id(d) == 0, pl.program_id(d) == grid[d] - 1
            first = f if first is None else jnp.logical_and(first, f)
            last = e if last is None else jnp.logical_and(last, e)

        def each(which):
            a = b = 0
            for k, cm in enumerate(comms):
                getattr(cm, which)(c_in[a:a + len(cm.ins)], c_out[b:b + len(cm.out_shapes)], sems[2 * k], sems[2 * k + 1])
                a, b = a + len(cm.ins), b + len(cm.out_shapes)

        if rank:
            pl.when(first)(lambda: each("start"))
        else:
            each("start")
        body(*ins, *outs, *scr)
        if rank:
            pl.when(last)(lambda: each("finish"))
        else:
            each("finish")

    if comms:
        sem = ("arbitrary",) * rank
    params = pltpu.CompilerParams(dimension_semantics=sem, vmem_limit_bytes=VMEM_LIMIT) if rank else \
        pltpu.CompilerParams(vmem_limit_bytes=VMEM_LIMIT)
    kw = dict(grid=grid) if rank else {}
    res = pl.pallas_call(
        hosted if comms else body, name=name, in_specs=list(in_specs) + [ANY] * len(c_args),
        out_specs=list(out_specs) + [ANY] * len(c_shapes), out_shape=list(out_shape) + c_shapes,
        scratch_shapes=list(scratch) + c_scratch, input_output_aliases=io_alias, compiler_params=params, **kw)(*args, *c_args)
    base, rest = list(res[:n_out]), list(res[n_out:])
    for cm in comms:
        got, rest = rest[:len(cm.out_shapes)], rest[len(cm.out_shapes):]
        if cm.on_done is not None:
            cm.on_done(got)
    return base


def run_comm(comms, name):
    _call(lambda: None, name=name, grid=(), in_specs=[], out_specs=[], out_shape=[], args=[], comms=comms)


def _place():
    x, y, c = lax.axis_index("x"), lax.axis_index("y"), lax.axis_index("c")
    chips = [(1 - x, y), (x, 1 - y), (1 - x, 1 - y)]
    return x, y, c, chips


def _half(ref, h, lead=()):
    n = ref.shape[len(lead)] // 2
    return ref.at[(*lead, pl.ds(h * n, n))]


def _remote(src, dst, send_sems, recv_sems, k, to):
    return pltpu.make_async_remote_copy(src_ref=src, dst_ref=dst, send_sem=send_sems.at[k], recv_sem=recv_sems.at[k],
                                        device_id=to, device_id_type=MESH)


def gather_ici(shards, on_done):
    n = len(shards)

    def start(ins, outs, ss, rs):
        x, y, c, chips = _place()
        own = 2 * x + y
        for t in range(n):
            for k, chip in enumerate(chips):
                _remote(_half(ins[t], c), _half(outs[t], c, (own,)), ss, rs, 3 * t + k, (*chip, c)).start()

    def finish(ins, outs, ss, rs):
        x, y, c, chips = _place()
        for t in range(n):
            for k, (px, py) in enumerate(chips):
                cp = _remote(_half(ins[t], c), _half(outs[t], c, (2 * px + py,)), ss, rs, 3 * t + k, (px, py, c))
                cp.wait_recv()
                cp.wait_send()

    return Comm(shards, [_sds((N_CHIPS,) + a.shape, a.dtype) for a in shards], 3 * n, start, finish, on_done=on_done)


def gather_forward(bufs, on_done):
    n = len(bufs)

    def start(ins, outs, ss, rs):
        x, y, c, chips = _place()
        for t in range(n):
            for k, (px, py) in enumerate(chips):
                blk = _half(outs[t], c, (2 * px + py,))
                _remote(blk, blk, ss, rs, 3 * t + k, (x, y, 1 - c)).start()

    def finish(ins, outs, ss, rs):
        x, y, c, chips = _place()
        for t in range(n):
            for k, (px, py) in enumerate(chips):
                cp = _remote(_half(outs[t], c, (2 * px + py,)), _half(outs[t], 1 - c, (2 * px + py,)), ss, rs, 3 * t + k, (x, y, 1 - c))
                cp.wait_recv()
                cp.wait_send()

    return Comm(bufs, [_sds(b.shape, b.dtype) for b in bufs], 3 * n, start, finish,
                aliases=[(t, t) for t in range(n)], on_done=on_done)


def swap_halves(grads, on_done):
    n = len(grads)

    def copies(ins, outs, ss, rs):
        x, y, c, _ = _place()
        out = []
        for t in range(n):
            for k in range(N_CHIPS):
                out.append(_remote(_half(ins[t], 1 - c, (k,)), outs[t].at[k], ss, rs, N_CHIPS * t + k, (x, y, 1 - c)))
        return out

    def start(ins, outs, ss, rs):
        for cp in copies(ins, outs, ss, rs):
            cp.start()

    def finish(ins, outs, ss, rs):
        for cp in copies(ins, outs, ss, rs):
            cp.wait()

    return Comm(grads, [_sds((N_CHIPS, g.shape[1] // 2, g.shape[2]), g.dtype) for g in grads], N_CHIPS * n, start, finish,
                on_done=on_done)


def scatter_partials(parts, on_done):
    n = len(parts)

    def start(ins, outs, ss, rs):
        x, y, c, chips = _place()
        own = 2 * x + y
        for t in range(n):
            for k, (px, py) in enumerate(chips):
                _remote(ins[t].at[2 * px + py], outs[t].at[own], ss, rs, 3 * t + k, (px, py, c)).start()

    def finish(ins, outs, ss, rs):
        x, y, c, chips = _place()
        own = 2 * x + y
        for t in range(n):
            for k, (px, py) in enumerate(chips):
                cp = _remote(ins[t].at[own], outs[t].at[2 * px + py], ss, rs, 3 * t + k, (px, py, c))
                cp.wait_recv()
                cp.wait_send()

    return Comm(parts, [_sds(p.shape, p.dtype) for p in parts], 3 * n, start, finish, on_done=on_done)


def share_halves(fins, on_done):
    n = len(fins)

    def copies(outs, ss, rs, c, to):
        return [_remote(_half(outs[t], c), _half(outs[t], c), ss, rs, t, to) for t in range(n)]

    def start(ins, outs, ss, rs):
        x, y, c, _ = _place()
        for cp in copies(outs, ss, rs, c, (x, y, 1 - c)):
            cp.start()

    def finish(ins, outs, ss, rs):
        x, y, c, _ = _place()
        for t in range(n):
            cp = _remote(_half(outs[t], c), _half(outs[t], 1 - c), ss, rs, t, (x, y, 1 - c))
            cp.wait_recv()
            cp.wait_send()

    return Comm(fins, [_sds(f.shape, f.dtype) for f in fins], n, start, finish, aliases=[(t, t) for t in range(n)], on_done=on_done)


def pair_sum(g, other, place, name):
    _, r, c = g.shape
    h = r // 2
    tm = ROW_TILE if h % ROW_TILE == 0 else h
    nb = h // tm

    def body(pl_ref, g_ref, o_ref, own_ref, wire_ref):
        k = pl.program_id(1)
        val = g_ref[...] + o_ref[...]
        wire_ref[...] = val.astype(BF16)

        @pl.when(k == pl_ref[1])
        def _():
            own_ref[...] = val

    return pl.pallas_call(
        body, name=name,
        grid_spec=pltpu.PrefetchScalarGridSpec(
            num_scalar_prefetch=1, grid=(nb, N_CHIPS),
            in_specs=[pl.BlockSpec((None, tm, c), lambda i, k, p: (k, p[0] * nb + i, 0)),
                      pl.BlockSpec((None, tm, c), lambda i, k, p: (k, i, 0))],
            out_specs=[pl.BlockSpec((tm, c), lambda i, k, p: (i, 0)),
                       pl.BlockSpec((None, tm, c), lambda i, k, p: (k, i, 0))]),
        out_shape=[_sds((h, c), F32), _sds((N_CHIPS, h, c), BF16)],
        compiler_params=pltpu.CompilerParams(dimension_semantics=("parallel", "arbitrary"), vmem_limit_bytes=VMEM_LIMIT))(place, g, other)


def chip_sum(own, recv, place, name):
    h, c = own.shape
    tm = ROW_TILE if h % ROW_TILE == 0 else h
    nb = h // tm

    def body(pl_ref, own_ref, a_ref, b_ref, c_ref, out_ref):
        out_ref[...] = ((own_ref[...] + a_ref[...].astype(F32)) + b_ref[...].astype(F32)) + c_ref[...].astype(F32)

    slot = lambda d: pl.BlockSpec((None, tm, c), lambda i, p: ((p[1] + d) % N_CHIPS, i, 0))
    return pl.pallas_call(
        body, name=name,
        grid_spec=pltpu.PrefetchScalarGridSpec(
            num_scalar_prefetch=1, grid=(nb,),
            in_specs=[pl.BlockSpec((tm, c), lambda i, p: (i, 0)), slot(1), slot(2), slot(3)],
            out_specs=pl.BlockSpec((tm, c), lambda i, p: (p[0] * nb + i, 0))),
        out_shape=_sds((2 * h, c), F32),
        compiler_params=pltpu.CompilerParams(dimension_semantics=("parallel",), vmem_limit_bytes=VMEM_LIMIT))(place, own, recv, recv, recv)


def allreduce_small(packed, comms=()):
    rows, cols = packed.shape

    def body(in_ref, out_ref, buf_ref, send_sems, recv_sems):
        x, y, c, _ = _place()
        me = 4 * x + 2 * y + c
        buf_ref[me] = in_ref[...]
        peers = [(x ^ (k >> 2), y ^ ((k >> 1) & 1), c ^ (k & 1)) for k in range(1, 8)]
        cps = [_remote(in_ref, buf_ref.at[me], send_sems, recv_sems, k, peer) for k, peer in enumerate(peers)]
        for cp in cps:
            cp.start()
        for k, (px, py, pc) in enumerate(peers):
            _remote(in_ref, buf_ref.at[4 * px + 2 * py + pc], send_sems, recv_sems, k, (px, py, pc)).wait_recv()
        for cp in cps:
            cp.wait_send()
        total = buf_ref[0]
        for d in range(1, 8):
            total = total + buf_ref[d]
        out_ref[...] = total

    vm = pl.BlockSpec(memory_space=pltpu.VMEM)
    return _call(body, name="allreduce_small", grid=(), in_specs=[vm], out_specs=[vm], out_shape=[_sds((rows, cols), F32)],
                 args=[packed], comms=comms,
                 scratch=[pltpu.VMEM((8, rows, cols), F32), pltpu.SemaphoreType.DMA((7,)), pltpu.SemaphoreType.DMA((7,))])[0]


def rms_fwd(x, g, name):
    m, d = x.shape
    tm = min(m, ROW_TILE)

    def body(x_ref, g_ref, h_ref):
        xv = x_ref[...]
        r = lax.rsqrt(jnp.mean(xv * xv, axis=-1, keepdims=True) + EPS)
        h_ref[...] = ((xv * r) * g_ref[...]).astype(BF16)

    return _call(body, name=name, grid=(m // tm,),
                 in_specs=[pl.BlockSpec((tm, d), lambda i: (i, 0)), pl.BlockSpec((1, d), lambda i: (0, 0))],
                 out_specs=[pl.BlockSpec((tm, d), lambda i: (i, 0))], out_shape=[_sds((m, d), BF16)],
                 args=[x, g], sem=("parallel",))[0]


def resid_post(x, y, g, name):
    m, d = x.shape
    tm = min(m, ROW_TILE)

    def body(x_ref, y_ref, g_ref, o_ref):
        yv = y_ref[...]
        r = lax.rsqrt(jnp.mean(yv * yv, axis=-1, keepdims=True) + EPS)
        o_ref[...] = x_ref[...] + (yv * r) * g_ref[...]

    row = pl.BlockSpec((tm, d), lambda i: (i, 0))
    return _call(body, name=name, grid=(m // tm,), in_specs=[row, row, pl.BlockSpec((1, d), lambda i: (0, 0))],
                 out_specs=[row], out_shape=[_sds((m, d), F32)], args=[x, y, g], sem=("parallel",))[0]


def rms_bwd(y, g, dout, name, add=None, out_dtype=F32, comms=()):
    m, d = y.shape
    tm = min(m, ROW_TILE)
    has_add = add is not None

    def body(*refs):
        y_ref, g_ref, d_ref = refs[:3]
        dy_ref, dg_ref = refs[-2:]
        i = pl.program_id(0)
        yv = y_ref[...]
        dv = d_ref[...].astype(F32)
        r = lax.rsqrt(jnp.mean(yv * yv, axis=-1, keepdims=True) + EPS)
        gy = g_ref[...] * dv
        t = jnp.mean(yv * gy, axis=-1, keepdims=True) * (r * r)
        dy = r * (gy - yv * t)
        if has_add:
            dy = dy + refs[3][...]
        dy_ref[...] = dy.astype(dy_ref.dtype)
        part = jnp.sum(dv * (yv * r), axis=0, keepdims=True)

        @pl.when(i == 0)
        def _():
            dg_ref[...] = part

        @pl.when(i > 0)
        def _():
            dg_ref[...] += part

    row = pl.BlockSpec((tm, d), lambda i: (i, 0))
    vec = pl.BlockSpec((1, d), lambda i: (0, 0))
    return _call(body, name=name, grid=(m // tm,), in_specs=[row, vec, row] + ([row] if has_add else []),
                 out_specs=[row, vec], out_shape=[_sds((m, d), out_dtype), _sds((1, d), F32)],
                 args=[y, g, dout] + ([add] if has_add else []), sem=("arbitrary",), comms=comms)


def loss_head(xf, tgt, name):
    m, d = xf.shape
    tm = min(m, ROW_TILE)

    def body(x_ref, t_ref, s_ref, dx_ref):
        i = pl.program_id(0)
        err = x_ref[...] - t_ref[...]
        dx_ref[...] = err * (1.0 / d)
        part = jnp.sum(err * err, axis=0, keepdims=True)

        @pl.when(i == 0)
        def _():
            s_ref[...] = part

        @pl.when(i > 0)
        def _():
            s_ref[...] += part

    row = pl.BlockSpec((tm, d), lambda i: (i, 0))
    vec = pl.BlockSpec((1, d), lambda i: (0, 0))
    return _call(body, name=name, grid=(m // tm,), in_specs=[row, row], out_specs=[vec, row],
                 out_shape=[_sds((1, d), F32), _sds((m, d), F32)], args=[xf, tgt], sem=("arbitrary",))


_DOT_DIMS = {"nn": (((1,), (0,)), ((), ())), "nt": (((1,), (1,)), ((), ())), "tn": (((0,), (0,)), ((), ()))}


def matmul(a, b, *, mode, dims, tiles, name, out_shapes, out_specs=None, epi=None, extras=(), extra_specs=(),
           a_spec=None, b_spec=None, comms=(), aliases=None, store=None):
    m, n, k = dims
    tm, tn, tk = tiles
    nk = k // tk
    assert m % tm == 0 and n % tn == 0 and k % tk == 0
    n_ex, n_out = len(extras), len(out_shapes)
    if a_spec is None:
        a_spec = pl.BlockSpec((tk, tm), lambda i, j, kk: (kk, i)) if mode == "tn" else pl.BlockSpec((tm, tk), lambda i, j, kk: (i, kk))
    if b_spec is None:
        b_spec = pl.BlockSpec((tn, tk), lambda i, j, kk: (j, kk)) if mode == "nt" else pl.BlockSpec((tk, tn), lambda i, j, kk: (kk, j))
    if out_specs is None:
        out_specs = [pl.BlockSpec((tm, tn), lambda i, j, kk: (i, j)) for _ in out_shapes]
    if epi is None:
        epi = lambda acc: (acc,)

    def body(*refs):
        a_ref, b_ref = refs[0], refs[1]
        ex = refs[2:2 + n_ex]
        outs = refs[2 + n_ex:2 + n_ex + n_out]
        part = lax.dot_general(a_ref[...], b_ref[...], _DOT_DIMS[mode], preferred_element_type=F32)

        def finish(acc):
            for o_ref, val in zip(outs, epi(acc, *ex)):
                if store is None:
                    o_ref[...] = val.astype(o_ref.dtype)
                else:
                    store(o_ref, val)

        if nk == 1:
            finish(part)
        else:
            acc_ref = refs[-1]
            kk = pl.program_id(2)

            @pl.when(kk == 0)
            def _():
                acc_ref[...] = part

            @pl.when(kk > 0)
            def _():
                acc_ref[...] += part

            @pl.when(kk == nk - 1)
            def _():
                finish(acc_ref[...])

    return _call(body, name=name, grid=(m // tm, n // tn, nk), in_specs=[a_spec, b_spec] + list(extra_specs),
                 out_specs=list(out_specs), out_shape=list(out_shapes), args=[a, b, *extras],
                 scratch=[pltpu.VMEM((tm, tn), F32)] if nk > 1 else [], sem=("parallel", "parallel", "arbitrary"),
                 comms=comms, aliases=aliases)


def _sigmoid(v):
    return 1.0 / (1.0 + jnp.exp(-v))


def _shifted_copies(src_ref, dst_ref, rows):
    n = rows + CONV_HALO - 8
    for b in range(1, 8):
        dst_ref[b - 1, 0:n, :] = src_ref[b:b + n, :]


def _tap(src_ref, sh_ref, off, rows):
    lo = (off // 8) * 8
    return src_ref[lo:lo + rows, :] if off % 8 == 0 else sh_ref[off % 8 - 1, lo:lo + rows, :]


def conv_fwd(ag, cw, cb, lg, lb, name, comms=()):
    s = ag.shape[0]
    tm = min(s, ROW_TILE)
    c = CONV_CH
    hb = tm // CONV_HALO

    def body(ag_ref, halo_ref, w_ref, cb_ref, lg_ref, lb_ref, u1_ref, u_ref, ext_ref, sh_ref):
        i = pl.program_id(0)
        u0 = ag_ref[:, :c] * _sigmoid(ag_ref[:, c:])
        h0 = halo_ref[:, :c] * _sigmoid(halo_ref[:, c:])
        ext_ref[0:CONV_HALO, :] = jnp.where(i == 0, 0.0, h0)
        ext_ref[CONV_HALO:, :] = u0
        _shifted_copies(ext_ref, sh_ref, tm)
        acc = jnp.zeros((tm, c), F32) + cb_ref[...]
        for k in range(CONV_WIDTH):
            off = CONV_HALO - (CONV_WIDTH - 1) + k
            acc = acc + w_ref[k:k + 1, :] * _tap(ext_ref, sh_ref, off, tm)
        u1_ref[...] = acc
        mu = jnp.mean(acc, axis=-1, keepdims=True)
        xc = acc - mu
        rstd = lax.rsqrt(jnp.mean(xc * xc, axis=-1, keepdims=True) + EPS)
        u2 = (xc * rstd) * lg_ref[...] + lb_ref[...]
        u_ref[...] = (u2 * _sigmoid(u2)).astype(BF16)

    vec = pl.BlockSpec((1, c), lambda i: (0, 0))
    return _call(body, name=name, grid=(s // tm,),
                 in_specs=[pl.BlockSpec((tm, 2 * c), lambda i: (i, 0)),
                           pl.BlockSpec((CONV_HALO, 2 * c), lambda i: (jnp.maximum(i * hb - 1, 0), 0)),
                           pl.BlockSpec((CONV_HALO, c), lambda i: (0, 0)), vec, vec, vec],
                 out_specs=[pl.BlockSpec((tm, c), lambda i: (i, 0)), pl.BlockSpec((tm, c), lambda i: (i, 0))],
                 out_shape=[_sds((s, c), F32), _sds((s, 2 * c), BF16)],
                 args=[ag, ag, cw, cb, lg, lb],
                 scratch=[pltpu.VMEM((tm + CONV_HALO, c), F32), pltpu.VMEM((7, tm + CONV_HALO, c), F32)],
                 sem=("parallel",), comms=comms)


def conv_bwd(du, u1, ag, cw, lg, lb, name, comms=()):
    s = du.shape[0]
    tm = min(s, ROW_TILE)
    c = CONV_CH
    hb = tm // CONV_HALO
    nt = s // tm
    last_halo = s // CONV_HALO - 1

    def ln_silu_bwd(du_v, u1_v, lg_v, lb_v):
        mu = jnp.mean(u1_v, axis=-1, keepdims=True)
        xc = u1_v - mu
        rstd = lax.rsqrt(jnp.mean(xc * xc, axis=-1, keepdims=True) + EPS)
        xh = xc * rstd
        u2 = xh * lg_v + lb_v
        sg = _sigmoid(u2)
        du2 = du_v * (sg * (1.0 + u2 * (1.0 - sg)))
        dxh = du2 * lg_v
        du1 = rstd * (dxh - jnp.mean(dxh, axis=-1, keepdims=True) - xh * jnp.mean(dxh * xh, axis=-1, keepdims=True))
        return du1, du2, xh

    def body(du_ref, dun_ref, u1_ref, u1n_ref, ag_ref, agp_ref, w_ref, lg_ref, lb_ref, dag_ref, sm_ref, ext_ref, dext_ref,
             sh_ref, dsh_ref):
        i = pl.program_id(0)
        lg_v, lb_v = lg_ref[...], lb_ref[...]
        du1, du2, xh = ln_silu_bwd(du_ref[...], u1_ref[...], lg_v, lb_v)
        du1n, _, _ = ln_silu_bwd(dun_ref[...], u1n_ref[...], lg_v, lb_v)
        dext_ref[0:tm, :] = du1
        dext_ref[tm:, :] = jnp.where(i == nt - 1, 0.0, du1n)
        a, g = ag_ref[:, :c], ag_ref[:, c:]
        sg = _sigmoid(g)
        ext_ref[0:CONV_HALO, :] = jnp.where(i == 0, 0.0, agp_ref[:, :c] * _sigmoid(agp_ref[:, c:]))
        ext_ref[CONV_HALO:, :] = a * sg
        _shifted_copies(ext_ref, sh_ref, tm)
        _shifted_copies(dext_ref, dsh_ref, tm)

        @pl.when(i == 0)
        def _():
            sm_ref[...] = jnp.zeros_like(sm_ref)

        du0 = jnp.zeros((tm, c), F32)
        for k in range(CONV_WIDTH):
            back = CONV_WIDTH - 1 - k
            du0 = du0 + w_ref[k:k + 1, :] * _tap(dext_ref, dsh_ref, back, tm)
            off = CONV_HALO - (CONV_WIDTH - 1) + k
            sm_ref[k:k + 1, :] += jnp.sum(du1 * _tap(ext_ref, sh_ref, off, tm), axis=0, keepdims=True)
        sm_ref[32:33, :] += jnp.sum(du1, axis=0, keepdims=True)
        sm_ref[33:34, :] += jnp.sum(du2 * xh, axis=0, keepdims=True)
        sm_ref[34:35, :] += jnp.sum(du2, axis=0, keepdims=True)
        dag_ref[:, :c] = (du0 * sg).astype(BF16)
        dag_ref[:, c:] = (du0 * a * (sg * (1.0 - sg))).astype(BF16)

    vec = pl.BlockSpec((1, c), lambda i: (0, 0))
    tile = pl.BlockSpec((tm, c), lambda i: (i, 0))
    nxt = pl.BlockSpec((CONV_HALO, c), lambda i: (jnp.minimum((i + 1) * hb, last_halo), 0))
    return _call(body, name=name, grid=(nt,),
                 in_specs=[tile, nxt, tile, nxt,
                           pl.BlockSpec((tm, 2 * c), lambda i: (i, 0)),
                           pl.BlockSpec((CONV_HALO, 2 * c), lambda i: (jnp.maximum(i * hb - 1, 0), 0)),
                           pl.BlockSpec((CONV_HALO, c), lambda i: (0, 0)), vec, vec],
                 out_specs=[pl.BlockSpec((tm, 2 * c), lambda i: (i, 0)), pl.BlockSpec((40, c), lambda i: (0, 0))],
                 out_shape=[_sds((s, IN_PAD), BF16), _sds((40, c), F32)],
                 args=[du, du, u1, u1, ag, ag, cw, lg, lb],
                 scratch=[pltpu.VMEM((tm + CONV_HALO, c), F32), pltpu.VMEM((tm + CONV_HALO, c), F32),
                          pltpu.VMEM((7, tm + CONV_HALO, c), F32), pltpu.VMEM((7, tm + CONV_HALO, c), F32)],
                 sem=("arbitrary",), comms=comms)


CUM_BLOCK = 256


def _tri(n, upper):
    r = lax.broadcasted_iota(jnp.int32, (n, n), 0)
    cidx = lax.broadcasted_iota(jnp.int32, (n, n), 1)
    return jnp.where((r <= cidx) if upper else (r >= cidx), 1.0, 0.0).astype(F32)


def fox_gate_fwd(fl_t, bf, name):
    h, s = fl_t.shape
    nb = s // CUM_BLOCK

    def body(fl_ref, bf_ref, cum_ref):
        tri = _tri(CUM_BLOCK, True)
        carry = jnp.zeros((h, 1), F32)
        for b in range(nb):
            v = fl_ref[:, b * CUM_BLOCK:(b + 1) * CUM_BLOCK] + bf_ref[...]
            logf = jnp.minimum(v, 0.0) - jnp.log(1.0 + jnp.exp(-jnp.abs(v)))
            cs = jnp.dot(logf, tri, precision=lax.Precision.HIGHEST, preferred_element_type=F32) + carry
            cum_ref[:, b * CUM_BLOCK:(b + 1) * CUM_BLOCK] = cs
            carry = carry + jnp.sum(logf, axis=-1, keepdims=True)

    return pl.pallas_call(body, name=name, out_shape=_sds((h, s), F32),
                          compiler_params=pltpu.CompilerParams(vmem_limit_bytes=VMEM_LIMIT))(fl_t, bf)


def fox_gate_bwd(dcol_t, drow_t, fl_t, bf, name):
    h, s = fl_t.shape
    nb = s // CUM_BLOCK

    def body(dcol_ref, drow_ref, fl_ref, bf_ref, dfl_ref, dbf_ref):
        tri = _tri(CUM_BLOCK, False)
        carry = jnp.zeros((h, 1), F32)
        dbf = jnp.zeros((h, 1), F32)
        for b in reversed(range(nb)):
            sl = slice(b * CUM_BLOCK, (b + 1) * CUM_BLOCK)
            dcb = dcol_ref[:, sl] + drow_ref[:, sl]
            dlogf = jnp.dot(dcb, tri, precision=lax.Precision.HIGHEST, preferred_element_type=F32) + carry
            carry = carry + jnp.sum(dcb, axis=-1, keepdims=True)
            dfl = dlogf * _sigmoid(-(fl_ref[:, sl] + bf_ref[...]))
            dfl_ref[:, sl] = dfl
            dbf = dbf + jnp.sum(dfl, axis=-1, keepdims=True)
        dbf_ref[...] = dbf

    return pl.pallas_call(body, name=name, out_shape=[_sds((h, s), F32), _sds((h, 1), F32)],
                          compiler_params=pltpu.CompilerParams(vmem_limit_bytes=VMEM_LIMIT))(dcol_t, drow_t, fl_t, bf)


FOX_TILE = 512


def _causal_mask(tq, tk, q0, k0):
    row = lax.broadcasted_iota(jnp.int32, (tq, tk), 0) + q0
    col = lax.broadcasted_iota(jnp.int32, (tq, tk), 1) + k0
    return row >= col


def _fox_bias(c_ref, hh, q0, k0, t):
    c_q = jnp.max(c_ref[hh:hh + 1, pl.ds(q0, 128)], axis=-1, keepdims=True)
    return c_q - c_ref[hh:hh + 1, pl.ds(k0, t)]


def fox_fwd(qkv, cum4, cat, name, comms=()):
    s = qkv.shape[0]
    t = min(s, FOX_TILE)
    nq = s // t
    dn = _DOT_DIMS["nt"]

    def body(q_ref, k_ref, v_ref, c_ref, cat_ref, o_ref, lse_ref):
        i = pl.program_id(1)
        q0 = pl.multiple_of(i * t, t)
        lane = lax.broadcasted_iota(jnp.int32, (t, 128), 1)
        qv = q_ref[...] * FOX_SCALE
        zero = jnp.zeros_like(qv)
        q_h = (jnp.where(lane < 64, qv, zero), jnp.where(lane >= 64, qv, zero))

        def step(j, carry, masked):
            k0 = pl.multiple_of(j * t, t)
            kj = k_ref[pl.ds(k0, t), :]
            vj = v_ref[pl.ds(k0, t), :]
            out = []
            for hh in range(2):
                m_prev, l_prev, acc_prev = carry[hh]
                bias = _fox_bias(c_ref, hh, q0, k0, t)
                sc = lax.dot_general(q_h[hh], kj, dn, preferred_element_type=F32) + bias
                if masked:
                    sc = jnp.where(_causal_mask(t, t, q0, k0), sc, NEG_INF)
                m_new = jnp.maximum(m_prev, jnp.max(sc, axis=-1, keepdims=True))
                alpha = jnp.exp(m_prev - m_new)
                p = jnp.exp(sc - m_new)
                l_new = alpha * l_prev + jnp.sum(p, axis=-1, keepdims=True)
                acc_new = alpha * acc_prev + jnp.dot(p.astype(BF16), vj, preferred_element_type=F32)
                out.append((m_new, l_new, acc_new))
            return tuple(out)

        init = tuple((jnp.full((t, 1), NEG_INF, F32), jnp.zeros((t, 1), F32), jnp.zeros((t, 128), F32)) for _ in range(2))
        carry = lax.fori_loop(0, i, lambda j, cr: step(j, cr, False), init)
        carry = step(i, carry, True)
        (m_a, l_a, acc_a), (m_b, l_b, acc_b) = carry
        o_ref[...] = jnp.where(lane < 64, acc_a / l_a, acc_b / l_b).astype(BF16)
        lse_ref[0] = jnp.broadcast_to(m_a + jnp.log(l_a), (t, 128))
        lse_ref[1] = jnp.broadcast_to(m_b + jnp.log(l_b), (t, 128))

    return _call(body, name=name, grid=(4, nq),
                 in_specs=[pl.BlockSpec((t, 128), lambda p, i: (i, 3 * p)),
                           pl.BlockSpec((s, 128), lambda p, i: (0, 3 * p + 1)),
                           pl.BlockSpec((s, 128), lambda p, i: (0, 3 * p + 2)),
                           pl.BlockSpec((None, 2, s), lambda p, i: (p, 0, 0)), ANY],
                 out_specs=[pl.BlockSpec((t, 128), lambda p, i: (i, 4 + p)),
                            pl.BlockSpec((2, t, 128), lambda p, i: (p, i, 0))],
                 out_shape=[_sds((s, 2 * FOX_WIDTH), BF16), _sds((FOX_HEADS, s, 128), F32)],
                 args=[qkv, qkv, qkv, cum4, cat], sem=("parallel", "arbitrary"), comms=comms, aliases={4: 0})


def fox_bwd(qkv, cum4, cat, datt, lse, dz, name, comms=()):
    s = qkv.shape[0]
    t = min(s, FOX_TILE)
    nk = s // t
    nt_dims, tn_dims = _DOT_DIMS["nt"], _DOT_DIMS["tn"]

    def body(q_ref, k_ref, v_ref, c_ref, o_ref, do_ref, lse_ref, dz_in, dz_ref, dc_ref, dr_ref, dq_acc, dk_acc, dv_acc, dr_acc):
        j = pl.program_id(1)
        k0 = pl.multiple_of(j * t, t)
        lane = lax.broadcasted_iota(jnp.int32, (t, 128), 1)
        lo = lane < 64
        kj, vj = k_ref[...], v_ref[...]
        zero = jnp.zeros_like(kj)
        k_h = (jnp.where(lo, kj, zero), jnp.where(lo, zero, kj))

        @pl.when(j == 0)
        def _():
            dq_acc[...] = jnp.zeros_like(dq_acc)
            dr_acc[...] = jnp.zeros_like(dr_acc)

        dk_acc[...] = jnp.zeros_like(dk_acc)
        dv_acc[...] = jnp.zeros_like(dv_acc)

        def step(i, dc, masked):
            q0 = pl.multiple_of(i * t, t)
            qi = q_ref[pl.ds(q0, t), :]
            doi = do_ref[pl.ds(q0, t), :]
            prod = doi.astype(F32) * o_ref[pl.ds(q0, t), :].astype(F32)
            zq = jnp.zeros_like(qi)
            dq_new = jnp.zeros((t, 128), F32)
            dc_out = []
            for hh in range(2):
                sel = lo if hh == 0 else jnp.logical_not(lo)
                q_m = jnp.where(sel, qi * FOX_SCALE, zq)
                do_m = jnp.where(sel, doi, zq)
                delta = jnp.sum(jnp.where(sel, prod, 0.0), axis=-1, keepdims=True)
                bias = _fox_bias(c_ref, hh, q0, k0, t)
                sc = lax.dot_general(q_m, kj, nt_dims, preferred_element_type=F32) + bias
                if masked:
                    sc = jnp.where(_causal_mask(t, t, q0, k0), sc, NEG_INF)
                lse_t = jnp.tile(lse_ref[hh, pl.ds(q0, t), :], (1, t // 128))
                p = jnp.exp(sc - lse_t)
                dv_acc[hh] += lax.dot_general(p.astype(BF16), doi, tn_dims, preferred_element_type=F32)
                dp = lax.dot_general(do_m, vj, nt_dims, preferred_element_type=F32)
                ds = p * (dp - delta)
                dc_out.append(dc[hh] - jnp.sum(ds, axis=0, keepdims=True))
                dr_acc[hh, pl.ds(q0, t), :] += jnp.sum(ds, axis=-1, keepdims=True)
                ds_b = ds.astype(BF16)
                dq_new = dq_new + jnp.dot(ds_b, k_h[hh], preferred_element_type=F32)
                dk_acc[hh] += lax.dot_general(ds_b, qi, tn_dims, preferred_element_type=F32)
            dq_acc[pl.ds(q0, t), :] += dq_new
            return tuple(dc_out)

        dc = step(j, (jnp.zeros((1, t), F32), jnp.zeros((1, t), F32)), True)
        dc = lax.fori_loop(j + 1, nk, lambda i, cr: step(i, cr, False), dc)
        dz_ref[pl.ds(k0, t), 128:256] = (jnp.where(lo, dk_acc[0], dk_acc[1]) * FOX_SCALE).astype(BF16)
        dz_ref[pl.ds(k0, t), 256:384] = jnp.where(lo, dv_acc[0], dv_acc[1]).astype(BF16)
        dc_ref[0:1, :] = dc[0]
        dc_ref[1:2, :] = dc[1]

        @pl.when(j == nk - 1)
        def _():
            dz_ref[:, 0:128] = (dq_acc[...] * FOX_SCALE).astype(BF16)
            eye = lax.broadcasted_iota(jnp.int32, (t, t), 0) == lax.broadcasted_iota(jnp.int32, (t, t), 1)
            for hh in range(2):
                for b in range(nk):
                    col = dr_acc[hh, b * t:(b + 1) * t, :]
                    dr_ref[hh:hh + 1, b * t:(b + 1) * t] = jnp.sum(jnp.where(eye, col, 0.0), axis=0, keepdims=True)

    return _call(body, name=name, grid=(4, nk),
                 in_specs=[pl.BlockSpec((s, 128), lambda p, j: (0, 3 * p)),
                           pl.BlockSpec((t, 128), lambda p, j: (j, 3 * p + 1)),
                           pl.BlockSpec((t, 128), lambda p, j: (j, 3 * p + 2)),
                           pl.BlockSpec((None, 2, s), lambda p, j: (p, 0, 0)),
                           pl.BlockSpec((s, 128), lambda p, j: (0, 4 + p)),
                           pl.BlockSpec((s, 128), lambda p, j: (0, p)),
                           pl.BlockSpec((2, s, 128), lambda p, j: (p, 0, 0)), ANY],
                 out_specs=[pl.BlockSpec((s, 384), lambda p, j: (0, 3 + p)),
                            pl.BlockSpec((None, 2, t), lambda p, j: (p, 0, j)),
                            pl.BlockSpec((None, 2, s), lambda p, j: (p, 0, 0))],
                 out_shape=[_sds((s, IN_PAD), BF16), _sds((4, 2, s), F32), _sds((4, 2, s), F32)],
                 args=[qkv, qkv, qkv, cum4, cat, datt, lse, dz],
                 scratch=[pltpu.VMEM((s, 128), F32), pltpu.VMEM((2, t, 128), F32), pltpu.VMEM((2, t, 128), F32),
                          pltpu.VMEM((2, s, 1), F32)],
                 sem=("parallel", "arbitrary"), comms=comms, aliases={7: 0})


def _mem_probs(q_h, k_h):
    sc = lax.dot_general(q_h, k_h, _DOT_DIMS["nt"], preferred_element_type=F32) * MEM_SCALE
    e = jnp.exp(sc - jnp.max(sc, axis=-1, keepdims=True))
    return e / jnp.sum(e, axis=-1, keepdims=True)


def mem_attn_fwd(qm, km, vm, name):
    s = qm.shape[0]
    tm = min(s, ROW_TILE)

    def body(q_ref, k_ref, v_ref, o_ref):
        for h in range(MEM_HEADS):
            sl = slice(h * MEM_HEAD_DIM, (h + 1) * MEM_HEAD_DIM)
            p = _mem_probs(q_ref[:, sl], k_ref[:, sl])
            o_ref[:, sl] = jnp.dot(p.astype(BF16), v_ref[:, sl], preferred_element_type=F32).astype(BF16)

    kv = pl.BlockSpec((N_MEM, MEM_INNER), lambda i: (0, 0))
    row = pl.BlockSpec((tm, MEM_INNER), lambda i: (i, 0))
    return _call(body, name=name, grid=(s // tm,), in_specs=[row, kv, kv], out_specs=[row],
                 out_shape=[_sds((s, MEM_INNER), BF16)], args=[qm, km, vm], sem=("parallel",))[0]


def mem_attn_bwd(qm, km, vm, dom, name):
    s = qm.shape[0]
    tm = min(s, ROW_TILE)
    tn_dims = _DOT_DIMS["tn"]

    def body(q_ref, k_ref, v_ref, do_ref, dq_ref, dk_ref, dv_ref):
        i = pl.program_id(0)

        @pl.when(i == 0)
        def _():
            dk_ref[...] = jnp.zeros_like(dk_ref)
            dv_ref[...] = jnp.zeros_like(dv_ref)

        for h in range(MEM_HEADS):
            sl = slice(h * MEM_HEAD_DIM, (h + 1) * MEM_HEAD_DIM)
            q_h, k_h, do_h = q_ref[:, sl], k_ref[:, sl], do_ref[:, sl]
            p = _mem_probs(q_h, k_h)
            dp = lax.dot_general(do_h, v_ref[:, sl], _DOT_DIMS["nt"], preferred_element_type=F32)
            ds = p * (dp - jnp.sum(p * dp, axis=-1, keepdims=True))
            ds_b = (ds * MEM_SCALE).astype(BF16)
            dq_ref[:, sl] = jnp.dot(ds_b, k_h, preferred_element_type=F32).astype(BF16)
            dk_ref[:, sl] += lax.dot_general(ds_b, q_h, tn_dims, preferred_element_type=F32)
            dv_ref[:, sl] += lax.dot_general(p.astype(BF16), do_h, tn_dims, preferred_element_type=F32)

    kv = pl.BlockSpec((N_MEM, MEM_INNER), lambda i: (0, 0))
    row = pl.BlockSpec((tm, MEM_INNER), lambda i: (i, 0))
    return _call(body, name=name, grid=(s // tm,), in_specs=[row, kv, kv, row], out_specs=[row, kv, kv],
                 out_shape=[_sds((s, MEM_INNER), BF16), _sds((N_MEM, MEM_INNER), F32), _sds((N_MEM, MEM_INNER), F32)],
                 args=[qm, km, vm, dom], sem=("arbitrary",))


def _adam_update(w, g, m, v):
    c1 = 1.0 - ADAM_B1 ** ADAM_STEP
    c2 = 1.0 - ADAM_B2 ** ADAM_STEP
    nm = ADAM_B1 * m + (1.0 - ADAM_B1) * g
    nv = ADAM_B2 * v + (1.0 - ADAM_B2) * (g * g)
    return -ADAM_LR * ((nm / c1) / (jnp.sqrt(nv / c2) + ADAM_EPS) + ADAM_WD * w), nm, nv


def adamw(w, g, m, v, name, comms=()):
    r, c = w.shape
    tm = ROW_TILE if r % ROW_TILE == 0 else r

    def body(w_ref, g_ref, m_ref, v_ref, d_ref, nm_ref, nv_ref):
        d_ref[...], nm_ref[...], nv_ref[...] = _adam_update(w_ref[...], g_ref[...], m_ref[...], v_ref[...])

    blk = pl.BlockSpec((tm, c), lambda i: (i, 0))
    return _call(body, name=name, grid=(r // tm,), in_specs=[blk] * 4, out_specs=[blk] * 3,
                 out_shape=[_sds((r, c), F32)] * 3, args=[w, g, m, v], sem=("parallel",), comms=comms)


def adamw_layers(w, m, v, g0, g1, name, comms=()):
    _, r, c = w.shape
    tm = ROW_TILE if r % ROW_TILE == 0 else r

    def body(w_ref, m_ref, v_ref, g0_ref, g1_ref, d_ref, nm_ref, nv_ref, g_ref):
        g = jnp.where(pl.program_id(0) == 0, g0_ref[...], g1_ref[...])
        g_ref[...] = g
        d_ref[...], nm_ref[...], nv_ref[...] = _adam_update(w_ref[...], g, m_ref[...], v_ref[...])

    blk = pl.BlockSpec((None, tm, c), lambda l, i: (l, i, 0))
    return _call(body, name=name, grid=(2, r // tm),
                 in_specs=[blk, blk, blk, pl.BlockSpec((tm, c), lambda l, i: (i * (1 - l), 0)),
                           pl.BlockSpec((tm, c), lambda l, i: (i * l, 0))],
                 out_specs=[blk] * 4, out_shape=[_sds(w.shape, F32)] * 4, args=[w, m, v, g0, g1], sem=("parallel", "parallel"),
                 comms=comms)


def _vec(v):
    return v.reshape(1, -1)


def _hosted(hosts, name):
    make = hosts.get(name)
    return make() if make is not None else ()


def layer_fwd(x0, mem, w, sm, l, hosts):
    s = x0.shape[0]
    tm = min(s, ROW_TILE)
    nm = lambda base: f"{base}{l}"
    h1 = rms_fwd(x0, _vec(sm["norm_mix_pre"]), nm("rms_mix_pre"))
    row = lambda width: pl.BlockSpec((tm, width), lambda i, j, k: (i, 0))
    ag, flp, qkv = matmul(
        h1, w("w_in"), mode="nn", dims=(s, IN_PAD, D_MODEL), tiles=(tm, IN_PAD, D_MODEL), name=nm("mix_in"),
        out_shapes=[_sds((s, GATE_COL), F32), _sds((s, 128), F32), _sds((s, 3 * FOX_WIDTH), BF16)],
        out_specs=[row(GATE_COL), row(128), row(3 * FOX_WIDTH)],
        epi=lambda acc: (acc[:, :GATE_COL], acc[:, GATE_COL:QKV_COL], acc[:, QKV_COL:]))
    u1, cat = conv_fwd(ag, w("conv_w"), _vec(sm["conv_b"]), _vec(sm["conv_ln_g"]), _vec(sm["conv_ln_b"]), nm("conv_fwd"),
                       comms=_hosted(hosts, nm("conv_fwd")))
    fl_t = flp[:, :FOX_HEADS].T
    cum = fox_gate_fwd(fl_t, sm["b_forget"].reshape(FOX_HEADS, 1), nm("fox_gate_fwd"))
    cum4 = cum.reshape(4, 2, s)
    cat, lse = fox_fwd(qkv, cum4, cat, nm("fox_fwd"), comms=_hosted(hosts, nm("fox_fwd")))
    def post_epi(acc, x_ref, g_ref):
        r = lax.rsqrt(jnp.mean(acc * acc, axis=-1, keepdims=True) + EPS)
        return acc, x_ref[...] + (acc * r) * g_ref[...]

    post_specs = (pl.BlockSpec((tm, D_MODEL), lambda i, j, k: (i, 0)), pl.BlockSpec((1, D_MODEL), lambda i, j, k: (0, 0)))
    two_rows = [_sds((s, D_MODEL), F32), _sds((s, D_MODEL), F32)]
    y1, x1 = matmul(cat, w("w_out"), mode="nn", dims=(s, D_MODEL, D_MODEL), tiles=(tm, D_MODEL, D_MODEL), name=nm("mix_out"),
                    out_shapes=two_rows, epi=post_epi, extras=(x0, _vec(sm["norm_mix_post"])), extra_specs=post_specs,
                    comms=_hosted(hosts, nm("mix_out")))

    h2 = rms_fwd(x1, _vec(sm["norm_mem_pre"]), nm("rms_mem_pre"))
    mn = rms_fwd(mem, _vec(sm["norm_memkv"]), nm("rms_memkv"))
    qm, = matmul(h2, w("w_mq"), mode="nn", dims=(s, MEM_INNER, D_MODEL), tiles=(tm, MEM_INNER, D_MODEL), name=nm("mem_q"),
                 out_shapes=[_sds((s, MEM_INNER), BF16)], comms=_hosted(hosts, nm("mem_q")))
    km, = matmul(mn, w("w_mk"), mode="nn", dims=(N_MEM, MEM_INNER, D_MODEL), tiles=(N_MEM, MEM_INNER, D_MODEL), name=nm("mem_k"),
                 out_shapes=[_sds((N_MEM, MEM_INNER), BF16)])
    vm, = matmul(mn, w("w_mv"), mode="nn", dims=(N_MEM, MEM_INNER, D_MODEL), tiles=(N_MEM, MEM_INNER, D_MODEL), name=nm("mem_v"),
                 out_shapes=[_sds((N_MEM, MEM_INNER), BF16)])
    om = mem_attn_fwd(qm, km, vm, nm("mem_attn_fwd"))
    y2, x2 = matmul(om, w("w_mo"), mode="nn", dims=(s, D_MODEL, MEM_INNER), tiles=(tm, D_MODEL, MEM_INNER), name=nm("mem_o"),
                    out_shapes=two_rows, epi=post_epi, extras=(x1, _vec(sm["norm_mem_post"])), extra_specs=post_specs,
                    comms=_hosted(hosts, nm("mem_o")))

    h3 = rms_fwd(x2, _vec(sm["norm_mlp_pre"]), nm("rms_mlp_pre"))

    def relu2(acc):
        r = jnp.maximum(acc, 0.0)
        return (r * r,)

    act, = matmul(h3, w("w_up"), mode="nn", dims=(s, D_FF, D_MODEL), tiles=(tm, 1024, D_MODEL), name=nm("mlp_up"),
                  out_shapes=[_sds((s, D_FF), BF16)], epi=relu2,
                  b_spec=pl.BlockSpec((None, D_MODEL, 1024), lambda i, j, k: (j, 0, 0)), comms=_hosted(hosts, nm("mlp_up")))
    y3, x3 = matmul(act, w("w_down"), mode="nn", dims=(s, D_MODEL, D_FF), tiles=(tm, D_MODEL, 1024), name=nm("mlp_down"),
                    out_shapes=two_rows, epi=post_epi, extras=(x2, _vec(sm["norm_mlp_post"])), extra_specs=post_specs,
                    comms=_hosted(hosts, nm("mlp_down")))
    saved = dict(x0=x0, h1=h1, ag=ag, qkv=qkv, fl_t=fl_t, u1=u1, cum4=cum4, lse=lse, cat=cat, y1=y1, x1=x1,
                 h2=h2, mn=mn, qm=qm, km=km, vm=vm, om=om, y2=y2, x2=x2, h3=h3, act=act, y3=y3)
    return x3, saved


def layer_bwd(dx3, mem, w, sm, sv, l, hosts, gw):
    s = dx3.shape[0]
    tm = min(s, ROW_TILE)
    nm = lambda base: f"{base}{l}"
    gs = {}
    shards = lambda g: g.reshape(N_CHIPS, g.shape[0] // N_CHIPS, g.shape[1])
    col_shards = lambda g: jnp.moveaxis(g.reshape(g.shape[0], N_CHIPS, g.shape[1] // N_CHIPS), 1, 0)
    dy3, gs["norm_mlp_post"] = rms_bwd(sv["y3"], _vec(sm["norm_mlp_post"]), dx3, nm("bwd_post_mlp"), out_dtype=BF16)
    g, = matmul(sv["act"], dy3, mode="tn", dims=(D_FF, D_MODEL, s), tiles=(1024, D_MODEL, tm), name=nm("dw_down"),
                out_shapes=[_sds((D_FF, D_MODEL), F32)], comms=_hosted(hosts, nm("dw_down")))
    gw[("w_down", l)] = shards(g)

    def dup_epi(acc, act_ref):
        return (acc * (2.0 * jnp.sqrt(act_ref[...].astype(F32))),)

    dup, = matmul(dy3, w("w_down"), mode="nt", dims=(s, D_FF, D_MODEL), tiles=(tm, 1024, D_MODEL), name=nm("d_act"),
                  out_shapes=[_sds((s, D_FF), BF16)], epi=dup_epi, extras=(sv["act"],),
                  extra_specs=(pl.BlockSpec((tm, 1024), lambda i, j, k: (i, j)),), comms=_hosted(hosts, nm("d_act")))
    def store_shards(o_ref, val):
        for k in range(N_CHIPS):
            o_ref[k] = val[:, 1024 * k:1024 * (k + 1)]

    gw[("w_up", l)], = matmul(sv["h3"], dup, mode="tn", dims=(D_MODEL, D_FF, s), tiles=(512, D_FF, tm), name=nm("dw_up"),
                              out_shapes=[_sds((N_CHIPS, D_MODEL, 1024), F32)],
                              out_specs=[pl.BlockSpec((N_CHIPS, 512, 1024), lambda i, j, k: (0, i, 0))], store=store_shards)
    dh3, = matmul(dup, w("w_up"), mode="nt", dims=(s, D_MODEL, D_FF), tiles=(tm, D_MODEL, 1024), name=nm("d_h3"),
                  out_shapes=[_sds((s, D_MODEL), F32)],
                  b_spec=pl.BlockSpec((None, D_MODEL, 1024), lambda i, j, k: (k, 0, 0)), comms=_hosted(hosts, nm("d_h3")))
    dx2, gs["norm_mlp_pre"] = rms_bwd(sv["x2"], _vec(sm["norm_mlp_pre"]), dh3, nm("bwd_pre_mlp"), add=dx3)

    dy2, gs["norm_mem_post"] = rms_bwd(sv["y2"], _vec(sm["norm_mem_post"]), dx2, nm("bwd_post_mem"), out_dtype=BF16)
    g, = matmul(sv["om"], dy2, mode="tn", dims=(MEM_INNER, D_MODEL, s), tiles=(MEM_INNER, D_MODEL, tm), name=nm("dw_mo"),
                out_shapes=[_sds((MEM_INNER, D_MODEL), F32)])
    gw[("w_mo", l)] = col_shards(g)
    dom, = matmul(dy2, w("w_mo"), mode="nt", dims=(s, MEM_INNER, D_MODEL), tiles=(tm, MEM_INNER, D_MODEL), name=nm("d_om"),
                  out_shapes=[_sds((s, MEM_INNER), BF16)])
    dqm, dkm, dvm = mem_attn_bwd(sv["qm"], sv["km"], sv["vm"], dom, nm("mem_attn_bwd"))
    g, = matmul(sv["h2"], dqm, mode="tn", dims=(D_MODEL, MEM_INNER, s), tiles=(D_MODEL, MEM_INNER, tm), name=nm("dw_mq"),
                out_shapes=[_sds((D_MODEL, MEM_INNER), F32)])
    gw[("w_mq", l)] = shards(g)
    dkm_b, dvm_b = dkm.astype(BF16), dvm.astype(BF16)
    g, = matmul(sv["mn"], dkm_b, mode="tn", dims=(D_MODEL, MEM_INNER, N_MEM), tiles=(D_MODEL, MEM_INNER, N_MEM),
                name=nm("dw_mk"), out_shapes=[_sds((D_MODEL, MEM_INNER), F32)])
    gw[("w_mk", l)] = shards(g)
    g, = matmul(sv["mn"], dvm_b, mode="tn", dims=(D_MODEL, MEM_INNER, N_MEM), tiles=(D_MODEL, MEM_INNER, N_MEM),
                name=nm("dw_mv"), out_shapes=[_sds((D_MODEL, MEM_INNER), F32)])
    gw[("w_mv", l)] = shards(g)
    dmn_k, = matmul(dkm_b, w("w_mk"), mode="nt", dims=(N_MEM, D_MODEL, MEM_INNER), tiles=(N_MEM, D_MODEL, MEM_INNER),
                    name=nm("d_mn_k"), out_shapes=[_sds((N_MEM, D_MODEL), F32)])
    dmn, = matmul(dvm_b, w("w_mv"), mode="nt", dims=(N_MEM, D_MODEL, MEM_INNER), tiles=(N_MEM, D_MODEL, MEM_INNER),
                  name=nm("d_mn_v"), out_shapes=[_sds((N_MEM, D_MODEL), F32)],
                  epi=lambda acc, other: (acc + other[...],), extras=(dmn_k,),
                  extra_specs=(pl.BlockSpec((N_MEM, D_MODEL), lambda i, j, k: (0, 0)),))
    _, gs["norm_memkv"] = rms_bwd(mem, _vec(sm["norm_memkv"]), dmn, nm("bwd_memkv"))
    dh2, = matmul(dqm, w("w_mq"), mode="nt", dims=(s, D_MODEL, MEM_INNER), tiles=(tm, D_MODEL, MEM_INNER), name=nm("d_h2"),
                  out_shapes=[_sds((s, D_MODEL), F32)])
    dx1, gs["norm_mem_pre"] = rms_bwd(sv["x1"], _vec(sm["norm_mem_pre"]), dh2, nm("bwd_pre_mem"), add=dx2)

    dy1, gs["norm_mix_post"] = rms_bwd(sv["y1"], _vec(sm["norm_mix_post"]), dx1, nm("bwd_post_mix"), out_dtype=BF16)
    g, = matmul(sv["cat"], dy1, mode="tn", dims=(D_MODEL, D_MODEL, s), tiles=(D_MODEL, D_MODEL, tm), name=nm("dw_out"),
                out_shapes=[_sds((D_MODEL, D_MODEL), F32)])
    gw[("w_out", l)] = shards(g)
    du, datt = matmul(dy1, w("w_out"), mode="nt", dims=(s, D_MODEL, D_MODEL), tiles=(tm, D_MODEL, D_MODEL), name=nm("d_cat"),
                      out_shapes=[_sds((s, CONV_CH), F32), _sds((s, FOX_WIDTH), BF16)],
                      out_specs=[pl.BlockSpec((tm, CONV_CH), lambda i, j, k: (i, 0)), pl.BlockSpec((tm, FOX_WIDTH), lambda i, j, k: (i, 0))],
                      epi=lambda acc: (acc[:, :CONV_CH], acc[:, CONV_CH:]))
    dz, csm = conv_bwd(du, sv["u1"], sv["ag"], w("conv_w"), _vec(sm["conv_ln_g"]), _vec(sm["conv_ln_b"]), nm("conv_bwd"),
                       comms=_hosted(hosts, nm("conv_bwd")))
    gs["conv_b"], gs["conv_ln_g"], gs["conv_ln_b"] = csm[32:33], csm[33:34], csm[34:35]
    dz, dcol4, drow4 = fox_bwd(sv["qkv"], sv["cum4"], sv["cat"], datt, sv["lse"], dz, nm("fox_bwd"),
                               comms=_hosted(hosts, nm("fox_bwd")))
    dfl_t, dbf = fox_gate_bwd(dcol4.reshape(FOX_HEADS, s), drow4.reshape(FOX_HEADS, s), sv["fl_t"],
                              sm["b_forget"].reshape(FOX_HEADS, 1), nm("fox_gate_bwd"))
    gs["b_forget"] = dbf.reshape(1, FOX_HEADS)
    dflp = jnp.pad(dfl_t.T.astype(BF16), ((0, 0), (0, 128 - FOX_HEADS)))
    dz = lax.dynamic_update_slice(dz, dflp, (0, GATE_COL))
    g, = matmul(sv["h1"], dz, mode="tn", dims=(D_MODEL, IN_PAD, s), tiles=(512, IN_PAD, tm), name=nm("dw_in"),
                out_shapes=[_sds((D_MODEL, IN_PAD), F32)])
    gw[("w_in", l)] = col_shards(_in_cols_back(g))
    dh1, = matmul(dz, w("w_in"), mode="nt", dims=(s, D_MODEL, IN_PAD), tiles=(tm, D_MODEL, IN_PAD), name=nm("d_h1"),
                  out_shapes=[_sds((s, D_MODEL), F32)], comms=_hosted(hosts, nm("d_h1")))
    dx0, gs["norm_mix_pre"] = rms_bwd(sv["x0"], _vec(sm["norm_mix_pre"]), dh1, nm("bwd_pre_mix"), add=dx1,
                                      comms=_hosted(hosts, nm("bwd_pre_mix")))
    return dx0, gs, csm[:CONV_WIDTH]


def _in_cols(w):
    c2, fw = 2 * CONV_CH, FOX_WIDTH
    parts = [w[:, :c2], w[:, c2 + 3 * fw:], jnp.zeros((w.shape[0], 128 - FOX_HEADS), w.dtype)]
    for p in range(4):
        parts += [w[:, c2 + part * fw + 128 * p:c2 + part * fw + 128 * (p + 1)] for part in range(3)]
    return jnp.concatenate(parts, axis=1)


def _in_cols_back(g):
    c2 = 2 * CONV_CH
    qkv = [[g[:, QKV_COL + 384 * p + 128 * part:QKV_COL + 384 * p + 128 * (part + 1)] for p in range(4)] for part in range(3)]
    return jnp.concatenate([g[:, :c2]] + [blk for part in qkv for blk in part] + [g[:, c2:c2 + FOX_HEADS]], axis=1)


def pack_small_grads(gss, gconv, loss):
    both = lambda f: jnp.concatenate([f(0), f(1)], axis=0)
    rows = [both(lambda l, nm=nm: gss[l][nm]) for nm in NORMS]
    rows.append(both(lambda l: jnp.concatenate([gss[l]["conv_b"], gss[l]["conv_ln_g"]], axis=1)))
    rows.append(both(lambda l: jnp.concatenate([gss[l]["conv_ln_b"], gss[l]["b_forget"],
                                                jnp.zeros((1, D_MODEL - CONV_CH - FOX_HEADS), F32)], axis=1)))
    loss_row = jnp.concatenate([loss.reshape(1, 1), jnp.zeros((1, D_MODEL - 1), F32)], axis=1)
    rows += [jnp.zeros((5, D_MODEL), F32), loss_row, jnp.concatenate([gconv[0], gconv[1]], axis=1), jnp.zeros((1, D_MODEL), F32)]
    return jnp.concatenate(rows, axis=0)


def unpack_small_grads(p):
    out = {nm: p[2 * idx:2 * idx + 2] for idx, nm in enumerate(NORMS)}
    out["loss"] = p[23, 0]
    out["conv_b"], out["conv_ln_g"] = p[14:16, :CONV_CH], p[14:16, CONV_CH:]
    out["conv_ln_b"], out["b_forget"] = p[16:18, :CONV_CH], p[16:18, CONV_CH:CONV_CH + FOX_HEADS]
    out["conv_w"] = jnp.stack([p[24:24 + CONV_WIDTH, :CONV_CH], p[24:24 + CONV_WIDTH, CONV_CH:]])
    return out


def kernel(x, mem, norm_mix_pre, norm_mix_post, w_in, b_forget, conv_w, conv_b, conv_ln_g, conv_ln_b, w_out, norm_mem_pre, norm_mem_post, norm_memkv, w_mq, w_mk, w_mv, w_mo, norm_mlp_pre, norm_mlp_post, w_up, w_down, loss_target, m_norm_mix_pre, m_norm_mix_post, m_w_in, m_b_forget, m_conv_w, m_conv_b, m_conv_ln_g, m_conv_ln_b, m_w_out, m_norm_mem_pre, m_norm_mem_post, m_norm_memkv, m_w_mq, m_w_mk, m_w_mv, m_w_mo, m_norm_mlp_pre, m_norm_mlp_post, m_w_up, m_w_down, v_norm_mix_pre, v_norm_mix_post, v_w_in, v_b_forget, v_conv_w, v_conv_b, v_conv_ln_g, v_conv_ln_b, v_w_out, v_norm_mem_pre, v_norm_mem_post, v_norm_memkv, v_w_mq, v_w_mk, v_w_mv, v_w_mo, v_norm_mlp_pre, v_norm_mlp_post, v_w_up, v_w_down):
    args = dict(locals())
    wts = {n: args[n] for n in WEIGHTS}
    mom = {n: args["m_" + n] for n in WEIGHTS}
    var = {n: args["v_" + n] for n in WEIGHTS}
    own = 2 * lax.axis_index("x") + lax.axis_index("y")
    place = jnp.stack([lax.axis_index("c"), own]).astype(jnp.int32)
    items = lambda group, l: [(n, l) for n in group]

    shard = {(n, l): wts[n][l].astype(BF16) for n in BIG for l in range(2)}
    shard[("conv_w", 0)] = wts["conv_w"]
    gath, full = {}, {}

    def gather_stage(keys):
        def done(res):
            for key, buf in zip(keys, res):
                gath[key] = lax.dynamic_update_slice(buf, shard[key][None], (own,) + (0,) * shard[key].ndim)
        return gather_ici([shard[k] for k in keys], done)

    def forward_stage(keys):
        def done(res):
            gath.update(zip(keys, res))
        return gather_forward([gath[k] for k in keys], done)

    def weight(l):
        def get(n):
            if (n, l) not in full:
                if n == "conv_w":
                    g = gath[("conv_w", 0)][:, l]
                    full[(n, l)] = jnp.pad(jnp.moveaxis(g, 0, 1).reshape(CONV_WIDTH, CONV_CH), ((0, CONV_HALO - CONV_WIDTH), (0, 0)))
                elif n == "w_in":
                    g = gath[(n, l)]
                    full[(n, l)] = _in_cols(jnp.moveaxis(g, 0, 1).reshape(D_MODEL, IN_COLS))
                elif n == "w_mo":
                    g = gath[(n, l)]
                    full[(n, l)] = jnp.moveaxis(g, 0, 1).reshape(MEM_INNER, D_MODEL)
                elif n == "w_up":
                    full[(n, l)] = gath[(n, l)]
                else:
                    g = gath[(n, l)]
                    full[(n, l)] = g.reshape(N_CHIPS * g.shape[1], g.shape[2])
            return full[(n, l)]
        return get

    first = [("w_in", 0), ("conv_w", 0)]
    run_comm([gather_stage(first)], "gather_first")
    run_comm([forward_stage(first)], "forward_first")
    mem0w, mlp0w = items(GROUP_B, 0), items(GROUP_C, 0)
    ab1 = items(GROUP_A, 1) + items(GROUP_B, 1)
    fwd_hosts = [
        {"conv_fwd0": lambda: [gather_stage([("w_out", 0)])],
         "fox_fwd0": lambda: [gather_stage(mem0w + mlp0w + ab1), forward_stage([("w_out", 0)])],
         "mix_out0": lambda: [forward_stage(mem0w)],
         "mem_q0": lambda: [forward_stage(mlp0w)],
         "mem_o0": lambda: [forward_stage(ab1)],
         "mlp_up0": lambda: [gather_stage([("w_up", 1)])],
         "mlp_down0": lambda: [gather_stage([("w_down", 1)]), forward_stage([("w_up", 1)])]},
        {"conv_fwd1": lambda: [forward_stage([("w_down", 1)])]},
    ]

    gw, swapped, own_sum, wire, recvd, fin = {}, {}, {}, {}, {}, {}

    def swap_stage(keys):
        return swap_halves([gw[k] for k in keys], lambda res: swapped.update(zip(keys, res)))

    def scatter_stage(keys):
        for n, l in keys:
            own_sum[(n, l)], wire[(n, l)] = pair_sum(gw[(n, l)], swapped[(n, l)], place, f"pair_sum_{n}{l}")
        return scatter_partials([wire[k] for k in keys], lambda res: recvd.update(zip(keys, res)))

    def share_stage(keys):
        for n, l in keys:
            fin[(n, l)] = chip_sum(own_sum[(n, l)], recvd[(n, l)], place, f"chip_sum_{n}{l}")
        return share_halves([fin[k] for k in keys], lambda res: fin.update(zip(keys, res)))

    a0, b0, c0 = [("w_in", 0)], items(GROUP_B, 0) + [("w_out", 0)], items(GROUP_C, 0)
    a1, b1, c1 = [("w_in", 1)], items(GROUP_B, 1) + [("w_out", 1)], items(GROUP_C, 1)
    bwd_hosts = [
        {"dw_down0": lambda: [scatter_stage(a1)],
         "d_act0": lambda: [share_stage(a1)],
         "d_h30": lambda: [swap_stage(c0)],
         "conv_bwd0": lambda: [scatter_stage(c0), swap_stage(b0)],
         "fox_bwd0": lambda: [share_stage(c0), scatter_stage(b0)],
         "d_h10": lambda: [share_stage(b0), swap_stage(a0)]},
        {"d_h31": lambda: [swap_stage(c1)],
         "conv_bwd1": lambda: [scatter_stage(c1), swap_stage(b1)],
         "fox_bwd1": lambda: [share_stage(c1), scatter_stage(b1)],
         "d_h11": lambda: [share_stage(b1), swap_stage(a1)]},
    ]

    sml = [{n: wts[n][l] for n in SMALL} for l in range(2)]
    mem0 = mem[0]
    saved, h = [], x[0]
    for l in range(2):
        h, sv = layer_fwd(h, mem0, weight(l), sml[l], l, fwd_hosts[l])
        saved.append(sv)
    sq, dx = loss_head(h, loss_target[0], "loss_head")
    gss, gconv = [None, None], [None, None]
    for l in (1, 0):
        dx, gss[l], gconv[l] = layer_bwd(dx, mem0, weight(l), sml[l], saved[l], l, bwd_hosts[l], gw)

    grads, delta, new_m, new_v = {}, {}, {}, {}
    packed = pack_small_grads(gss, gconv, 0.5 * jnp.sum(sq) / D_MODEL)
    g_small = unpack_small_grads(allreduce_small(packed, comms=[scatter_stage(a0)]))
    loss = g_small["loss"]
    taps = lambda a: a.reshape(2 * CONV_WIDTH, CONV_CH // N_CHIPS)
    grads["conv_w"] = lax.dynamic_slice_in_dim(g_small["conv_w"], own * (CONV_CH // N_CHIPS), CONV_CH // N_CHIPS, axis=2)
    d_, m_, v_ = adamw(taps(wts["conv_w"]), taps(grads["conv_w"]), taps(mom["conv_w"]), taps(var["conv_w"]), "adamw_conv_w",
                       comms=[share_stage(a0)])
    delta["conv_w"], new_m["conv_w"], new_v["conv_w"] = (a.reshape(wts["conv_w"].shape) for a in (d_, m_, v_))
    for n in SMALL:
        grads[n] = g_small[n]
        delta[n], new_m[n], new_v[n] = adamw(wts[n], grads[n], mom[n], var[n], "adamw_" + n)
    for n in BIG:
        delta[n], new_m[n], new_v[n], grads[n] = adamw_layers(wts[n], mom[n], var[n], fin[(n, 0)], fin[(n, 1)], "adamw_" + n)

    return (loss, dx[None], *[grads[n] for n in WEIGHTS], *[delta[n] for n in WEIGHTS],
            *[new_m[n] for n in WEIGHTS], *[new_v[n] for n in WEIGHTS])
```
